```python
import math
import jax
import jax.numpy as jnp
from jax import lax
import numpy as np

D_MODEL = 2048
BATCH = 4
SEQ = 2048
DEPTH = 2
DEC_BATCH = 128
DEC_SEQ = 4
PAST_LEN = 8192
PAGE_SIZE = 128

MLA_HEADS = 16
Q_LORA = 512
KV_LORA = 512
QK_NOPE = 128
QK_ROPE = 64
V_HEAD = 128
ROPE_THETA = 10000.0
MLA_SCALE = (QK_NOPE + QK_ROPE) ** -0.5
MLA_IN = Q_LORA + KV_LORA + QK_ROPE

DSA_HEADS = 16
DSA_KV_HEADS = 4
DSA_GROUP = DSA_HEADS // DSA_KV_HEADS
DSA_HEAD_DIM = 128
DSA_SCALE = DSA_HEAD_DIM ** -0.5
IDX_HEADS = 16
IDX_DIM = 64
IDX_ROPE = 32
IDX_SCALE = IDX_DIM ** -0.5
TOPK_MAX = 256
_DQ = DSA_HEADS * DSA_HEAD_DIM
_DKV = DSA_KV_HEADS * DSA_HEAD_DIM
_DIQ = IDX_HEADS * IDX_DIM
DSA_SPLITS = (_DQ, _DQ + _DKV, _DQ + 2 * _DKV, _DQ + 2 * _DKV + _DIQ, _DQ + 2 * _DKV + _DIQ + IDX_DIM)
DSA_IN = DSA_SPLITS[-1] + IDX_HEADS

REL_BUCKETS = 32
REL_MAX_DIST = 128

D_FF = -(-8 * D_MODEL // (3 * 256)) * 256

N_MLA_LAYERS = (DEPTH + 1) // 2
N_DSA_LAYERS = DEPTH // 2
DEEPNORM_ALPHA = (2 * DEPTH) ** 0.25
DEEPNORM_BETA = (8 * DEPTH) ** -0.25
Q_BLOCK = 128
LN_EPS = 1e-5
RMS_EPS = 1e-6

kernel_name = 'hybrid_mla_dsa_deepnorm_adaln_step'


def layer_norm(x, g, b):
    xf = x.astype(jnp.float32)
    mu = jnp.mean(xf, -1, keepdims=True)
    var = jnp.mean(jnp.square(xf - mu), -1, keepdims=True)
    return ((xf - mu) * lax.rsqrt(var + LN_EPS) * g.astype(jnp.float32) + b.astype(jnp.float32)).astype(x.dtype)


def rms_norm(x, g):
    xf = x.astype(jnp.float32)
    y = xf * lax.rsqrt(jnp.mean(jnp.square(xf), -1, keepdims=True) + RMS_EPS)
    return (y * g.astype(jnp.float32)).astype(x.dtype)


def rope(x, pos):
    half = x.shape[-1] // 2
    freq = ROPE_THETA ** (-jnp.arange(half, dtype=jnp.float32) / half)
    ang = pos.astype(jnp.float32)[:, None] * freq[None, :]
    ang = ang.reshape(ang.shape[0], *([1] * (x.ndim - 3)), half)
    cos, sin = jnp.cos(ang), jnp.sin(ang)
    xf = x.astype(jnp.float32)
    x1, x2 = xf[..., :half], xf[..., half:]
    return jnp.concatenate([x1 * cos - x2 * sin, x1 * sin + x2 * cos], -1).astype(x.dtype)


def partial_rope(x, pos, n_rot):
    return jnp.concatenate([rope(x[..., :n_rot], pos), x[..., n_rot:]], -1)


def t5_bucket(dist):
    dist = jnp.maximum(dist, 0)
    exact = REL_BUCKETS // 2
    d = jnp.maximum(dist, 1).astype(jnp.float32)
    large = exact + (jnp.log(d / exact) / math.log(REL_MAX_DIST / exact) * (REL_BUCKETS - exact)).astype(jnp.int32)
    large = jnp.minimum(large, REL_BUCKETS - 1)
    return jnp.where(dist < exact, dist, large)


def take_rows(a, idx):
    return jax.vmap(lambda r, i: r[i])(a, idx)


def to_blocks(a):
    b, t = a.shape[:2]
    return a.reshape(b, t // Q_BLOCK, Q_BLOCK, *a.shape[2:]).swapaxes(0, 1)


def from_blocks(o):
    nb, b = o.shape[:2]
    return o.swapaxes(0, 1).reshape(b, nb * Q_BLOCK, *o.shape[3:])


def ada_mod(c, w, b):
    m = (jax.nn.silu(c) @ w + b)[:, None, :]
    return jnp.split(m, 6, axis=-1)


def modulate(y, shift, scale):
    return y * (1 + scale) + shift


def post_norm(y, h, gate, g, b):
    return layer_norm(DEEPNORM_ALPHA * y + gate * h, g, b)


def swiglu(u, w_gu, w_down):
    g, up = jnp.split(u @ w_gu, 2, axis=-1)
    return (jax.nn.silu(g) * up) @ w_down


def mla_project(u, pos, w_in, q_norm_g, kv_norm_g, w_uq, w_uk):
    b, t, _ = u.shape
    cq, ckv, kpe = jnp.split(u @ w_in, [Q_LORA, Q_LORA + KV_LORA], axis=-1)
    cq = rms_norm(cq, q_norm_g)
    ckv = rms_norm(ckv, kv_norm_g)
    kpe = rope(kpe, pos)
    q = (cq @ w_uq).reshape(b, t, MLA_HEADS, QK_NOPE + QK_ROPE)
    q_pe = rope(q[..., QK_NOPE:], pos)
    q_lat = jnp.einsum('bthn,chn->bthc', q[..., :QK_NOPE], w_uk)
    return q_lat, q_pe, ckv, kpe


def mla_attend(q_lat, q_pe, ckv, kpe, q_pos, k_pos):
    s = jnp.einsum('bthc,bsc->bhts', q_lat, ckv) + jnp.einsum('bthr,bsr->bhts', q_pe, kpe)
    s = s.astype(jnp.float32) * MLA_SCALE
    s = jnp.where(k_pos[None, None, None, :] <= q_pos[None, None, :, None], s, -jnp.inf)
    p = jax.nn.softmax(s, axis=-1).astype(ckv.dtype)
    return jnp.einsum('bhts,bsc->bthc', p, ckv)


def mla_out(o_lat, w_uv, w_o):
    o = jnp.einsum('bthc,chv->bthv', o_lat, w_uv)
    return o.reshape(o.shape[0], o.shape[1], -1) @ w_o


def mla_prompt(u, w_in, q_norm_g, kv_norm_g, w_uq, w_uk, w_uv, w_o):
    t = u.shape[1]
    pos = jnp.arange(t, dtype=jnp.int32)
    q_lat, q_pe, ckv, kpe = mla_project(u, pos, w_in, q_norm_g, kv_norm_g, w_uq, w_uk)
    o_lat = lax.map(lambda a: mla_attend(a[0], a[1], ckv, kpe, a[2], pos),
                    (to_blocks(q_lat), to_blocks(q_pe), pos.reshape(-1, Q_BLOCK)))
    return mla_out(from_blocks(o_lat), w_uv, w_o), ckv, kpe


def mla_sample(u, page_table, pool_ckv, pool_kpe, j, w_in, q_norm_g, kv_norm_g, w_uq, w_uk, w_uv, w_o):
    t = u.shape[1]
    past = page_table.shape[1] * pool_ckv.shape[2]
    q_pos = past + jnp.arange(t, dtype=jnp.int32)
    k_pos = jnp.arange(past + t, dtype=jnp.int32)
    q_lat, q_pe, ckv, kpe = mla_project(u, q_pos, w_in, q_norm_g, kv_norm_g, w_uq, w_uk)

    def one_seq(a):
        pt, ql, qp, cn, kn = a
        ckv_all = jnp.concatenate([pool_ckv[j, pt].reshape(past, KV_LORA), cn], axis=0)
        kpe_all = jnp.concatenate([pool_kpe[j, pt].reshape(past, QK_ROPE), kn], axis=0)
        return mla_attend(ql[None], qp[None], ckv_all[None], kpe_all[None], q_pos, k_pos)[0]

    o_lat = lax.map(one_seq, (page_table, q_lat, q_pe, ckv, kpe))
    return mla_out(o_lat, w_uv, w_o), ckv, kpe


def dsa_project(u, pos, w_in):
    b, t, _ = u.shape
    q, k, v, qi, ki, wi = jnp.split(u @ w_in, DSA_SPLITS, axis=-1)
    q = q.reshape(b, t, DSA_KV_HEADS, DSA_GROUP, DSA_HEAD_DIM)
    k = k.reshape(b, t, DSA_KV_HEADS, DSA_HEAD_DIM)
    v = v.reshape(b, t, DSA_KV_HEADS, DSA_HEAD_DIM)
    qi = partial_rope(qi.reshape(b, t, IDX_HEADS, IDX_DIM), pos, IDX_ROPE)
    ki = partial_rope(ki, pos, IDX_ROPE)
    wi = wi * (IDX_HEADS ** -0.5)
    return q, k, v, qi, ki, wi


def index_topk(qi, wi, ki, q_pos, k_pos, k_sel):
    dots = jnp.einsum('bthd,bsd->bths', qi, ki) * IDX_SCALE
    score = jnp.einsum('bths,bth->bts', jax.nn.relu(dots), wi).astype(jnp.float32)
    score = jnp.where(k_pos[None, None, :] <= q_pos[None, :, None], score, -jnp.inf)
    return lax.top_k(score, k_sel)[1]


def dsa_attend(q, k_g, v_g, sel_pos, q_pos, rel_bias):
    b, t = q.shape[:2]
    s = jnp.einsum('btngd,btknd->btngk', q, k_g).astype(jnp.float32) * DSA_SCALE
    bucket = t5_bucket(q_pos[None, :, None] - sel_pos)
    bias = rel_bias[bucket].astype(jnp.float32)
    bias = bias.reshape(*bucket.shape, DSA_KV_HEADS, DSA_GROUP).transpose(0, 1, 3, 4, 2)
    valid = (sel_pos <= q_pos[None, :, None])[:, :, None, None, :]
    p = jax.nn.softmax(jnp.where(valid, s + bias, -jnp.inf), axis=-1).astype(v_g.dtype)
    return jnp.einsum('btngk,btknd->btngd', p, v_g).reshape(b, t, -1)


def dsa_prompt(u, w_in, w_o, rel_bias):
    t = u.shape[1]
    pos = jnp.arange(t, dtype=jnp.int32)
    q, k, v, qi, ki, wi = dsa_project(u, pos, w_in)
    k_sel = min(TOPK_MAX, t // 4)

    def block(a):
        qb, qib, wib, pb = a
        sel = index_topk(qib, wib, ki, pb, pos, k_sel)
        return dsa_attend(qb, take_rows(k, sel), take_rows(v, sel), sel, pb, rel_bias)

    o = from_blocks(lax.map(block, (to_blocks(q), to_blocks(qi), to_blocks(wi), pos.reshape(-1, Q_BLOCK))))
    return o @ w_o, k, v, ki


def dsa_sample(u, page_table, pool_k, pool_v, pool_ik, j, w_in, w_o, rel_bias):
    b, t, _ = u.shape
    page = pool_k.shape[2]
    past = page_table.shape[1] * page
    q_pos = past + jnp.arange(t, dtype=jnp.int32)
    k_pos = jnp.arange(past + t, dtype=jnp.int32)
    q, k, v, qi, ki, wi = dsa_project(u, q_pos, w_in)
    ki_all = jnp.concatenate([pool_ik[j, page_table].reshape(b, past, IDX_DIM), ki], axis=1)
    sel = index_topk(qi, wi, ki_all, q_pos, k_pos, min(TOPK_MAX, (past + t) // 4))
    in_past = (sel < past)[..., None, None]
    sp = jnp.minimum(sel, past - 1)
    phys = jnp.take_along_axis(page_table, (sp // page).reshape(b, -1), axis=1).reshape(sel.shape)
    off = sp % page
    new_idx = jnp.clip(sel - past, 0, t - 1)
    k_g = jnp.where(in_past, pool_k[j, phys, off], take_rows(k, new_idx))
    v_g = jnp.where(in_past, pool_v[j, phys, off], take_rows(v, new_idx))
    o = dsa_attend(q, k_g, v_g, sel, q_pos, rel_bias)
    return o @ w_o, k, v, ki


def setup_inputs(seed: int = 0) -> dict:
    key = jax.random.key(seed)
    ks = iter(jax.random.split(key, 40))
    n_pages = PAST_LEN // PAGE_SIZE
    n_used = DEC_BATCH * n_pages
    n_pool = n_used + -(-n_used // 4)

    def nrm(shape, scale=1.0):
        return jax.random.normal(next(ks), shape, jnp.float32) * scale

    def gain(shape):
        return 1.0 + nrm(shape, 0.02)

    page_table = jax.random.permutation(next(ks), n_pool)[:n_used].reshape(DEC_BATCH, n_pages).astype(jnp.int32)
    return {
        'x_prompt': nrm((BATCH, SEQ, D_MODEL)),
        'x_sample': nrm((DEC_BATCH, DEC_SEQ, D_MODEL)),
        'cache_mla_ckv': nrm((N_MLA_LAYERS, n_pool, PAGE_SIZE, KV_LORA)),
        'cache_mla_kpe': nrm((N_MLA_LAYERS, n_pool, PAGE_SIZE, QK_ROPE)),
        'cache_dsa_k': nrm((N_DSA_LAYERS, n_pool, PAGE_SIZE, DSA_KV_HEADS, DSA_HEAD_DIM)),
        'cache_dsa_v': nrm((N_DSA_LAYERS, n_pool, PAGE_SIZE, DSA_KV_HEADS, DSA_HEAD_DIM)),
        'cache_dsa_idx_k': nrm((N_DSA_LAYERS, n_pool, PAGE_SIZE, IDX_DIM)),
        'page_table': page_table,
        'c_prompt': nrm((BATCH, D_MODEL)),
        'c_sample': nrm((DEC_BATCH, D_MODEL)),
        'w_ada': nrm((DEPTH, D_MODEL, 6 * D_MODEL), 0.5 * D_MODEL ** -0.5),
        'b_ada': nrm((DEPTH, 6 * D_MODEL), 0.02),
        'ln_g': gain((DEPTH, 2, D_MODEL)),
        'ln_b': nrm((DEPTH, 2, D_MODEL), 0.02),
        'mla_w_in': nrm((N_MLA_LAYERS, D_MODEL, MLA_IN), D_MODEL ** -0.5),
        'mla_q_norm': gain((N_MLA_LAYERS, Q_LORA)),
        'mla_kv_norm': gain((N_MLA_LAYERS, KV_LORA)),
        'mla_w_uq': nrm((N_MLA_LAYERS, Q_LORA, MLA_HEADS * (QK_NOPE + QK_ROPE)), Q_LORA ** -0.5),
        'mla_w_uk': nrm((N_MLA_LAYERS, KV_LORA, MLA_HEADS, QK_NOPE), KV_LORA ** -0.5),
        'mla_w_uv': nrm((N_MLA_LAYERS, KV_LORA, MLA_HEADS, V_HEAD), KV_LORA ** -0.5),
        'mla_w_o': nrm((N_MLA_LAYERS, MLA_HEADS * V_HEAD, D_MODEL), DEEPNORM_BETA * (MLA_HEADS * V_HEAD) ** -0.5),
        'dsa_w_in': nrm((N_DSA_LAYERS, D_MODEL, DSA_IN), D_MODEL ** -0.5),
        'dsa_w_o': nrm((N_DSA_LAYERS, DSA_HEADS * DSA_HEAD_DIM, D_MODEL), DEEPNORM_BETA * (DSA_HEADS * DSA_HEAD_DIM) ** -0.5),
        'rel_bias': nrm((REL_BUCKETS, DSA_HEADS), 0.5),
        'ffn_w_gu': nrm((DEPTH, D_MODEL, 2 * D_FF), D_MODEL ** -0.5),
        'ffn_w_down': nrm((DEPTH, D_FF, D_MODEL), DEEPNORM_BETA * D_FF ** -0.5),
    }


def reference(x_prompt, x_sample, cache_mla_ckv, cache_mla_kpe, cache_dsa_k, cache_dsa_v, cache_dsa_idx_k,
              page_table, c_prompt, c_sample, w_ada, b_ada, ln_g, ln_b,
              mla_w_in, mla_q_norm, mla_kv_norm, mla_w_uq, mla_w_uk, mla_w_uv, mla_w_o,
              dsa_w_in, dsa_w_o, rel_bias, ffn_w_gu, ffn_w_down):
    yp, ys = x_prompt, x_sample
    p_ckv, p_kpe, p_k, p_v, p_ik = [], [], [], [], []
    s_ckv, s_kpe, s_k, s_v, s_ik = [], [], [], [], []
    for i in range(DEPTH):
        j = i // 2
        sh1p, sc1p, g1p, sh2p, sc2p, g2p = ada_mod(c_prompt, w_ada[i], b_ada[i])
        sh1s, sc1s, g1s, sh2s, sc2s, g2s = ada_mod(c_sample, w_ada[i], b_ada[i])
        up = modulate(yp, sh1p, sc1p)
        us = modulate(ys, sh1s, sc1s)
        if i % 2 == 0:
            mla_w = (mla_w_in[j], mla_q_norm[j], mla_kv_norm[j], mla_w_uq[j], mla_w_uk[j], mla_w_uv[j], mla_w_o[j])
            hp, ckv_p, kpe_p = mla_prompt(up, *mla_w)
            hs, ckv_s, kpe_s = mla_sample(us, page_table, cache_mla_ckv, cache_mla_kpe, j, *mla_w)
            p_ckv.append(ckv_p); p_kpe.append(kpe_p)
            s_ckv.append(ckv_s); s_kpe.append(kpe_s)
        else:
            hp, k_p, v_p, ik_p = dsa_prompt(up, dsa_w_in[j], dsa_w_o[j], rel_bias)
            hs, k_s, v_s, ik_s = dsa_sample(us, page_table, cache_dsa_k, cache_dsa_v, cache_dsa_idx_k, j,
                                            dsa_w_in[j], dsa_w_o[j], rel_bias)
            p_k.append(k_p); p_v.append(v_p); p_ik.append(ik_p)
            s_k.append(k_s); s_v.append(v_s); s_ik.append(ik_s)
        yp = post_norm(yp, hp, g1p, ln_g[i, 0], ln_b[i, 0])
        ys = post_norm(ys, hs, g1s, ln_g[i, 0], ln_b[i, 0])
        yp = post_norm(yp, swiglu(modulate(yp, sh2p, sc2p), ffn_w_gu[i], ffn_w_down[i]), g2p, ln_g[i, 1], ln_b[i, 1])
        ys = post_norm(ys, swiglu(modulate(ys, sh2s, sc2s), ffn_w_gu[i], ffn_w_down[i]), g2s, ln_g[i, 1], ln_b[i, 1])
    return (yp, ys,
            jnp.stack(p_ckv), jnp.stack(p_kpe), jnp.stack(p_k), jnp.stack(p_v), jnp.stack(p_ik),
            jnp.stack(s_ckv), jnp.stack(s_kpe), jnp.stack(s_k), jnp.stack(s_v), jnp.stack(s_ik))
```

```python
import functools
import math

import jax
import jax.numpy as jnp
from jax import lax
from jax.experimental import pallas as pl
from jax.experimental.pallas import tpu as pltpu

F32 = jnp.float32
BF16 = jnp.bfloat16
NEG_INF = float("-inf")

LANES = 128
ROPE_THETA = 10000.0
IDX_ROPE = 32
TOPK_MAX = 256
REL_MAX_DIST = 128
LN_EPS = 1e-5
RMS_EPS = 1e-6
BISECT_ITERS = 40
VMEM_LIMIT_MB = 56


def _cparams(sem, vmem_mb=VMEM_LIMIT_MB):
    return pltpu.CompilerParams(dimension_semantics=sem, vmem_limit_bytes=vmem_mb * 1024 * 1024)


def _dot(a, b):
    return jnp.dot(a, b, preferred_element_type=F32)


def _dot_nt(a, b):
    return lax.dot_general(a, b, (((1,), (1,)), ((), ())), preferred_element_type=F32)


def _rot(v, c, s, half):
    lane = lax.broadcasted_iota(jnp.int32, v.shape, 1)
    partner = jnp.where((lane % (2 * half)) < half,
                        pltpu.roll(v, LANES - half, 1), pltpu.roll(v, half, 1))
    return v * c + partner * s


def _mm_body(*refs, nk, act, rot_half, has_bias, second):
    it = iter(refs)
    x_ref = next(it)
    w_ref = next(it)
    b_ref = next(it) if has_bias else None
    c_ref = next(it) if rot_half else None
    s_ref = next(it) if rot_half else None
    o_ref = next(it)
    o2_ref = next(it) if second else None
    acc_ref = next(it) if nk > 1 else None

    x = x_ref[...]
    if act == "silu":
        xf = x.astype(F32)
        x = xf * jax.nn.sigmoid(xf)
    part = _dot(x.astype(BF16), w_ref[...].astype(BF16))

    def finish(acc):
        if has_bias:
            acc = acc + b_ref[...]
        if rot_half:
            c = c_ref[...]
            s = s_ref[...]
            for j in range(acc.shape[1] // LANES):
                sl = slice(j * LANES, (j + 1) * LANES)
                r = _rot(acc[:, sl], c, s, rot_half)
                o_ref[:, sl] = r.astype(o_ref.dtype)
                if second:
                    o2_ref[:, sl] = r.astype(o2_ref.dtype)
        else:
            o_ref[...] = acc.astype(o_ref.dtype)
            if second:
                o2_ref[...] = acc.astype(o2_ref.dtype)

    if nk == 1:
        finish(part)
    else:
        k = pl.program_id(2)

        @pl.when(k == 0)
        def _():
            acc_ref[...] = part

        @pl.when(k > 0)
        def _():
            acc_ref[...] += part

        @pl.when(k == nk - 1)
        def _():
            finish(acc_ref[...])


def _mm(x, w, *, n_cols=None, col0=0, tm=512, tn=512, tk=None, out_dtype=F32, second_dtype=None,
        act=None, bias=None, rot=None, name="mm"):
    M, K = x.shape
    N = n_cols if n_cols is not None else w.shape[1]
    tm = min(tm, M)
    tn = min(tn, N)
    while N % tn or col0 % tn:
        tn //= 2
    tk = K if tk is None else min(tk, K)
    assert M % tm == 0 and K % tk == 0 and tn % LANES == 0
    nk = K // tk
    jb = col0 // tn
    in_specs = [pl.BlockSpec((tm, tk), lambda i, j, k: (i, k)),
                pl.BlockSpec((tk, tn), lambda i, j, k: (k, j + jb))]
    args = [x, w]
    if bias is not None:
        in_specs.append(pl.BlockSpec((1, tn), lambda i, j, k: (0, j)))
        args.append(bias)
    rot_half = 0
    if rot is not None:
        c_tab, s_tab, rot_half = rot
        nr = c_tab.shape[0] // tm
        assert c_tab.shape[0] % tm == 0
        for t in (c_tab, s_tab):
            in_specs.append(pl.BlockSpec((tm, LANES), lambda i, j, k: (i % nr, 0)))
            args.append(t)
    out_shape = [jax.ShapeDtypeStruct((M, N), out_dtype)]
    out_specs = [pl.BlockSpec((tm, tn), lambda i, j, k: (i, j))]
    if second_dtype is not None:
        out_shape.append(jax.ShapeDtypeStruct((M, N), second_dtype))
        out_specs.append(pl.BlockSpec((tm, tn), lambda i, j, k: (i, j)))
    scratch = [pltpu.VMEM((tm, tn), F32)] if nk > 1 else []
    res = pl.pallas_call(
        functools.partial(_mm_body, nk=nk, act=act, rot_half=rot_half, has_bias=bias is not None,
                          second=second_dtype is not None),
        grid=(M // tm, N // tn, nk),
        in_specs=in_specs, out_specs=out_specs, out_shape=out_shape, scratch_shapes=scratch,
        compiler_params=_cparams(("parallel", "parallel", "arbitrary")), name=name)(*args)
    return res if second_dtype is not None else res[0]


def _modulate_body(y_ref, sh_ref, sc_ref, u_ref):
    u_ref[...] = (y_ref[...] * (1.0 + sc_ref[...]) + sh_ref[...]).astype(u_ref.dtype)


def _mod_spec(tm, n, rows_per_group):
    if rows_per_group is None:
        return pl.BlockSpec((tm, n), lambda i, *_: (i, 0))
    per = rows_per_group // tm
    return pl.BlockSpec((None, 1, n), lambda i, *_: (i // per, 0, 0))


def _modulate(y, sh, sc, rows_per_group, tm=512):
    M, D = y.shape
    tm = min(tm, M)
    return pl.pallas_call(
        _modulate_body, grid=(M // tm,),
        in_specs=[pl.BlockSpec((tm, D), lambda i: (i, 0)), _mod_spec(tm, D, rows_per_group),
                  _mod_spec(tm, D, rows_per_group)],
        out_specs=pl.BlockSpec((tm, D), lambda i: (i, 0)),
        out_shape=jax.ShapeDtypeStruct((M, D), BF16),
        compiler_params=_cparams(("parallel",)), name="modulate")(y, sh, sc)


def _mm_postnorm_body(*refs, nk, alpha, with_mod):
    if with_mod:
        x_ref, w_ref, y_ref, g_ref, lg_ref, lb_ref, sh_ref, sc_ref, yo_ref, u_ref, acc_ref = refs
    else:
        x_ref, w_ref, y_ref, g_ref, lg_ref, lb_ref, yo_ref, acc_ref = refs
    k = pl.program_id(1)
    part = _dot(x_ref[...].astype(BF16), w_ref[...].astype(BF16))

    @pl.when(k == 0)
    def _():
        acc_ref[...] = part

    @pl.when(k > 0)
    def _():
        acc_ref[...] += part

    @pl.when(k == nk - 1)
    def _():
        z = alpha * y_ref[...] + g_ref[...] * acc_ref[...]
        mu = jnp.mean(z, axis=-1, keepdims=True)
        zc = z - mu
        var = jnp.mean(zc * zc, axis=-1, keepdims=True)
        yn = zc * lax.rsqrt(var + LN_EPS) * lg_ref[...] + lb_ref[...]
        yo_ref[...] = yn
        if with_mod:
            u_ref[...] = (yn * (1.0 + sc_ref[...]) + sh_ref[...]).astype(u_ref.dtype)


def _mm_postnorm(x, w, y, gate, ln_g, ln_b, mod, rows_per_group, alpha, tm=512, tk=512, name="mm_postnorm"):
    M, K = x.shape
    N = w.shape[1]
    tm = min(tm, M)
    tk = min(tk, K)
    while K % tk:
        tk //= 2
    assert M % tm == 0 and tk % LANES == 0
    nk = K // tk
    row = pl.BlockSpec((tm, N), lambda i, k: (i, 0))
    vec = pl.BlockSpec((1, N), lambda i, k: (0, 0))
    in_specs = [pl.BlockSpec((tm, tk), lambda i, k: (i, k)), pl.BlockSpec((tk, N), lambda i, k: (k, 0)),
                row, _mod_spec(tm, N, rows_per_group), vec, vec]
    args = [x, w, y, gate, ln_g.reshape(1, N), ln_b.reshape(1, N)]
    out_shape = [jax.ShapeDtypeStruct((M, N), F32)]
    out_specs = [row]
    if mod is not None:
        in_specs += [_mod_spec(tm, N, rows_per_group)] * 2
        args += list(mod)
        out_shape.append(jax.ShapeDtypeStruct((M, N), BF16))
        out_specs.append(row)
    res = pl.pallas_call(
        functools.partial(_mm_postnorm_body, nk=nk, alpha=alpha, with_mod=mod is not None),
        grid=(M // tm, nk), in_specs=in_specs, out_specs=out_specs, out_shape=out_shape,
        scratch_shapes=[pltpu.VMEM((tm, N), F32)],
        compiler_params=_cparams(("parallel", "arbitrary")), name=name)(*args)
    return (res[0], res[1]) if mod is not None else (res[0], None)


def _swiglu_body(x_ref, wg_ref, wu_ref, o_ref):
    x = x_ref[...]
    g = _dot(x, wg_ref[...].astype(BF16))
    u = _dot(x, wu_ref[...].astype(BF16))
    o_ref[...] = (g * jax.nn.sigmoid(g) * u).astype(o_ref.dtype)


def _swiglu_up(x, w_gu, tm=1024, tn=512):
    M, K = x.shape
    F = w_gu.shape[1] // 2
    tm = min(tm, M)
    tn = min(tn, F)
    while F % tn:
        tn //= 2
    nj = F // tn
    return pl.pallas_call(
        _swiglu_body, grid=(M // tm, nj),
        in_specs=[pl.BlockSpec((tm, K), lambda i, j: (i, 0)),
                  pl.BlockSpec((K, tn), lambda i, j: (0, j)),
                  pl.BlockSpec((K, tn), lambda i, j: (0, j + nj))],
        out_specs=pl.BlockSpec((tm, tn), lambda i, j: (i, j)),
        out_shape=jax.ShapeDtypeStruct((M, F), BF16),
        compiler_params=_cparams(("parallel", "parallel")), name="swiglu_up")(x, w_gu, w_gu)


def _mla_in_body(x_ref, w_ref, qg_ref, kvg_ref, c_ref, s_ref, cq_ref, ckv_ref, ckvb_ref, kpe_ref, kpeb_ref,
                 *, ql, kvl, rope):
    acc = _dot(x_ref[...], w_ref[...])

    def rms(v, g):
        return v * lax.rsqrt(jnp.mean(v * v, axis=-1, keepdims=True) + RMS_EPS) * g

    cq_ref[...] = rms(acc[:, :ql], qg_ref[...]).astype(cq_ref.dtype)
    ckv = rms(acc[:, ql:ql + kvl], kvg_ref[...])
    ckv_ref[...] = ckv
    ckvb_ref[...] = ckv.astype(ckvb_ref.dtype)
    kpe = _rot(acc[:, ql + kvl:], c_ref[...], s_ref[...], rope // 2)[:, :rope]
    kpe_ref[...] = kpe
    kpeb_ref[...] = kpe.astype(kpeb_ref.dtype)


def _mla_in(u, w_pad, q_g, kv_g, c_tab, s_tab, ql, kvl, rope, tm=512):
    M, K = u.shape
    N = w_pad.shape[1]
    tm = min(tm, M)
    nr = c_tab.shape[0] // tm
    tab = pl.BlockSpec((tm, LANES), lambda i: (i % nr, 0))

    def row(n):
        return pl.BlockSpec((tm, n), lambda i: (i, 0))

    return pl.pallas_call(
        functools.partial(_mla_in_body, ql=ql, kvl=kvl, rope=rope), grid=(M // tm,),
        in_specs=[row(K), pl.BlockSpec((K, N), lambda i: (0, 0)),
                  pl.BlockSpec((1, ql), lambda i: (0, 0)), pl.BlockSpec((1, kvl), lambda i: (0, 0)), tab, tab],
        out_specs=[row(ql), row(kvl), row(kvl), row(rope), row(rope)],
        out_shape=[jax.ShapeDtypeStruct((M, ql), BF16), jax.ShapeDtypeStruct((M, kvl), F32),
                   jax.ShapeDtypeStruct((M, kvl), BF16), jax.ShapeDtypeStruct((M, rope), F32),
                   jax.ShapeDtypeStruct((M, rope), BF16)],
        compiler_params=_cparams(("parallel",)), name="mla_in")(
            u, w_pad, q_g.reshape(1, ql), kv_g.reshape(1, kvl), c_tab, s_tab)


def _mla_q_body(cq_ref, wn_ref, wp_ref, wuk_ref, c_ref, s_ref, ql_ref, qp_ref, *, H, nope, rope):
    cq = cq_ref[...]
    qn = _dot(cq, wn_ref[...]).astype(BF16)
    for h in range(H):
        ql_ref[0, h] = _dot(qn[:, h * nope:(h + 1) * nope], wuk_ref[h]).astype(ql_ref.dtype)
    qp = _dot(cq, wp_ref[...])
    c = c_ref[...]
    s = s_ref[...]
    per = LANES // rope
    for j in range(H // per):
        r = _rot(qp[:, j * LANES:(j + 1) * LANES], c, s, rope // 2)
        for e in range(per):
            qp_ref[0, j * per + e] = r[:, e * rope:(e + 1) * rope].astype(qp_ref.dtype)


def _mla_q(cq, w_nope, w_pe, w_uk_t, c_tab, s_tab, nb, tb, H, nope, rope, kvl, tm=512):
    M, ql = cq.shape
    tm = min(tm, tb)
    per = tb // tm
    nr = c_tab.shape[0] // tm
    tab = pl.BlockSpec((tm, LANES), lambda i: (i % nr, 0))
    full2 = lambda a: pl.BlockSpec(a.shape, lambda i: (0, 0))
    return pl.pallas_call(
        functools.partial(_mla_q_body, H=H, nope=nope, rope=rope), grid=(M // tm,),
        in_specs=[pl.BlockSpec((tm, ql), lambda i: (i, 0)), full2(w_nope), full2(w_pe),
                  pl.BlockSpec(w_uk_t.shape, lambda i: (0, 0, 0)), tab, tab],
        out_specs=[pl.BlockSpec((1, H, tm, kvl), lambda i: (i // per, 0, i % per, 0)),
                   pl.BlockSpec((1, H, tm, rope), lambda i: (i // per, 0, i % per, 0))],
        out_shape=[jax.ShapeDtypeStruct((nb, H, tb, kvl), BF16), jax.ShapeDtypeStruct((nb, H, tb, rope), BF16)],
        compiler_params=_cparams(("parallel",)), name="mla_q")(cq, w_nope, w_pe, w_uk_t, c_tab, s_tab)


def _flash_update(s, v, m_ref, l_ref, acc_ref):
    m_old = m_ref[...]
    m_new = jnp.maximum(m_old, jnp.max(s, axis=-1, keepdims=True))
    m_safe = jnp.where(m_new == NEG_INF, 0.0, m_new)
    a = jnp.exp(m_old - m_safe)
    p = jnp.exp(s - m_safe)
    l_ref[...] = a * l_ref[...] + jnp.sum(p, axis=-1, keepdims=True)
    acc_ref[...] = a * acc_ref[...] + _dot(p.astype(BF16), v)
    m_ref[...] = m_new


def _mla_prompt_body(ql_ref, qp_ref, ckv_ref, kpe_ref, wuv_ref, o_ref, m_ref, l_ref, acc_ref,
                     *, H, tq, tk, nk, scale, vh):
    i = pl.program_id(1)
    j = pl.program_id(2)

    @pl.when(j == 0)
    def _():
        m_ref[...] = jnp.full(m_ref.shape, NEG_INF, F32)
        l_ref[...] = jnp.zeros(l_ref.shape, F32)
        acc_ref[...] = jnp.zeros(acc_ref.shape, F32)

    @pl.when(j * tk <= i * tq + tq - 1)
    def _():
        ql = ql_ref[0].reshape(H * tq, ql_ref.shape[-1])
        qp = qp_ref[0].reshape(H * tq, qp_ref.shape[-1])
        ckv = ckv_ref[0]
        s = (_dot_nt(ql, ckv) + _dot_nt(qp, kpe_ref[0])) * scale
        tok = i * tq + (lax.broadcasted_iota(jnp.int32, s.shape, 0) % tq)
        key = j * tk + lax.broadcasted_iota(jnp.int32, s.shape, 1)
        s = jnp.where(key <= tok, s, NEG_INF)
        _flash_update(s, ckv, m_ref, l_ref, acc_ref)

    @pl.when(j == nk - 1)
    def _():
        o = (acc_ref[...] / l_ref[...]).astype(BF16)
        for h in range(H):
            o_ref[:, h * vh:(h + 1) * vh] = _dot(o[h * tq:(h + 1) * tq], wuv_ref[:, h * vh:(h + 1) * vh]
                                                 ).astype(o_ref.dtype)


def _mla_prompt_attn(q_lat, q_pe, ckv_b, kpe_b, w_uv, scale, tq=256, tk=256):
    B, H, T, C = q_lat.shape
    R = q_pe.shape[-1]
    tq = min(tq, T)
    tk = min(tk, T)
    nq, nk = T // tq, T // tk
    vh = w_uv.shape[1] // H

    def kmap(b, i, j):
        return (b, jnp.minimum(j, (i * tq + tq - 1) // tk), 0)

    return pl.pallas_call(
        functools.partial(_mla_prompt_body, H=H, tq=tq, tk=tk, nk=nk, scale=scale, vh=vh),
        grid=(B, nq, nk),
        in_specs=[pl.BlockSpec((1, H, tq, C), lambda b, i, j: (b, 0, i, 0)),
                  pl.BlockSpec((1, H, tq, R), lambda b, i, j: (b, 0, i, 0)),
                  pl.BlockSpec((1, tk, C), kmap), pl.BlockSpec((1, tk, R), kmap),
                  pl.BlockSpec(w_uv.shape, lambda b, i, j: (0, 0))],
        out_specs=pl.BlockSpec((tq, H * vh), lambda b, i, j: (b * nq + i, 0)),
        out_shape=jax.ShapeDtypeStruct((B * T, H * vh), BF16),
        scratch_shapes=[pltpu.VMEM((H * tq, 1), F32), pltpu.VMEM((H * tq, 1), F32),
                        pltpu.VMEM((H * tq, C), F32)],
        compiler_params=_cparams(("parallel", "parallel", "arbitrary")), name="mla_prompt_attn")(
            q_lat, q_pe, ckv_b, kpe_b, w_uv)


def _new_key_scores(q_parts, k_parts, n_new):
    cols = []
    for t in range(n_new):
        acc = None
        for q, k in zip(q_parts, k_parts):
            d = jnp.sum(q * k[t:t + 1, :], axis=-1, keepdims=True)
            acc = d if acc is None else acc + d
        cols.append(acc)
    return cols


def _cols_to_block(cols, rows):
    lane = lax.broadcasted_iota(jnp.int32, (rows, LANES), 1)
    blk = jnp.full((rows, LANES), NEG_INF, F32)
    for t, c in enumerate(cols):
        blk = jnp.where(lane == t, c, blk)
    return blk


def _flash_update_new(s_blk, v_new, n_new, m_ref, l_ref, acc_ref):
    m_old = m_ref[...]
    m_new = jnp.maximum(m_old, jnp.max(s_blk, axis=-1, keepdims=True))
    m_safe = jnp.where(m_new == NEG_INF, 0.0, m_new)
    a = jnp.exp(m_old - m_safe)
    p = jnp.exp(s_blk - m_safe)
    l_ref[...] = a * l_ref[...] + jnp.sum(p, axis=-1, keepdims=True)
    acc = a * acc_ref[...]
    for t in range(n_new):
        acc = acc + p[:, t:t + 1] * v_new[t:t + 1, :]
    acc_ref[...] = acc
    m_ref[...] = m_new


def _mla_sample_body(*refs, pps, H, ts, page, nc, scale, vh):
    pt_ref = refs[0]
    ql_ref, qp_ref = refs[1], refs[2]
    ckv_refs = refs[3:3 + pps]
    kpe_refs = refs[3 + pps:3 + 2 * pps]
    cn_ref, kn_ref, wuv_ref, o_ref, kc_ref, kp_ref, m_ref, l_ref, acc_ref = refs[3 + 2 * pps:]
    del pt_ref
    c = pl.program_id(1)
    R = H * ts

    @pl.when(c == 0)
    def _():
        m_ref[...] = jnp.full(m_ref.shape, NEG_INF, F32)
        l_ref[...] = jnp.zeros(l_ref.shape, F32)
        acc_ref[...] = jnp.zeros(acc_ref.shape, F32)

    for r in range(pps):
        kc_ref[r * page:(r + 1) * page, :] = ckv_refs[r][0].astype(BF16)
        kp_ref[r * page:(r + 1) * page, :] = kpe_refs[r][0].astype(BF16)
    ql = ql_ref[0]
    qp = qp_ref[0]
    ckv = kc_ref[...]
    s = (_dot_nt(ql, ckv) + _dot_nt(qp, kp_ref[...])) * scale
    _flash_update(s, ckv, m_ref, l_ref, acc_ref)

    @pl.when(c == nc - 1)
    def _():
        cn = cn_ref[0]
        cols = _new_key_scores([ql.astype(F32), qp.astype(F32)], [cn, kn_ref[0]], ts)
        blk = _cols_to_block(cols, R) * scale
        qt = lax.broadcasted_iota(jnp.int32, (R, LANES), 0) % ts
        lane = lax.broadcasted_iota(jnp.int32, (R, LANES), 1)
        blk = jnp.where(lane <= qt, blk, NEG_INF)
        _flash_update_new(blk, cn, ts, m_ref, l_ref, acc_ref)
        o = (acc_ref[...] / l_ref[...]).astype(BF16)
        full = _dot(o, wuv_ref[...])
        o_ref[0] = jnp.concatenate([full[h * ts:(h + 1) * ts, h * vh:(h + 1) * vh] for h in range(H)], axis=1)


def _mla_sample_attn(q_lat, q_pe, pool_ckv, pool_kpe, page_table, ckv_new, kpe_new, w_uv, scale, H, pps=16):
    Bs, R, C = q_lat.shape
    Rr = q_pe.shape[-1]
    ts = R // H
    n_pages = page_table.shape[1]
    page = pool_ckv.shape[1]
    pps = min(pps, n_pages)
    nc = n_pages // pps
    vh = w_uv.shape[1] // H

    def pmap(r):
        return lambda b, c, pt: (pt[b * n_pages + c * pps + r], 0, 0)

    in_specs = [pl.BlockSpec((1, R, C), lambda b, c, pt: (b, 0, 0)),
                pl.BlockSpec((1, R, Rr), lambda b, c, pt: (b, 0, 0))]
    in_specs += [pl.BlockSpec((1, page, C), pmap(r)) for r in range(pps)]
    in_specs += [pl.BlockSpec((1, page, Rr), pmap(r)) for r in range(pps)]
    in_specs += [pl.BlockSpec((1, ts, C), lambda b, c, pt: (b, 0, 0)),
                 pl.BlockSpec((1, ts, Rr), lambda b, c, pt: (b, 0, 0)),
                 pl.BlockSpec(w_uv.shape, lambda b, c, pt: (0, 0))]
    grid_spec = pltpu.PrefetchScalarGridSpec(
        num_scalar_prefetch=1, grid=(Bs, nc), in_specs=in_specs,
        out_specs=pl.BlockSpec((1, ts, H * vh), lambda b, c, pt: (b, 0, 0)),
        scratch_shapes=[pltpu.VMEM((pps * page, C), BF16), pltpu.VMEM((pps * page, Rr), BF16),
                        pltpu.VMEM((R, 1), F32), pltpu.VMEM((R, 1), F32), pltpu.VMEM((R, C), F32)])
    return pl.pallas_call(
        functools.partial(_mla_sample_body, pps=pps, H=H, ts=ts, page=page, nc=nc, scale=scale, vh=vh),
        grid_spec=grid_spec, out_shape=jax.ShapeDtypeStruct((Bs, ts, H * vh), F32),
        compiler_params=_cparams(("parallel", "arbitrary")), name="mla_sample_attn")(
            page_table.reshape(-1), q_lat, q_pe, *([pool_ckv] * pps), *([pool_kpe] * pps),
            ckv_new, kpe_new, w_uv)


def _t5_bias_of_dist(d, rb_ref, h, n_buckets):
    dist = jnp.maximum(d, 0)
    exact = n_buckets // 2
    df = jnp.maximum(dist, 1).astype(F32)
    large = exact + (jnp.log(df / exact) / math.log(REL_MAX_DIST / exact) * (n_buckets - exact)).astype(jnp.int32)
    large = jnp.minimum(large, n_buckets - 1)
    bucket = jnp.where(dist < exact, dist, large)
    out = jnp.zeros(d.shape, F32)
    for b in range(n_buckets):
        out = jnp.where(bucket == b, rb_ref[b, h], out)
    return out


def _bias_tables_body(rb_ref, t0_ref, t1_ref, tp_ref, tn_ref, *, NH, tq, ts8, n_buckets):
    r = lax.broadcasted_iota(jnp.int32, (tq, tq), 0)
    c = lax.broadcasted_iota(jnp.int32, (tq, tq), 1)
    r8 = lax.broadcasted_iota(jnp.int32, (ts8, LANES), 0)
    c8 = lax.broadcasted_iota(jnp.int32, (ts8, LANES), 1)
    for h in range(NH):
        t0_ref[h] = _t5_bias_of_dist(r - c, rb_ref, h, n_buckets)
        t1_ref[h] = _t5_bias_of_dist(r - c + tq, rb_ref, h, n_buckets)
        tp_ref[h] = _t5_bias_of_dist(r8 - c8 + LANES, rb_ref, h, n_buckets)
        tn_ref[h] = _t5_bias_of_dist(r8 - c8, rb_ref, h, n_buckets)


def _bias_tables(rel_bias, tq, ts8):
    nb, NH = rel_bias.shape
    return pl.pallas_call(
        functools.partial(_bias_tables_body, NH=NH, tq=tq, ts8=ts8, n_buckets=nb),
        in_specs=[pl.BlockSpec(memory_space=pltpu.SMEM)],
        out_shape=[jax.ShapeDtypeStruct((NH, tq, tq), F32), jax.ShapeDtypeStruct((NH, tq, tq), F32),
                   jax.ShapeDtypeStruct((NH, ts8, LANES), F32), jax.ShapeDtypeStruct((NH, ts8, LANES), F32)],
        compiler_params=pltpu.CompilerParams(vmem_limit_bytes=VMEM_LIMIT_MB * 1024 * 1024),
        name="t5_bias_tables")(rel_bias)


def _bisect(lo0, hi0, count_ge, k_sel):
    def body(_, lh):
        lo, hi = lh
        mid = 0.5 * (lo + hi)
        ge = count_ge(mid) >= k_sel
        return jnp.where(ge, mid, lo), jnp.where(ge, hi, mid)

    lo, _ = lax.fori_loop(0, BISECT_ITERS, body, (lo0, hi0))
    return lo


def _dsa_prompt_body(far_ref, q_ref, qi_ref, wi_ref, k_ref, v_ref, ki_ref, t0_ref, t1_ref, o_ref,
                     sc_ref, m_ref, l_ref, acc_ref, *, tq, k_sel, IH, ID, KVH, G, HD, scale):
    i = pl.program_id(1)
    row_tok = i * tq + lax.broadcasted_iota(jnp.int32, (tq, tq), 0)
    col_in = lax.broadcasted_iota(jnp.int32, (tq, tq), 1)
    qi = qi_ref[...]
    wi = wi_ref[...]

    def idx_block(c, carry):
        lo, hi = carry
        kic = ki_ref[0, pl.ds(pl.multiple_of(c * tq, tq), tq), :]
        tot = jnp.zeros((tq, tq), F32)
        for h in range(IH):
            d = _dot_nt(qi[:, h * ID:(h + 1) * ID], kic)
            tot = tot + jnp.maximum(d, 0.0) * wi[:, h:h + 1]
        ok = (c * tq + col_in) <= row_tok
        sc_ref[c] = jnp.where(ok, tot, NEG_INF)
        lo = jnp.minimum(lo, jnp.min(jnp.where(ok, tot, jnp.inf), axis=-1, keepdims=True))
        hi = jnp.maximum(hi, jnp.max(jnp.where(ok, tot, NEG_INF), axis=-1, keepdims=True))
        return lo, hi

    lo0, hi0 = lax.fori_loop(0, i + 1, idx_block,
                             (jnp.full((tq, 1), jnp.inf, F32), jnp.full((tq, 1), NEG_INF, F32)))

    def count_ge(mid):
        def blk(c, acc):
            t = sc_ref[c]
            part = jnp.zeros((tq, LANES), F32)
            for j in range(tq // LANES):
                part = part + jnp.where(t[:, j * LANES:(j + 1) * LANES] >= mid, 1.0, 0.0)
            return acc + part
        acc = lax.fori_loop(0, i + 1, blk, jnp.zeros((tq, LANES), F32))
        return jnp.sum(acc, axis=-1, keepdims=True)

    thr = _bisect(lo0, hi0, count_ge, float(k_sel))

    def to_mask(c, _):
        sc_ref[c] = jnp.where(sc_ref[c] >= thr, 0.0, NEG_INF)
        return 0

    lax.fori_loop(0, i + 1, to_mask, 0)

    q = q_ref[...]
    for n in range(KVH):
        qn = jnp.concatenate([q[:, (n * G + g) * HD:(n * G + g + 1) * HD] for g in range(G)], axis=0)
        m_ref[...] = jnp.full(m_ref.shape, NEG_INF, F32)
        l_ref[...] = jnp.zeros(l_ref.shape, F32)
        acc_ref[...] = jnp.zeros(acc_ref.shape, F32)

        def step(c, bias):
            start = pl.multiple_of(c * tq, tq)
            kc = k_ref[0, pl.ds(start, tq), n * HD:(n + 1) * HD]
            vc = v_ref[0, pl.ds(start, tq), n * HD:(n + 1) * HD]
            msk = sc_ref[c]
            s = _dot_nt(qn, kc) * scale + bias + jnp.concatenate([msk] * G, axis=0)
            _flash_update(s, vc, m_ref, l_ref, acc_ref)

        far = jnp.concatenate([jnp.full((tq, 1), far_ref[n * G + g], F32) for g in range(G)], axis=0)

        def far_step(c, _):
            step(c, far)
            return 0

        lax.fori_loop(0, jnp.maximum(i - 1, 0), far_step, 0)

        @pl.when(i >= 1)
        def _():
            step(i - 1, t1_ref[n * G:(n + 1) * G].reshape(G * tq, tq))

        step(i, t0_ref[n * G:(n + 1) * G].reshape(G * tq, tq))
        o = acc_ref[...] / l_ref[...]
        for g in range(G):
            o_ref[:, (n * G + g) * HD:(n * G + g + 1) * HD] = o[g * tq:(g + 1) * tq].astype(o_ref.dtype)


def _dsa_prompt(q, qi, wi, k_b, v_b, ki_b, t0, t1, far_bias, B, T, k_sel, IH, ID, KVH, G, HD, scale, tq):
    M = B * T
    nq = T // tq
    NH = KVH * G
    row = lambda n: pl.BlockSpec((tq, n), lambda b, i, f: (b * nq + i, 0))
    seq = lambda n: pl.BlockSpec((1, T, n), lambda b, i, f: (b, 0, 0))
    tab = pl.BlockSpec((NH, tq, tq), lambda b, i, f: (0, 0, 0))
    grid_spec = pltpu.PrefetchScalarGridSpec(
        num_scalar_prefetch=1, grid=(B, nq),
        in_specs=[row(NH * HD), row(IH * ID), row(IH), seq(KVH * HD), seq(KVH * HD), seq(ID), tab, tab],
        out_specs=row(NH * HD),
        scratch_shapes=[pltpu.VMEM((nq, tq, tq), F32), pltpu.VMEM((G * tq, 1), F32),
                        pltpu.VMEM((G * tq, 1), F32), pltpu.VMEM((G * tq, HD), F32)])
    return pl.pallas_call(
        functools.partial(_dsa_prompt_body, tq=tq, k_sel=k_sel, IH=IH, ID=ID, KVH=KVH, G=G, HD=HD, scale=scale),
        grid_spec=grid_spec, out_shape=jax.ShapeDtypeStruct((M, NH * HD), BF16),
        compiler_params=_cparams(("parallel", "arbitrary")), name="dsa_prompt")(
            far_bias, q, qi, wi, k_b.reshape(B, T, -1), v_b.reshape(B, T, -1), ki_b.reshape(B, T, -1), t0, t1)


def _dsa_sample_index_body(*refs, pps, IH, ts, ts8, page, nc, k_sel):
    qi_ref, wi_ref = refs[1], refs[2]
    ik_refs = refs[3:3 + pps]
    kn_ref, mask_ref, maskn_ref, kb_ref, sc_ref = refs[3 + pps:]
    c = pl.program_id(1)
    for r in range(pps):
        kb_ref[r * page:(r + 1) * page, :] = ik_refs[r][0].astype(BF16)
    qi = qi_ref[0]
    wcol = wi_ref[0]
    d = jnp.maximum(_dot_nt(qi, kb_ref[...]), 0.0) * wcol
    tot = d[0:ts8]
    for h in range(1, IH):
        tot = tot + d[h * ts8:(h + 1) * ts8]
    sc_ref[c] = tot

    @pl.when(c == nc - 1)
    def _():
        cols = _new_key_scores([qi.astype(F32)], [kn_ref[0]], ts)
        ncols = []
        for col in cols:
            col = jnp.maximum(col, 0.0) * wcol
            t = col[0:ts8]
            for h in range(1, IH):
                t = t + col[h * ts8:(h + 1) * ts8]
            ncols.append(t)
        blk = _cols_to_block(ncols, ts8)
        qt = lax.broadcasted_iota(jnp.int32, (ts8, LANES), 0)
        lane = lax.broadcasted_iota(jnp.int32, (ts8, LANES), 1)
        blk = jnp.where(lane <= qt, blk, NEG_INF)
        lo0 = jnp.min(jnp.where(blk > NEG_INF, blk, jnp.inf), axis=-1, keepdims=True)
        hi0 = jnp.max(blk, axis=-1, keepdims=True)
        for cc in range(nc):
            lo0 = jnp.minimum(lo0, jnp.min(sc_ref[cc], axis=-1, keepdims=True))
            hi0 = jnp.maximum(hi0, jnp.max(sc_ref[cc], axis=-1, keepdims=True))

        def count_ge(mid):
            acc = jnp.where(blk >= mid, 1.0, 0.0)
            for cc in range(nc):
                t = sc_ref[cc]
                for j in range(t.shape[1] // LANES):
                    acc = acc + jnp.where(t[:, j * LANES:(j + 1) * LANES] >= mid, 1.0, 0.0)
            return jnp.sum(acc, axis=-1, keepdims=True)

        thr = _bisect(lo0, hi0, count_ge, float(k_sel))
        for cc in range(nc):
            mask_ref[0, cc] = jnp.where(sc_ref[cc] >= thr, 0.0, NEG_INF)
        maskn_ref[0] = jnp.where(blk >= thr, 0.0, NEG_INF)


def _dsa_sample_index(qi_rows, wi_rows, pool_ik, page_table, ki_new, k_sel, IH, ts, ts8, pps=16):
    Bs, R, ID = qi_rows.shape
    n_pages = page_table.shape[1]
    page = pool_ik.shape[1]
    pps = min(pps, n_pages)
    nc = n_pages // pps
    W = pps * page

    def pmap(r):
        return lambda b, c, pt: (pt[b * n_pages + c * pps + r], 0, 0)

    in_specs = [pl.BlockSpec((1, R, ID), lambda b, c, pt: (b, 0, 0)),
                pl.BlockSpec((1, R, 1), lambda b, c, pt: (b, 0, 0))]
    in_specs += [pl.BlockSpec((1, page, ID), pmap(r)) for r in range(pps)]
    in_specs += [pl.BlockSpec((1, ts, ID), lambda b, c, pt: (b, 0, 0))]
    grid_spec = pltpu.PrefetchScalarGridSpec(
        num_scalar_prefetch=1, grid=(Bs, nc), in_specs=in_specs,
        out_specs=[pl.BlockSpec((1, nc, ts8, W), lambda b, c, pt: (b, 0, 0, 0)),
                   pl.BlockSpec((1, ts8, LANES), lambda b, c, pt: (b, 0, 0))],
        scratch_shapes=[pltpu.VMEM((W, ID), BF16), pltpu.VMEM((nc, ts8, W), F32)])
    return pl.pallas_call(
        functools.partial(_dsa_sample_index_body, pps=pps, IH=IH, ts=ts, ts8=ts8, page=page, nc=nc, k_sel=k_sel),
        grid_spec=grid_spec,
        out_shape=[jax.ShapeDtypeStruct((Bs, nc, ts8, W), F32), jax.ShapeDtypeStruct((Bs, ts8, LANES), F32)],
        compiler_params=_cparams(("parallel", "arbitrary")), name="dsa_sample_index")(
            page_table.reshape(-1), qi_rows, wi_rows, *([pool_ik] * pps), ki_new)


def _dsa_sample_attn_body(*refs, pps, KVH, G, HD, ts, ts8, page, nc, scale):
    far_ref = refs[1]
    q_ref, mask_ref, maskn_ref = refs[2], refs[3], refs[4]
    k_refs = refs[5:5 + pps]
    v_refs = refs[5 + pps:5 + 2 * pps]
    kn_ref, vn_ref, tp_ref, tn_ref, o_ref, kb_ref, vb_ref, m_ref, l_ref, acc_ref = refs[5 + 2 * pps:]
    c = pl.program_id(1)
    R = G * ts8
    W = pps * page

    @pl.when(c == 0)
    def _():
        m_ref[...] = jnp.full(m_ref.shape, NEG_INF, F32)
        l_ref[...] = jnp.zeros(l_ref.shape, F32)
        acc_ref[...] = jnp.zeros(acc_ref.shape, F32)

    for r in range(pps):
        kb_ref[r * page:(r + 1) * page, :] = k_refs[r][0].astype(BF16)
        vb_ref[r * page:(r + 1) * page, :] = v_refs[r][0].astype(BF16)
    msk = jnp.concatenate([mask_ref[0, 0]] * G, axis=0)
    last = c == nc - 1
    for n in range(KVH):
        qn = q_ref[0, n]
        far = jnp.concatenate([jnp.full((ts8, 1), far_ref[n * G + g], F32) for g in range(G)], axis=0)
        near = tp_ref[n * G:(n + 1) * G].reshape(R, LANES)
        delta = jnp.where(last, near - far, 0.0)
        s = _dot_nt(qn, kb_ref[:, n * HD:(n + 1) * HD]) * scale + far + msk
        s = jnp.concatenate([s[:, :W - LANES], s[:, W - LANES:] + delta], axis=1)
        _flash_update(s, vb_ref[:, n * HD:(n + 1) * HD], m_ref.at[n], l_ref.at[n], acc_ref.at[n])

    @pl.when(last)
    def _():
        mskn = jnp.concatenate([maskn_ref[0]] * G, axis=0)
        outs = []
        for n in range(KVH):
            qn = q_ref[0, n].astype(F32)
            kn = kn_ref[0][:, n * HD:(n + 1) * HD]
            vn = vn_ref[0][:, n * HD:(n + 1) * HD]
            cols = _new_key_scores([qn], [kn], ts)
            blk = _cols_to_block(cols, R) * scale + tn_ref[n * G:(n + 1) * G].reshape(R, LANES) + mskn
            _flash_update_new(blk, vn, ts, m_ref.at[n], l_ref.at[n], acc_ref.at[n])
            o = acc_ref[n] / l_ref[n]
            outs += [o[g * ts8:g * ts8 + ts] for g in range(G)]
        o_ref[0] = jnp.concatenate(outs, axis=1)


def _dsa_sample_attn(q_rows, mask, mask_new, pool_k, pool_v, page_table, k_new, v_new, tp, tn, far_bias,
                     KVH, G, HD, ts, ts8, scale, pps=16):
    Bs = q_rows.shape[0]
    R = G * ts8
    n_pages = page_table.shape[1]
    page = pool_k.shape[1]
    pps = min(pps, n_pages)
    nc = n_pages // pps
    W = pps * page
    NH = KVH * G

    def pmap(r):
        return lambda b, c, pt, f: (pt[b * n_pages + c * pps + r], 0, 0)

    in_specs = [pl.BlockSpec((1, KVH, R, HD), lambda b, c, pt, f: (b, 0, 0, 0)),
                pl.BlockSpec((1, 1, ts8, W), lambda b, c, pt, f: (b, c, 0, 0)),
                pl.BlockSpec((1, ts8, LANES), lambda b, c, pt, f: (b, 0, 0))]
    in_specs += [pl.BlockSpec((1, page, KVH * HD), pmap(r)) for r in range(pps)] * 2
    in_specs += [pl.BlockSpec((1, ts, KVH * HD), lambda b, c, pt, f: (b, 0, 0))] * 2
    in_specs += [pl.BlockSpec((NH, ts8, LANES), lambda b, c, pt, f: (0, 0, 0))] * 2
    grid_spec = pltpu.PrefetchScalarGridSpec(
        num_scalar_prefetch=2, grid=(Bs, nc), in_specs=in_specs,
        out_specs=pl.BlockSpec((1, ts, NH * HD), lambda b, c, pt, f: (b, 0, 0)),
        scratch_shapes=[pltpu.VMEM((W, KVH * HD), BF16), pltpu.VMEM((W, KVH * HD), BF16),
                        pltpu.VMEM((KVH, R, 1), F32), pltpu.VMEM((KVH, R, 1), F32), pltpu.VMEM((KVH, R, HD), F32)])
    return pl.pallas_call(
        functools.partial(_dsa_sample_attn_body, pps=pps, KVH=KVH, G=G, HD=HD, ts=ts, ts8=ts8, page=page,
                          nc=nc, scale=scale),
        grid_spec=grid_spec, out_shape=jax.ShapeDtypeStruct((Bs, ts, NH * HD), F32),
        compiler_params=_cparams(("parallel", "arbitrary")), name="dsa_sample_attn")(
            page_table.reshape(-1), far_bias, q_rows, mask, mask_new, *([pool_k] * pps), *([pool_v] * pps),
            k_new, v_new, tp, tn)


def _rot_tables(pos, half, group, extra=None):
    lane = jnp.arange(LANES)
    g = lane % group
    rotary = g < 2 * half
    freq = ROPE_THETA ** (-jnp.arange(half, dtype=F32) / half)
    ang = pos.astype(F32)[:, None] * freq[None, :]
    cos = jnp.cos(ang)[:, g % half]
    sin = jnp.sin(ang)[:, g % half]
    c_tab = jnp.where(rotary[None, :], cos, 1.0)
    s_tab = jnp.where(rotary[None, :], jnp.where((g < half)[None, :], -sin, sin), 0.0)
    if extra is not None:
        a, b, val = extra
        c_tab = jnp.where(((lane >= a) & (lane < b))[None, :], val, c_tab)
    return c_tab.astype(F32), s_tab.astype(F32)


def kernel(x_prompt, x_sample, cache_mla_ckv, cache_mla_kpe, cache_dsa_k, cache_dsa_v, cache_dsa_idx_k,
           page_table, c_prompt, c_sample, w_ada, b_ada, ln_g, ln_b,
           mla_w_in, mla_q_norm, mla_kv_norm, mla_w_uq, mla_w_uk, mla_w_uv, mla_w_o,
           dsa_w_in, dsa_w_o, rel_bias, ffn_w_gu, ffn_w_down):
    B, T, D = x_prompt.shape
    Bs, ts, _ = x_sample.shape
    depth = w_ada.shape[0]
    alpha = (2 * depth) ** 0.25
    n_pages = page_table.shape[1]
    page = cache_mla_ckv.shape[2]
    past = n_pages * page
    Mp, Ms = B * T, Bs * ts
    ts8 = 8 * (-(-ts // 8))

    ql = mla_q_norm.shape[1]
    kvl = cache_mla_ckv.shape[-1]
    rope = cache_mla_kpe.shape[-1]
    H = mla_w_uk.shape[2]
    nope = mla_w_uk.shape[3]
    vh = mla_w_uv.shape[3]
    mla_scale = (nope + rope) ** -0.5
    KVH, HD = cache_dsa_k.shape[-2:]
    ID = cache_dsa_idx_k.shape[-1]
    NH = dsa_w_o.shape[1] // HD
    G = NH // KVH
    dq, dkv = NH * HD, KVH * HD
    IH = (dsa_w_in.shape[2] - dq - 2 * dkv - ID) // (ID + 1)
    dsa_scale = HD ** -0.5
    idx_wscale = (IH ** -0.5) * (ID ** -0.5)
    tq_dsa = min(256, T)

    pos_p = jnp.arange(T, dtype=jnp.int32)
    pos_s = jnp.tile(past + jnp.arange(ts, dtype=jnp.int32), Bs)
    tabs = {
        "mla": (_rot_tables(pos_p, rope // 2, rope), _rot_tables(pos_s, rope // 2, rope)),
        "idx": (_rot_tables(pos_p, IDX_ROPE // 2, ID), _rot_tables(pos_s, IDX_ROPE // 2, ID)),
        "tail": (_rot_tables(pos_p, IDX_ROPE // 2, LANES, (ID, ID + IH, idx_wscale)),
                 _rot_tables(pos_s, IDX_ROPE // 2, LANES, (ID, ID + IH, idx_wscale))),
    }

    c_all = jnp.concatenate([c_prompt, c_sample], axis=0)
    nc_rows = c_all.shape[0]
    c_all = jnp.pad(c_all, ((0, -nc_rows % 8), (0, 0)))
    mods = [_mm(c_all, w_ada[i], bias=b_ada[i].reshape(1, -1), act="silu", tm=c_all.shape[0], tn=1024,
                name="ada_mod") for i in range(depth)]

    def mod_vectors(i):
        m = mods[i]
        mp = [m[:B, j * D:(j + 1) * D].reshape(B, 1, D) for j in range(6)]
        ms = [jnp.repeat(m[B:B + Bs, j * D:(j + 1) * D], ts, axis=0) for j in range(6)]
        return mp, ms

    groups = [dict(y=x_prompt.reshape(Mp, D), rpg=T, nb=B, tb=T, g=0),
              dict(y=x_sample.reshape(Ms, D), rpg=None, nb=1, tb=Ms, g=1)]
    modv = [mod_vectors(i) for i in range(depth)]
    for grp in groups:
        sh1, sc1 = modv[0][grp["g"]][0], modv[0][grp["g"]][1]
        grp["u"] = _modulate(grp["y"], sh1, sc1, grp["rpg"])

    t0 = t1 = tp = tn = far_bias = None
    if depth > 1:
        t0, t1, tp, tn = _bias_tables(rel_bias, tq_dsa, ts8)
        far_bias = rel_bias[-1]

    outs = {k: [[], []] for k in ("ckv", "kpe", "k", "v", "ik")}
    for i in range(depth):
        j = i // 2
        if i % 2 == 0:
            w_in_pad = jnp.pad(mla_w_in[j], ((0, 0), (0, LANES - rope))).astype(BF16)
            w_uq = mla_w_uq[j].reshape(ql, H, nope + rope)
            w_nope = w_uq[:, :, :nope].reshape(ql, H * nope).astype(BF16)
            w_pe = w_uq[:, :, nope:].reshape(ql, H * rope).astype(BF16)
            w_uk_t = jnp.transpose(mla_w_uk[j], (1, 2, 0)).astype(BF16)
            w_uv = mla_w_uv[j].reshape(kvl, H * vh).astype(BF16)
            w_o = mla_w_o[j].astype(BF16)
        else:
            w_in = dsa_w_in[j].astype(BF16)
            w_tail = jnp.pad(dsa_w_in[j][:, dq + 2 * dkv + IH * ID:], ((0, 0), (0, LANES - ID - IH))).astype(BF16)
            w_o = dsa_w_o[j].astype(BF16)
        w_down = ffn_w_down[i].astype(BF16)

        for grp in groups:
            g = grp["g"]
            u = grp["u"]
            mv = modv[i][g]
            if i % 2 == 0:
                ct, st = tabs["mla"][g]
                cq, ckv, ckv_b, kpe, kpe_b = _mla_in(u, w_in_pad, mla_q_norm[j], mla_kv_norm[j], ct, st,
                                                     ql, kvl, rope)
                q_lat, q_pe = _mla_q(cq, w_nope, w_pe, w_uk_t, ct, st, grp["nb"], grp["tb"], H, nope, rope, kvl)
                if g == 0:
                    o = _mla_prompt_attn(q_lat, q_pe, ckv_b.reshape(B, T, kvl), kpe_b.reshape(B, T, rope),
                                         w_uv, mla_scale)
                else:
                    def rows(a):
                        n = a.shape[-1]
                        return a.reshape(H, Bs, ts, n).transpose(1, 0, 2, 3).reshape(Bs, H * ts, n)
                    o = _mla_sample_attn(rows(q_lat), rows(q_pe), cache_mla_ckv[j], cache_mla_kpe[j], page_table,
                                         ckv.reshape(Bs, ts, kvl), kpe.reshape(Bs, ts, rope), w_uv, mla_scale, H)
                    o = o.reshape(Ms, H * vh).astype(BF16)
                outs["ckv"][g].append(ckv)
                outs["kpe"][g].append(kpe)
            else:
                q = _mm(u, w_in, n_cols=dq, col0=0, out_dtype=BF16, name="dsa_q")
                k, k_b = _mm(u, w_in, n_cols=dkv, col0=dq, second_dtype=BF16, name="dsa_k")
                v, v_b = _mm(u, w_in, n_cols=dkv, col0=dq + dkv, second_dtype=BF16, name="dsa_v")
                ci, si = tabs["idx"][g]
                qi = _mm(u, w_in, n_cols=IH * ID, col0=dq + 2 * dkv, out_dtype=BF16,
                         rot=(ci, si, IDX_ROPE // 2), name="dsa_qi")
                ctl, stl = tabs["tail"][g]
                tail = _mm(u, w_tail, rot=(ctl, stl, IDX_ROPE // 2), name="dsa_tail")
                ki = tail[:, :ID]
                wi = tail[:, ID:ID + IH]
                if g == 0:
                    o = _dsa_prompt(q, qi, wi, k_b, v_b, ki.astype(BF16), t0, t1, far_bias, B, T,
                                    min(TOPK_MAX, T // 4), IH, ID, KVH, G, HD, dsa_scale, tq_dsa)
                else:
                    pad_t = ((0, 0), (0, 0), (0, ts8 - ts), (0, 0))
                    qi_rows = jnp.pad(qi.reshape(Bs, ts, IH, ID).transpose(0, 2, 1, 3), pad_t
                                      ).reshape(Bs, IH * ts8, ID)
                    wi_rows = jnp.pad(wi.reshape(Bs, ts, IH).transpose(0, 2, 1), ((0, 0), (0, 0), (0, ts8 - ts))
                                      ).reshape(Bs, IH * ts8, 1)
                    mask, mask_new = _dsa_sample_index(qi_rows, wi_rows, cache_dsa_idx_k[j], page_table,
                                                       ki.reshape(Bs, ts, ID), min(TOPK_MAX, (past + ts) // 4),
                                                       IH, ts, ts8)
                    q_rows = jnp.pad(q.reshape(Bs, ts, NH, HD).transpose(0, 2, 1, 3), pad_t
                                     ).reshape(Bs, KVH, G * ts8, HD)
                    o = _dsa_sample_attn(q_rows, mask, mask_new,
                                         cache_dsa_k[j].reshape(-1, page, dkv), cache_dsa_v[j].reshape(-1, page, dkv),
                                         page_table, k.reshape(Bs, ts, dkv), v.reshape(Bs, ts, dkv), tp, tn, far_bias,
                                         KVH, G, HD, ts, ts8, dsa_scale)
                    o = o.reshape(Ms, NH * HD).astype(BF16)
                outs["k"][g].append(k)
                outs["v"][g].append(v)
                outs["ik"][g].append(ki)
            y1, u2 = _mm_postnorm(o, w_o, grp["y"], mv[2], ln_g[i, 0], ln_b[i, 0], (mv[3], mv[4]), grp["rpg"],
                                  alpha, name="attn_out_postnorm")
            hmid = _swiglu_up(u2, ffn_w_gu[i])
            nxt = None
            if i + 1 < depth:
                nmv = modv[i + 1][g]
                nxt = (nmv[0], nmv[1])
            grp["y"], grp["u"] = _mm_postnorm(hmid, w_down, y1, mv[5], ln_g[i, 1], ln_b[i, 1], nxt, grp["rpg"],
                                              alpha, name="ffn_down_postnorm")

    def stack(key, g, shape):
        return jnp.stack([a.reshape(shape) for a in outs[key][g]])

    yp = groups[0]["y"].reshape(B, T, D)
    ys = groups[1]["y"].reshape(Bs, ts, D)
    return (yp, ys,
            stack("ckv", 0, (B, T, kvl)), stack("kpe", 0, (B, T, rope)),
            stack("k", 0, (B, T, KVH, HD)), stack("v", 0, (B, T, KVH, HD)), stack("ik", 0, (B, T, ID)),
            stack("ckv", 1, (Bs, ts, kvl)), stack("kpe", 1, (Bs, ts, rope)),
            stack("k", 1, (Bs, ts, KVH, HD)), stack("v", 1, (Bs, ts, KVH, HD)), stack("ik", 1, (Bs, ts, ID)))
```

```python
import functools
import math

import jax
import jax.numpy as jnp
from jax import lax
from jax.experimental import pallas as pl
from jax.experimental.pallas import tpu as pltpu

F32 = jnp.float32
BF16 = jnp.bfloat16
NEG_INF = float("-inf")

LANES = 128
ROPE_THETA = 10000.0
IDX_ROPE = 32
TOPK_MAX = 256
REL_MAX_DIST = 128
LN_EPS = 1e-5
RMS_EPS = 1e-6
BISECT_ITERS = 32
VMEM_LIMIT_MB = 56


def _cparams(sem, vmem_mb=VMEM_LIMIT_MB):
    return pltpu.CompilerParams(dimension_semantics=sem, vmem_limit_bytes=vmem_mb * 1024 * 1024)


def _dot(a, b):
    return jnp.dot(a, b, preferred_element_type=F32)


def _dot_nt(a, b):
    return lax.dot_general(a, b, (((1,), (1,)), ((), ())), preferred_element_type=F32)


def _rot(v, c, s, half):
    lane = lax.broadcasted_iota(jnp.int32, v.shape, 1)
    partner = jnp.where((lane % (2 * half)) < half,
                        pltpu.roll(v, LANES - half, 1), pltpu.roll(v, half, 1))
    return v * c + partner * s


def _mm_body(*refs, nk, act, rot_half, has_bias, second):
    it = iter(refs)
    x_ref = next(it)
    w_ref = next(it)
    b_ref = next(it) if has_bias else None
    c_ref = next(it) if rot_half else None
    s_ref = next(it) if rot_half else None
    o_ref = next(it)
    o2_ref = next(it) if second else None
    acc_ref = next(it) if nk > 1 else None

    x = x_ref[...]
    if act == "silu":
        xf = x.astype(F32)
        x = xf * jax.nn.sigmoid(xf)
    part = _dot(x.astype(BF16), w_ref[...].astype(BF16))

    def finish(acc):
        if has_bias:
            acc = acc + b_ref[...]
        if rot_half:
            c = c_ref[...]
            s = s_ref[...]
            for j in range(acc.shape[1] // LANES):
                sl = slice(j * LANES, (j + 1) * LANES)
                r = _rot(acc[:, sl], c, s, rot_half)
                o_ref[:, sl] = r.astype(o_ref.dtype)
                if second:
                    o2_ref[:, sl] = r.astype(o2_ref.dtype)
        else:
            o_ref[...] = acc.astype(o_ref.dtype)
            if second:
                o2_ref[...] = acc.astype(o2_ref.dtype)

    if nk == 1:
        finish(part)
    else:
        k = pl.program_id(2)

        @pl.when(k == 0)
        def _():
            acc_ref[...] = part

        @pl.when(k > 0)
        def _():
            acc_ref[...] += part

        @pl.when(k == nk - 1)
        def _():
            finish(acc_ref[...])


def _mm(x, w, *, layer=None, n_cols=None, col0=0, tm=512, tn=512, tk=None, out_dtype=F32, second_dtype=None,
        act=None, bias=None, rot=None, name="mm"):
    M, K = x.shape
    N = n_cols if n_cols is not None else w.shape[-1]
    tm = min(tm, M)
    tn = min(tn, N)
    while N % tn or col0 % tn:
        tn //= 2
    tk = K if tk is None else min(tk, K)
    assert M % tm == 0 and K % tk == 0 and tn % LANES == 0
    nk = K // tk
    jb = col0 // tn
    if layer is None:
        w_spec = pl.BlockSpec((tk, tn), lambda i, j, k: (k, j + jb))
    else:
        w_spec = pl.BlockSpec((None, tk, tn), lambda i, j, k: (layer, k, j + jb))
    in_specs = [pl.BlockSpec((tm, tk), lambda i, j, k: (i, k)), w_spec]
    args = [x, w]
    if bias is not None:
        in_specs.append(pl.BlockSpec((1, tn), lambda i, j, k: (0, j)))
        args.append(bias)
    rot_half = 0
    if rot is not None:
        c_tab, s_tab, rot_half = rot
        nr = c_tab.shape[0] // tm
        assert c_tab.shape[0] % tm == 0
        for t in (c_tab, s_tab):
            in_specs.append(pl.BlockSpec((tm, LANES), lambda i, j, k: (i % nr, 0)))
            args.append(t)
    out_shape = [jax.ShapeDtypeStruct((M, N), out_dtype)]
    out_specs = [pl.BlockSpec((tm, tn), lambda i, j, k: (i, j))]
    if second_dtype is not None:
        out_shape.append(jax.ShapeDtypeStruct((M, N), second_dtype))
        out_specs.append(pl.BlockSpec((tm, tn), lambda i, j, k: (i, j)))
    scratch = [pltpu.VMEM((tm, tn), F32)] if nk > 1 else []
    res = pl.pallas_call(
        functools.partial(_mm_body, nk=nk, act=act, rot_half=rot_half, has_bias=bias is not None,
                          second=second_dtype is not None),
        grid=(M // tm, N // tn, nk),
        in_specs=in_specs, out_specs=out_specs, out_shape=out_shape, scratch_shapes=scratch,
        compiler_params=_cparams(("parallel", "parallel", "arbitrary")), name=name)(*args)
    return res if second_dtype is not None else res[0]


def _modulate_body(y_ref, sh_ref, sc_ref, u_ref):
    u_ref[...] = (y_ref[...] * (1.0 + sc_ref[...]) + sh_ref[...]).astype(u_ref.dtype)


def _mod_spec(tm, n, rows_per_group):
    if rows_per_group is None:
        return pl.BlockSpec((tm, n), lambda i, *_: (i, 0))
    per = rows_per_group // tm
    return pl.BlockSpec((None, 1, n), lambda i, *_: (i // per, 0, 0))


def _modulate(y, sh, sc, rows_per_group, tm=512):
    M, D = y.shape
    tm = min(tm, M)
    return pl.pallas_call(
        _modulate_body, grid=(M // tm,),
        in_specs=[pl.BlockSpec((tm, D), lambda i: (i, 0)), _mod_spec(tm, D, rows_per_group),
                  _mod_spec(tm, D, rows_per_group)],
        out_specs=pl.BlockSpec((tm, D), lambda i: (i, 0)),
        out_shape=jax.ShapeDtypeStruct((M, D), BF16),
        compiler_params=_cparams(("parallel",)), name="modulate")(y, sh, sc)


def _mm_postnorm_body(*refs, nk, alpha, with_mod):
    if with_mod:
        x_ref, w_ref, y_ref, g_ref, lg_ref, lb_ref, sh_ref, sc_ref, yo_ref, u_ref, acc_ref = refs
    else:
        x_ref, w_ref, y_ref, g_ref, lg_ref, lb_ref, yo_ref, acc_ref = refs
    k = pl.program_id(1)
    part = _dot(x_ref[...].astype(BF16), w_ref[...].astype(BF16))

    @pl.when(k == 0)
    def _():
        acc_ref[...] = part

    @pl.when(k > 0)
    def _():
        acc_ref[...] += part

    @pl.when(k == nk - 1)
    def _():
        z = alpha * y_ref[...] + g_ref[...] * acc_ref[...]
        mu = jnp.mean(z, axis=-1, keepdims=True)
        zc = z - mu
        var = jnp.mean(zc * zc, axis=-1, keepdims=True)
        yn = zc * lax.rsqrt(var + LN_EPS) * lg_ref[...] + lb_ref[...]
        yo_ref[...] = yn
        if with_mod:
            u_ref[...] = (yn * (1.0 + sc_ref[...]) + sh_ref[...]).astype(u_ref.dtype)


def _mm_postnorm(x, w, y, gate, ln_g, ln_b, mod, rows_per_group, alpha, tm=512, tk=512, name="mm_postnorm"):
    M, K = x.shape
    N = w.shape[1]
    tm = min(tm, M)
    tk = min(tk, K)
    while K % tk:
        tk //= 2
    assert M % tm == 0 and tk % LANES == 0
    nk = K // tk
    row = pl.BlockSpec((tm, N), lambda i, k: (i, 0))
    vec = pl.BlockSpec((1, N), lambda i, k: (0, 0))
    in_specs = [pl.BlockSpec((tm, tk), lambda i, k: (i, k)), pl.BlockSpec((tk, N), lambda i, k: (k, 0)),
                row, _mod_spec(tm, N, rows_per_group), vec, vec]
    args = [x, w, y, gate, ln_g.reshape(1, N), ln_b.reshape(1, N)]
    out_shape = [jax.ShapeDtypeStruct((M, N), F32)]
    out_specs = [row]
    if mod is not None:
        in_specs += [_mod_spec(tm, N, rows_per_group)] * 2
        args += list(mod)
        out_shape.append(jax.ShapeDtypeStruct((M, N), BF16))
        out_specs.append(row)
    res = pl.pallas_call(
        functools.partial(_mm_postnorm_body, nk=nk, alpha=alpha, with_mod=mod is not None),
        grid=(M // tm, nk), in_specs=in_specs, out_specs=out_specs, out_shape=out_shape,
        scratch_shapes=[pltpu.VMEM((tm, N), F32)],
        compiler_params=_cparams(("parallel", "arbitrary")), name=name)(*args)
    return (res[0], res[1]) if mod is not None else (res[0], None)


def _swiglu_body(x_ref, wg_ref, wu_ref, o_ref):
    x = x_ref[...]
    g = _dot(x, wg_ref[...].astype(BF16))
    u = _dot(x, wu_ref[...].astype(BF16))
    o_ref[...] = (g * jax.nn.sigmoid(g) * u).astype(o_ref.dtype)


def _swiglu_up(x, w_gu, layer, tm=1024, tn=512):
    M, K = x.shape
    F = w_gu.shape[2] // 2
    tm = min(tm, M)
    tn = min(tn, F)
    while F % tn:
        tn //= 2
    nj = F // tn
    return pl.pallas_call(
        _swiglu_body, grid=(M // tm, nj),
        in_specs=[pl.BlockSpec((tm, K), lambda i, j: (i, 0)),
                  pl.BlockSpec((None, K, tn), lambda i, j: (layer, 0, j)),
                  pl.BlockSpec((None, K, tn), lambda i, j: (layer, 0, j + nj))],
        out_specs=pl.BlockSpec((tm, tn), lambda i, j: (i, j)),
        out_shape=jax.ShapeDtypeStruct((M, F), BF16),
        compiler_params=_cparams(("parallel", "parallel")), name="swiglu_up")(x, w_gu, w_gu)


def _mla_in_body(x_ref, w_ref, qg_ref, kvg_ref, c_ref, s_ref, cq_ref, ckv_ref, ckvb_ref, kpe_ref, kpeb_ref,
                 *, ql, kvl, rope):
    acc = _dot(x_ref[...], w_ref[...])

    def rms(v, g):
        return v * lax.rsqrt(jnp.mean(v * v, axis=-1, keepdims=True) + RMS_EPS) * g

    cq_ref[...] = rms(acc[:, :ql], qg_ref[...]).astype(cq_ref.dtype)
    ckv = rms(acc[:, ql:ql + kvl], kvg_ref[...])
    ckv_ref[...] = ckv
    ckvb_ref[...] = ckv.astype(ckvb_ref.dtype)
    kpe = _rot(acc[:, ql + kvl:], c_ref[...], s_ref[...], rope // 2)[:, :rope]
    kpe_ref[...] = kpe
    kpeb_ref[...] = kpe.astype(kpeb_ref.dtype)


def _mla_in(u, w_pad, q_g, kv_g, c_tab, s_tab, ql, kvl, rope, tm=512):
    M, K = u.shape
    N = w_pad.shape[1]
    tm = min(tm, M)
    nr = c_tab.shape[0] // tm
    tab = pl.BlockSpec((tm, LANES), lambda i: (i % nr, 0))

    def row(n):
        return pl.BlockSpec((tm, n), lambda i: (i, 0))

    return pl.pallas_call(
        functools.partial(_mla_in_body, ql=ql, kvl=kvl, rope=rope), grid=(M // tm,),
        in_specs=[row(K), pl.BlockSpec((K, N), lambda i: (0, 0)),
                  pl.BlockSpec((1, ql), lambda i: (0, 0)), pl.BlockSpec((1, kvl), lambda i: (0, 0)), tab, tab],
        out_specs=[row(ql), row(kvl), row(kvl), row(rope), row(rope)],
        out_shape=[jax.ShapeDtypeStruct((M, ql), BF16), jax.ShapeDtypeStruct((M, kvl), F32),
                   jax.ShapeDtypeStruct((M, kvl), BF16), jax.ShapeDtypeStruct((M, rope), F32),
                   jax.ShapeDtypeStruct((M, rope), BF16)],
        compiler_params=_cparams(("parallel",)), name="mla_in")(
            u, w_pad, q_g.reshape(1, ql), kv_g.reshape(1, kvl), c_tab, s_tab)


def _mla_q_body(cq_ref, wn_ref, wp_ref, wuk_ref, c_ref, s_ref, ql_ref, qp_ref, *, H, nope, rope):
    cq = cq_ref[...]
    qn = _dot(cq, wn_ref[...]).astype(BF16)
    for h in range(H):
        ql_ref[0, h] = _dot(qn[:, h * nope:(h + 1) * nope], wuk_ref[h]).astype(ql_ref.dtype)
    qp = _dot(cq, wp_ref[...])
    c = c_ref[...]
    s = s_ref[...]
    per = LANES // rope
    for j in range(H // per):
        r = _rot(qp[:, j * LANES:(j + 1) * LANES], c, s, rope // 2)
        for e in range(per):
            qp_ref[0, j * per + e] = r[:, e * rope:(e + 1) * rope].astype(qp_ref.dtype)


def _mla_q(cq, w_nope, w_pe, w_uk_t, c_tab, s_tab, nb, tb, H, nope, rope, kvl, tm=512):
    M, ql = cq.shape
    tm = min(tm, tb)
    per = tb // tm
    nr = c_tab.shape[0] // tm
    tab = pl.BlockSpec((tm, LANES), lambda i: (i % nr, 0))
    full2 = lambda a: pl.BlockSpec(a.shape, lambda i: (0, 0))
    return pl.pallas_call(
        functools.partial(_mla_q_body, H=H, nope=nope, rope=rope), grid=(M // tm,),
        in_specs=[pl.BlockSpec((tm, ql), lambda i: (i, 0)), full2(w_nope), full2(w_pe),
                  pl.BlockSpec(w_uk_t.shape, lambda i: (0, 0, 0)), tab, tab],
        out_specs=[pl.BlockSpec((1, H, tm, kvl), lambda i: (i // per, 0, i % per, 0)),
                   pl.BlockSpec((1, H, tm, rope), lambda i: (i // per, 0, i % per, 0))],
        out_shape=[jax.ShapeDtypeStruct((nb, H, tb, kvl), BF16), jax.ShapeDtypeStruct((nb, H, tb, rope), BF16)],
        compiler_params=_cparams(("parallel",)), name="mla_q")(cq, w_nope, w_pe, w_uk_t, c_tab, s_tab)


def _flash_update(s, v, m_ref, l_ref, acc_ref):
    m_old = m_ref[...]
    m_new = jnp.maximum(m_old, jnp.max(s, axis=-1, keepdims=True))
    m_safe = jnp.where(m_new == NEG_INF, 0.0, m_new)
    a = jnp.exp(m_old - m_safe)
    p = jnp.exp(s - m_safe)
    l_ref[...] = a * l_ref[...] + jnp.sum(p, axis=-1, keepdims=True)
    acc_ref[...] = a * acc_ref[...] + _dot(p.astype(BF16), v)
    m_ref[...] = m_new


def _mla_prompt_body(ql_ref, qp_ref, ckv_ref, kpe_ref, wuv_ref, o_ref, m_ref, l_ref, acc_ref,
                     *, H, tq, tk, nk, scale, vh):
    i = pl.program_id(1)
    j = pl.program_id(2)

    @pl.when(j == 0)
    def _():
        m_ref[...] = jnp.full(m_ref.shape, NEG_INF, F32)
        l_ref[...] = jnp.zeros(l_ref.shape, F32)
        acc_ref[...] = jnp.zeros(acc_ref.shape, F32)

    def step(masked):
        ql = ql_ref[0].reshape(H * tq, ql_ref.shape[-1])
        qp = qp_ref[0].reshape(H * tq, qp_ref.shape[-1])
        ckv = ckv_ref[0]
        s = (_dot_nt(ql, ckv) + _dot_nt(qp, kpe_ref[0])) * scale
        if masked:
            tok = i * tq + (lax.broadcasted_iota(jnp.int32, s.shape, 0) % tq)
            key = j * tk + lax.broadcasted_iota(jnp.int32, s.shape, 1)
            s = jnp.where(key <= tok, s, NEG_INF)
        _flash_update(s, ckv, m_ref, l_ref, acc_ref)

    first_key, last_key = j * tk, j * tk + tk - 1

    @pl.when(last_key <= i * tq)
    def _():
        step(False)

    @pl.when((last_key > i * tq) & (first_key <= i * tq + tq - 1))
    def _():
        step(True)

    @pl.when(j == nk - 1)
    def _():
        o = (acc_ref[...] / l_ref[...]).astype(BF16)
        for h in range(H):
            o_ref[:, h * vh:(h + 1) * vh] = _dot(o[h * tq:(h + 1) * tq], wuv_ref[:, h * vh:(h + 1) * vh]
                                                 ).astype(o_ref.dtype)


def _mla_prompt_attn(q_lat, q_pe, ckv_b, kpe_b, w_uv, scale, tq=256, tk=256):
    B, H, T, C = q_lat.shape
    R = q_pe.shape[-1]
    tq = min(tq, T)
    tk = min(tk, T)
    nq, nk = T // tq, T // tk
    vh = w_uv.shape[1] // H

    def kmap(b, i, j):
        return (b, jnp.minimum(j, (i * tq + tq - 1) // tk), 0)

    return pl.pallas_call(
        functools.partial(_mla_prompt_body, H=H, tq=tq, tk=tk, nk=nk, scale=scale, vh=vh),
        grid=(B, nq, nk),
        in_specs=[pl.BlockSpec((1, H, tq, C), lambda b, i, j: (b, 0, i, 0)),
                  pl.BlockSpec((1, H, tq, R), lambda b, i, j: (b, 0, i, 0)),
                  pl.BlockSpec((1, tk, C), kmap), pl.BlockSpec((1, tk, R), kmap),
                  pl.BlockSpec(w_uv.shape, lambda b, i, j: (0, 0))],
        out_specs=pl.BlockSpec((tq, H * vh), lambda b, i, j: (b * nq + i, 0)),
        out_shape=jax.ShapeDtypeStruct((B * T, H * vh), BF16),
        scratch_shapes=[pltpu.VMEM((H * tq, 1), F32), pltpu.VMEM((H * tq, 1), F32),
                        pltpu.VMEM((H * tq, C), F32)],
        compiler_params=_cparams(("parallel", "parallel", "arbitrary")), name="mla_prompt_attn")(
            q_lat, q_pe, ckv_b, kpe_b, w_uv)


def _new_key_scores(q_parts, k_parts, n_new):
    cols = []
    for t in range(n_new):
        acc = None
        for q, k in zip(q_parts, k_parts):
            d = jnp.sum(q * k[t:t + 1, :], axis=-1, keepdims=True)
            acc = d if acc is None else acc + d
        cols.append(acc)
    return cols


def _cols_to_block(cols, rows):
    lane = lax.broadcasted_iota(jnp.int32, (rows, LANES), 1)
    blk = jnp.full((rows, LANES), NEG_INF, F32)
    for t, c in enumerate(cols):
        blk = jnp.where(lane == t, c, blk)
    return blk


def _flash_update_new(s_blk, v_new, n_new, m_ref, l_ref, acc_ref):
    m_old = m_ref[...]
    m_new = jnp.maximum(m_old, jnp.max(s_blk, axis=-1, keepdims=True))
    m_safe = jnp.where(m_new == NEG_INF, 0.0, m_new)
    a = jnp.exp(m_old - m_safe)
    p = jnp.exp(s_blk - m_safe)
    l_ref[...] = a * l_ref[...] + jnp.sum(p, axis=-1, keepdims=True)
    acc = a * acc_ref[...]
    for t in range(n_new):
        acc = acc + p[:, t:t + 1] * v_new[t:t + 1, :]
    acc_ref[...] = acc
    m_ref[...] = m_new


def _mla_sample_body(*refs, pps, H, ts, page, nc, scale, vh):
    pt_ref = refs[0]
    ql_ref, qp_ref = refs[1], refs[2]
    ckv_refs = refs[3:3 + pps]
    kpe_refs = refs[3 + pps:3 + 2 * pps]
    cn_ref, kn_ref, wuv_ref, o_ref, kc_ref, kp_ref, m_ref, l_ref, acc_ref = refs[3 + 2 * pps:]
    del pt_ref
    c = pl.program_id(1)
    R = H * ts

    @pl.when(c == 0)
    def _():
        m_ref[...] = jnp.full(m_ref.shape, NEG_INF, F32)
        l_ref[...] = jnp.zeros(l_ref.shape, F32)
        acc_ref[...] = jnp.zeros(acc_ref.shape, F32)

    for r in range(pps):
        kc_ref[r * page:(r + 1) * page, :] = ckv_refs[r][...].astype(BF16)
        kp_ref[:, r * page:(r + 1) * page] = kpe_refs[r][...].astype(BF16)
    ql = ql_ref[0]
    qp = qp_ref[0]
    ckv = kc_ref[...]
    s = (_dot_nt(ql, ckv) + _dot(qp, kp_ref[...])) * scale
    _flash_update(s, ckv, m_ref, l_ref, acc_ref)

    @pl.when(c == nc - 1)
    def _():
        cn = cn_ref[0]
        cols = _new_key_scores([ql.astype(F32), qp.astype(F32)], [cn, kn_ref[0]], ts)
        blk = _cols_to_block(cols, R) * scale
        qt = lax.broadcasted_iota(jnp.int32, (R, LANES), 0) % ts
        lane = lax.broadcasted_iota(jnp.int32, (R, LANES), 1)
        blk = jnp.where(lane <= qt, blk, NEG_INF)
        _flash_update_new(blk, cn, ts, m_ref, l_ref, acc_ref)
        o = (acc_ref[...] / l_ref[...]).astype(BF16)
        full = _dot(o, wuv_ref[...])
        o_ref[0] = jnp.concatenate([full[h * ts:(h + 1) * ts, h * vh:(h + 1) * vh] for h in range(H)], axis=1)


def _mla_sample_attn(q_lat, q_pe, pool_ckv, pool_kpe_t, layer, page_table, ckv_new, kpe_new, w_uv, scale, H,
                     pps=16):
    Bs, R, C = q_lat.shape
    Rr = q_pe.shape[-1]
    ts = R // H
    n_pages = page_table.shape[1]
    page = pool_ckv.shape[2]
    pps = min(pps, n_pages)
    nc = n_pages // pps
    vh = w_uv.shape[1] // H

    def pmap(r):
        return lambda b, c, pt: (layer, pt[b * n_pages + c * pps + r], 0, 0)

    in_specs = [pl.BlockSpec((1, R, C), lambda b, c, pt: (b, 0, 0)),
                pl.BlockSpec((1, R, Rr), lambda b, c, pt: (b, 0, 0))]
    in_specs += [pl.BlockSpec((None, None, page, C), pmap(r)) for r in range(pps)]
    in_specs += [pl.BlockSpec((None, None, Rr, page), pmap(r)) for r in range(pps)]
    in_specs += [pl.BlockSpec((1, ts, C), lambda b, c, pt: (b, 0, 0)),
                 pl.BlockSpec((1, ts, Rr), lambda b, c, pt: (b, 0, 0)),
                 pl.BlockSpec(w_uv.shape, lambda b, c, pt: (0, 0))]
    grid_spec = pltpu.PrefetchScalarGridSpec(
        num_scalar_prefetch=1, grid=(Bs, nc), in_specs=in_specs,
        out_specs=pl.BlockSpec((1, ts, H * vh), lambda b, c, pt: (b, 0, 0)),
        scratch_shapes=[pltpu.VMEM((pps * page, C), BF16), pltpu.VMEM((Rr, pps * page), BF16),
                        pltpu.VMEM((R, 1), F32), pltpu.VMEM((R, 1), F32), pltpu.VMEM((R, C), F32)])
    return pl.pallas_call(
        functools.partial(_mla_sample_body, pps=pps, H=H, ts=ts, page=page, nc=nc, scale=scale, vh=vh),
        grid_spec=grid_spec, out_shape=jax.ShapeDtypeStruct((Bs, ts, H * vh), F32),
        compiler_params=_cparams(("parallel", "arbitrary")), name="mla_sample_attn")(
            page_table.reshape(-1), q_lat, q_pe, *([pool_ckv] * pps), *([pool_kpe_t] * pps),
            ckv_new, kpe_new, w_uv)


def _t5_bias_of_dist(d, rb_ref, h, n_buckets):
    dist = jnp.maximum(d, 0)
    exact = n_buckets // 2
    df = jnp.maximum(dist, 1).astype(F32)
    large = exact + (jnp.log(df / exact) / math.log(REL_MAX_DIST / exact) * (n_buckets - exact)).astype(jnp.int32)
    large = jnp.minimum(large, n_buckets - 1)
    bucket = jnp.where(dist < exact, dist, large)
    out = jnp.zeros(d.shape, F32)
    for b in range(n_buckets):
        out = jnp.where(bucket == b, rb_ref[b, h], out)
    return out


def _bias_tables_body(rb_ref, t0_ref, t1_ref, tp_ref, tn_ref, *, NH, tq, ts8, n_buckets):
    r = lax.broadcasted_iota(jnp.int32, (tq, tq), 0)
    c = lax.broadcasted_iota(jnp.int32, (tq, tq), 1)
    r8 = lax.broadcasted_iota(jnp.int32, (ts8, LANES), 0)
    c8 = lax.broadcasted_iota(jnp.int32, (ts8, LANES), 1)
    for h in range(NH):
        t0_ref[h] = _t5_bias_of_dist(r - c, rb_ref, h, n_buckets)
        t1_ref[h] = _t5_bias_of_dist(r - c + tq, rb_ref, h, n_buckets)
        tp_ref[h] = _t5_bias_of_dist(r8 - c8 + LANES, rb_ref, h, n_buckets)
        tn_ref[h] = _t5_bias_of_dist(r8 - c8, rb_ref, h, n_buckets)


def _bias_tables(rel_bias, tq, ts8):
    nb, NH = rel_bias.shape
    return pl.pallas_call(
        functools.partial(_bias_tables_body, NH=NH, tq=tq, ts8=ts8, n_buckets=nb),
        in_specs=[pl.BlockSpec(memory_space=pltpu.SMEM)],
        out_shape=[jax.ShapeDtypeStruct((NH, tq, tq), F32), jax.ShapeDtypeStruct((NH, tq, tq), F32),
                   jax.ShapeDtypeStruct((NH, ts8, LANES), F32), jax.ShapeDtypeStruct((NH, ts8, LANES), F32)],
        compiler_params=pltpu.CompilerParams(vmem_limit_bytes=VMEM_LIMIT_MB * 1024 * 1024),
        name="t5_bias_tables")(rel_bias)


def _bisect(lo0, hi0, count_ge, k_sel):
    def body(_, lh):
        lo, hi = lh
        mid = 0.5 * (lo + hi)
        ge = count_ge(mid) >= k_sel
        return jnp.where(ge, mid, lo), jnp.where(ge, hi, mid)

    lo, _ = lax.fori_loop(0, BISECT_ITERS, body, (lo0, hi0))
    return lo


def _dsa_prompt_body(far_ref, q_ref, qi_ref, wi_ref, k_ref, v_ref, kit_ref, t0_ref, t1_ref, o_ref,
                     sc_ref, wb_ref, s_ref, lp_ref, acc_ref, *, tq, nq, k_sel, IH, ID, KVH, G, HD, scale):
    i = pl.program_id(1)
    row_tok = i * tq + lax.broadcasted_iota(jnp.int32, (tq, tq), 0)
    col_in = lax.broadcasted_iota(jnp.int32, (tq, tq), 1)
    qi = qi_ref[...]
    wi = wi_ref[...]
    for h in range(IH):
        wb_ref[h] = jnp.broadcast_to(wi[:, h:h + 1], (tq, tq))

    def idx_block(c, carry):
        lo, hi = carry
        kit = kit_ref[0, c]
        tot = jnp.zeros((tq, tq), F32)
        for h in range(IH):
            d = _dot(qi[:, h * ID:(h + 1) * ID], kit)
            tot = tot + jnp.maximum(d, 0.0) * wb_ref[h]
        ok = (c * tq + col_in) <= row_tok
        sc_ref[c] = jnp.where(ok, tot, NEG_INF)
        lo = jnp.minimum(lo, jnp.min(jnp.where(ok, tot, jnp.inf), axis=-1, keepdims=True))
        hi = jnp.maximum(hi, jnp.max(jnp.where(ok, tot, NEG_INF), axis=-1, keepdims=True))
        return lo, hi

    lo0, hi0 = lax.fori_loop(0, i + 1, idx_block,
                             (jnp.full((tq, 1), jnp.inf, F32), jnp.full((tq, 1), NEG_INF, F32)))

    def count_ge(mid):
        def blk(c, acc):
            t = sc_ref[c]
            part = jnp.zeros((tq, LANES), F32)
            for j in range(tq // LANES):
                part = part + jnp.where(t[:, j * LANES:(j + 1) * LANES] >= mid, 1.0, 0.0)
            return acc + part
        acc = lax.fori_loop(0, i + 1, blk, jnp.zeros((tq, LANES), F32))
        return jnp.sum(acc, axis=-1, keepdims=True)

    thr = _bisect(lo0, hi0, count_ge, float(k_sel))

    def to_mask(c, _):
        sc_ref[c] = jnp.where(sc_ref[c] >= thr, 0.0, NEG_INF)
        return 0

    lax.fori_loop(0, i + 1, to_mask, 0)

    q = q_ref[...]
    R = G * tq
    nlv = tq // LANES

    def lane_fold(x, op):
        r = x[:, :LANES]
        for jj in range(1, nlv):
            r = op(r, x[:, jj * LANES:(jj + 1) * LANES])
        return r

    for n in range(KVH):
        qn = jnp.concatenate([q[:, (n * G + g) * HD:(n * G + g + 1) * HD] for g in range(G)], axis=0)

        def score(c, slot, bias):
            kc = k_ref[0, pl.ds(pl.multiple_of(c * tq, tq), tq), n * HD:(n + 1) * HD]
            s = _dot_nt(qn, kc) * scale + bias + jnp.concatenate([sc_ref[c]] * G, axis=0)
            s_ref[slot] = s
            return lane_fold(s, jnp.maximum)

        far = jnp.concatenate([jnp.full((tq, 1), far_ref[n * G + g], F32) for g in range(G)], axis=0)
        mx = lax.fori_loop(0, jnp.maximum(i - 1, 0), lambda c, mx: jnp.maximum(mx, score(c, c, far)),
                           jnp.full((R, LANES), NEG_INF, F32))
        prev = score(jnp.maximum(i - 1, 0), jnp.where(i >= 1, i - 1, nq), t1_ref[n * G:(n + 1) * G].reshape(R, tq))
        mx = jnp.where(i >= 1, jnp.maximum(mx, prev), mx)
        mx = jnp.maximum(mx, score(i, i, t0_ref[n * G:(n + 1) * G].reshape(R, tq)))
        m = jnp.max(mx, axis=-1, keepdims=True)

        lp_ref[...] = jnp.zeros(lp_ref.shape, F32)
        acc_ref[...] = jnp.zeros(acc_ref.shape, F32)

        def pv(c, _):
            p = jnp.exp(s_ref[c] - m)
            lp_ref[...] += lane_fold(p, jnp.add)
            vc = v_ref[0, pl.ds(pl.multiple_of(c * tq, tq), tq), n * HD:(n + 1) * HD]
            acc_ref[...] += _dot(p.astype(BF16), vc)
            return 0

        lax.fori_loop(0, i + 1, pv, 0)
        o = acc_ref[...] / jnp.sum(lp_ref[...], axis=-1, keepdims=True)
        for g in range(G):
            o_ref[:, (n * G + g) * HD:(n * G + g + 1) * HD] = o[g * tq:(g + 1) * tq].astype(o_ref.dtype)


def _dsa_prompt(q, qi, wi, k_b, v_b, ki_t, t0, t1, far_bias, B, T, k_sel, IH, ID, KVH, G, HD, scale, tq):
    M = B * T
    nq = T // tq
    NH = KVH * G
    row = lambda n: pl.BlockSpec((tq, n), lambda b, i, f: (b * nq + i, 0))
    seq = lambda n: pl.BlockSpec((1, T, n), lambda b, i, f: (b, 0, 0))
    tab = pl.BlockSpec((NH, tq, tq), lambda b, i, f: (0, 0, 0))
    grid_spec = pltpu.PrefetchScalarGridSpec(
        num_scalar_prefetch=1, grid=(B, nq),
        in_specs=[row(NH * HD), row(IH * ID), row(IH), seq(KVH * HD), seq(KVH * HD),
                  pl.BlockSpec((1, nq, ID, tq), lambda b, i, f: (b, 0, 0, 0)), tab, tab],
        out_specs=row(NH * HD),
        scratch_shapes=[pltpu.VMEM((nq, tq, tq), F32), pltpu.VMEM((IH, tq, tq), F32),
                        pltpu.VMEM((nq + 1, G * tq, tq), F32), pltpu.VMEM((G * tq, LANES), F32),
                        pltpu.VMEM((G * tq, HD), F32)])
    return pl.pallas_call(
        functools.partial(_dsa_prompt_body, tq=tq, nq=nq, k_sel=k_sel, IH=IH, ID=ID, KVH=KVH, G=G, HD=HD,
                          scale=scale),
        grid_spec=grid_spec, out_shape=jax.ShapeDtypeStruct((M, NH * HD), BF16),
        compiler_params=_cparams(("parallel", "arbitrary")), name="dsa_prompt")(
            far_bias, q, qi, wi, k_b.reshape(B, T, -1), v_b.reshape(B, T, -1), ki_t, t0, t1)


def _dsa_sample_index_body(*refs, pps, IH, ts, ts8, page, nc):
    qi_ref, wi_ref = refs[1], refs[2]
    ik_refs = refs[3:3 + pps]
    kn_ref, sc_ref, scn_ref, kb_ref = refs[3 + pps:]
    c = pl.program_id(1)
    for r in range(pps):
        kb_ref[:, r * page:(r + 1) * page] = ik_refs[r][...].astype(BF16)
    qi = qi_ref[0]
    wcol = wi_ref[0]
    d = jnp.maximum(_dot(qi, kb_ref[...]), 0.0) * wcol
    tot = d[0:ts8]
    for h in range(1, IH):
        tot = tot + d[h * ts8:(h + 1) * ts8]
    sc_ref[0, 0] = tot

    @pl.when(c == 0)
    def _():
        cols = _new_key_scores([qi.astype(F32)], [kn_ref[0]], ts)
        ncols = []
        for col in cols:
            col = jnp.maximum(col, 0.0) * wcol
            t = col[0:ts8]
            for h in range(1, IH):
                t = t + col[h * ts8:(h + 1) * ts8]
            ncols.append(t)
        blk = _cols_to_block(ncols, ts8)
        qt = lax.broadcasted_iota(jnp.int32, (ts8, LANES), 0)
        lane = lax.broadcasted_iota(jnp.int32, (ts8, LANES), 1)
        scn_ref[0] = jnp.where(lane <= qt, blk, NEG_INF)


def _dsa_sample_index(qi_rows, wi_rows, pool_ik_t, layer, page_table, ki_new, IH, ts, ts8, pps=16):
    Bs, R, ID = qi_rows.shape
    n_pages = page_table.shape[1]
    page = pool_ik_t.shape[3]
    pps = min(pps, n_pages)
    nc = n_pages // pps
    W = pps * page

    def pmap(r):
        return lambda b, c, pt: (layer, pt[b * n_pages + c * pps + r], 0, 0)

    in_specs = [pl.BlockSpec((1, R, ID), lambda b, c, pt: (b, 0, 0)),
                pl.BlockSpec((1, R, 1), lambda b, c, pt: (b, 0, 0))]
    in_specs += [pl.BlockSpec((None, None, ID, page), pmap(r)) for r in range(pps)]
    in_specs += [pl.BlockSpec((1, ts, ID), lambda b, c, pt: (b, 0, 0))]
    grid_spec = pltpu.PrefetchScalarGridSpec(
        num_scalar_prefetch=1, grid=(Bs, nc), in_specs=in_specs,
        out_specs=[pl.BlockSpec((1, 1, ts8, W), lambda b, c, pt: (b, c, 0, 0)),
                   pl.BlockSpec((1, ts8, LANES), lambda b, c, pt: (b, 0, 0))],
        scratch_shapes=[pltpu.VMEM((ID, W), BF16)])
    return pl.pallas_call(
        functools.partial(_dsa_sample_index_body, pps=pps, IH=IH, ts=ts, ts8=ts8, page=page, nc=nc),
        grid_spec=grid_spec,
        out_shape=[jax.ShapeDtypeStruct((Bs, nc, ts8, W), F32), jax.ShapeDtypeStruct((Bs, ts8, LANES), F32)],
        compiler_params=_cparams(("parallel", "arbitrary")), name="dsa_sample_index")(
            page_table.reshape(-1), qi_rows, wi_rows, *([pool_ik_t] * pps), ki_new)


def _topk_threshold_body(sc_ref, scn_ref, thr_ref, *, nc, k_sel):
    bb, _, ts8, W = sc_ref.shape
    scn = scn_ref[...]
    lo0 = jnp.min(jnp.where(scn > NEG_INF, scn, jnp.inf), axis=-1, keepdims=True)
    hi0 = jnp.max(scn, axis=-1, keepdims=True)
    for c in range(nc):
        t = sc_ref[:, c]
        lo0 = jnp.minimum(lo0, jnp.min(t, axis=-1, keepdims=True))
        hi0 = jnp.maximum(hi0, jnp.max(t, axis=-1, keepdims=True))

    def count_ge(mid):
        acc = jnp.where(scn >= mid, 1.0, 0.0)
        for c in range(nc):
            for j in range(W // LANES):
                acc = acc + jnp.where(sc_ref[:, c, :, j * LANES:(j + 1) * LANES] >= mid, 1.0, 0.0)
        return jnp.sum(acc, axis=-1, keepdims=True)

    thr = _bisect(lo0, hi0, count_ge, float(k_sel))
    thr_ref[...] = jnp.broadcast_to(thr, thr_ref.shape)


def _topk_threshold(scores, scores_new, k_sel, bb=16):
    Bs, nc, ts8, W = scores.shape
    bb = min(bb, Bs)
    assert Bs % bb == 0
    return pl.pallas_call(
        functools.partial(_topk_threshold_body, nc=nc, k_sel=k_sel), grid=(Bs // bb,),
        in_specs=[pl.BlockSpec((bb, nc, ts8, W), lambda i: (i, 0, 0, 0)),
                  pl.BlockSpec((bb, ts8, LANES), lambda i: (i, 0, 0))],
        out_specs=pl.BlockSpec((bb, ts8, LANES), lambda i: (i, 0, 0)),
        out_shape=jax.ShapeDtypeStruct((Bs, ts8, LANES), F32),
        compiler_params=_cparams(("parallel",)), name="topk_threshold")(scores, scores_new)


def _dsa_sample_attn_body(*refs, pps, KVH, G, HD, ts, ts8, page, nc, scale):
    far_ref = refs[1]
    q_ref, sc_ref, scn_ref, thr_ref = refs[2], refs[3], refs[4], refs[5]
    k_refs = refs[6:6 + pps]
    v_refs = refs[6 + pps:6 + 2 * pps]
    kn_ref, vn_ref, tp_ref, tn_ref, o_ref, kb_ref, vb_ref, m_ref, l_ref, acc_ref = refs[6 + 2 * pps:]
    c = pl.program_id(1)
    R = G * ts8
    W = pps * page

    @pl.when(c == 0)
    def _():
        m_ref[...] = jnp.full(m_ref.shape, NEG_INF, F32)
        l_ref[...] = jnp.zeros(l_ref.shape, F32)
        acc_ref[...] = jnp.zeros(acc_ref.shape, F32)

    for r in range(pps):
        for n in range(KVH):
            kb_ref[r * page:(r + 1) * page, n * HD:(n + 1) * HD] = k_refs[r][pl.ds(n, page, stride=KVH), :].astype(BF16)
            vb_ref[r * page:(r + 1) * page, n * HD:(n + 1) * HD] = v_refs[r][pl.ds(n, page, stride=KVH), :].astype(BF16)
    thr = thr_ref[0][:, :1]
    msk = jnp.concatenate([jnp.where(sc_ref[0, 0] >= thr, 0.0, NEG_INF)] * G, axis=0)
    last = c == nc - 1
    for n in range(KVH):
        qn = q_ref[0, n]
        far = jnp.concatenate([jnp.full((ts8, 1), far_ref[n * G + g], F32) for g in range(G)], axis=0)
        near = tp_ref[n * G:(n + 1) * G].reshape(R, LANES)
        delta = jnp.where(last, near - far, 0.0)
        s = _dot_nt(qn, kb_ref[:, n * HD:(n + 1) * HD]) * scale + far + msk
        s = jnp.concatenate([s[:, :W - LANES], s[:, W - LANES:] + delta], axis=1)
        _flash_update(s, vb_ref[:, n * HD:(n + 1) * HD], m_ref.at[n], l_ref.at[n], acc_ref.at[n])

    @pl.when(last)
    def _():
        mskn = jnp.concatenate([jnp.where(scn_ref[0] >= thr, 0.0, NEG_INF)] * G, axis=0)
        outs = []
        for n in range(KVH):
            qn = q_ref[0, n].astype(F32)
            kn = kn_ref[0][:, n * HD:(n + 1) * HD]
            vn = vn_ref[0][:, n * HD:(n + 1) * HD]
            cols = _new_key_scores([qn], [kn], ts)
            blk = _cols_to_block(cols, R) * scale + tn_ref[n * G:(n + 1) * G].reshape(R, LANES) + mskn
            _flash_update_new(blk, vn, ts, m_ref.at[n], l_ref.at[n], acc_ref.at[n])
            o = acc_ref[n] / l_ref[n]
            outs += [o[g * ts8:g * ts8 + ts] for g in range(G)]
        o_ref[0] = jnp.concatenate(outs, axis=1)


def _dsa_sample_attn(q_rows, scores, scores_new, thr, pool_k, pool_v, layer, page_table, k_new, v_new, tp, tn,
                     far_bias, KVH, G, HD, ts, ts8, scale, pps=16):
    Bs = q_rows.shape[0]
    R = G * ts8
    n_pages = page_table.shape[1]
    page = pool_k.shape[2] // KVH
    pps = min(pps, n_pages)
    nc = n_pages // pps
    W = pps * page
    NH = KVH * G

    def pmap(r):
        return lambda b, c, pt, f: (layer, pt[b * n_pages + c * pps + r], 0, 0)

    in_specs = [pl.BlockSpec((1, KVH, R, HD), lambda b, c, pt, f: (b, 0, 0, 0)),
                pl.BlockSpec((1, 1, ts8, W), lambda b, c, pt, f: (b, c, 0, 0)),
                pl.BlockSpec((1, ts8, LANES), lambda b, c, pt, f: (b, 0, 0)),
                pl.BlockSpec((1, ts8, LANES), lambda b, c, pt, f: (b, 0, 0))]
    in_specs += [pl.BlockSpec((None, None, page * KVH, HD), pmap(r)) for r in range(pps)] * 2
    in_specs += [pl.BlockSpec((1, ts, KVH * HD), lambda b, c, pt, f: (b, 0, 0))] * 2
    in_specs += [pl.BlockSpec((NH, ts8, LANES), lambda b, c, pt, f: (0, 0, 0))] * 2
    grid_spec = pltpu.PrefetchScalarGridSpec(
        num_scalar_prefetch=2, grid=(Bs, nc), in_specs=in_specs,
        out_specs=pl.BlockSpec((1, ts, NH * HD), lambda b, c, pt, f: (b, 0, 0)),
        scratch_shapes=[pltpu.VMEM((W, KVH * HD), BF16), pltpu.VMEM((W, KVH * HD), BF16),
                        pltpu.VMEM((KVH, R, 1), F32), pltpu.VMEM((KVH, R, 1), F32), pltpu.VMEM((KVH, R, HD), F32)])
    return pl.pallas_call(
        functools.partial(_dsa_sample_attn_body, pps=pps, KVH=KVH, G=G, HD=HD, ts=ts, ts8=ts8, page=page,
                          nc=nc, scale=scale),
        grid_spec=grid_spec, out_shape=jax.ShapeDtypeStruct((Bs, ts, NH * HD), F32),
        compiler_params=_cparams(("parallel", "arbitrary")), name="dsa_sample_attn")(
            page_table.reshape(-1), far_bias, q_rows, scores, scores_new, thr, *([pool_k] * pps), *([pool_v] * pps),
            k_new, v_new, tp, tn)


def _rot_tables(pos, half, group, extra=None):
    lane = jnp.arange(LANES)
    g = lane % group
    rotary = g < 2 * half
    freq = ROPE_THETA ** (-jnp.arange(half, dtype=F32) / half)
    ang = pos.astype(F32)[:, None] * freq[None, :]
    cos = jnp.cos(ang)[:, g % half]
    sin = jnp.sin(ang)[:, g % half]
    c_tab = jnp.where(rotary[None, :], cos, 1.0)
    s_tab = jnp.where(rotary[None, :], jnp.where((g < half)[None, :], -sin, sin), 0.0)
    if extra is not None:
        a, b, val = extra
        c_tab = jnp.where(((lane >= a) & (lane < b))[None, :], val, c_tab)
    return c_tab.astype(F32), s_tab.astype(F32)


def kernel(x_prompt, x_sample, cache_mla_ckv, cache_mla_kpe, cache_dsa_k, cache_dsa_v, cache_dsa_idx_k,
           page_table, c_prompt, c_sample, w_ada, b_ada, ln_g, ln_b,
           mla_w_in, mla_q_norm, mla_kv_norm, mla_w_uq, mla_w_uk, mla_w_uv, mla_w_o,
           dsa_w_in, dsa_w_o, rel_bias, ffn_w_gu, ffn_w_down):
    B, T, D = x_prompt.shape
    Bs, ts, _ = x_sample.shape
    depth = w_ada.shape[0]
    alpha = (2 * depth) ** 0.25
    n_pages = page_table.shape[1]
    page = cache_mla_ckv.shape[2]
    past = n_pages * page
    Mp, Ms = B * T, Bs * ts
    ts8 = 8 * (-(-ts // 8))

    ql = mla_q_norm.shape[1]
    kvl = cache_mla_ckv.shape[-1]
    rope = cache_mla_kpe.shape[-1]
    H = mla_w_uk.shape[2]
    nope = mla_w_uk.shape[3]
    vh = mla_w_uv.shape[3]
    mla_scale = (nope + rope) ** -0.5
    KVH, HD = cache_dsa_k.shape[-2:]
    ID = cache_dsa_idx_k.shape[-1]
    NH = dsa_w_o.shape[1] // HD
    G = NH // KVH
    dq, dkv = NH * HD, KVH * HD
    IH = (dsa_w_in.shape[2] - dq - 2 * dkv - ID) // (ID + 1)
    dsa_scale = HD ** -0.5
    idx_wscale = (IH ** -0.5) * (ID ** -0.5)
    tq_dsa = min(256, T)

    pos_p = jnp.arange(T, dtype=jnp.int32)
    pos_s = jnp.tile(past + jnp.arange(ts, dtype=jnp.int32), Bs)
    tabs = {
        "mla": (_rot_tables(pos_p, rope // 2, rope), _rot_tables(pos_s, rope // 2, rope)),
        "idx": (_rot_tables(pos_p, IDX_ROPE // 2, ID), _rot_tables(pos_s, IDX_ROPE // 2, ID)),
        "tail": (_rot_tables(pos_p, IDX_ROPE // 2, LANES, (ID, ID + IH, idx_wscale)),
                 _rot_tables(pos_s, IDX_ROPE // 2, LANES, (ID, ID + IH, idx_wscale))),
    }

    c_all = jnp.concatenate([c_prompt, c_sample], axis=0)
    nc_rows = c_all.shape[0]
    c_all = jnp.pad(c_all, ((0, -nc_rows % 8), (0, 0)))
    mods = [_mm(c_all, w_ada, layer=i, bias=b_ada[i].reshape(1, -1), act="silu", tm=c_all.shape[0], tn=1024,
                name="ada_mod") for i in range(depth)]

    pool_kpe_t = jnp.swapaxes(cache_mla_kpe, 2, 3)
    pool_ik_t = jnp.swapaxes(cache_dsa_idx_k, 2, 3)
    pool_k = cache_dsa_k.reshape(cache_dsa_k.shape[0], -1, page * cache_dsa_k.shape[3], cache_dsa_k.shape[4])
    pool_v = cache_dsa_v.reshape(pool_k.shape)

    def mod_vectors(i):
        m = mods[i]
        mp = [m[:B, j * D:(j + 1) * D].reshape(B, 1, D) for j in range(6)]
        ms = [jnp.repeat(m[B:B + Bs, j * D:(j + 1) * D], ts, axis=0) for j in range(6)]
        return mp, ms

    groups = [dict(y=x_prompt.reshape(Mp, D), rpg=T, nb=B, tb=T, g=0),
              dict(y=x_sample.reshape(Ms, D), rpg=None, nb=1, tb=Ms, g=1)]
    modv = [mod_vectors(i) for i in range(depth)]
    for grp in groups:
        sh1, sc1 = modv[0][grp["g"]][0], modv[0][grp["g"]][1]
        grp["u"] = _modulate(grp["y"], sh1, sc1, grp["rpg"])

    t0 = t1 = tp = tn = far_bias = None
    if depth > 1:
        t0, t1, tp, tn = _bias_tables(rel_bias, tq_dsa, ts8)
        far_bias = rel_bias[-1]

    outs = {k: [[], []] for k in ("ckv", "kpe", "k", "v", "ik")}
    for i in range(depth):
        j = i // 2
        if i % 2 == 0:
            w_in_pad = jnp.pad(mla_w_in[j], ((0, 0), (0, LANES - rope))).astype(BF16)
            w_uq = mla_w_uq[j].reshape(ql, H, nope + rope)
            w_nope = w_uq[:, :, :nope].reshape(ql, H * nope).astype(BF16)
            w_pe = w_uq[:, :, nope:].reshape(ql, H * rope).astype(BF16)
            w_uk_t = jnp.transpose(mla_w_uk[j], (1, 2, 0)).astype(BF16)
            w_uv = mla_w_uv[j].reshape(kvl, H * vh).astype(BF16)
            w_o = mla_w_o[j].astype(BF16)
        else:
            w_in = dsa_w_in[j].astype(BF16)
            w_tail = jnp.pad(dsa_w_in[j][:, dq + 2 * dkv + IH * ID:], ((0, 0), (0, LANES - ID - IH))).astype(BF16)
            w_o = dsa_w_o[j].astype(BF16)
        w_down = ffn_w_down[i].astype(BF16)

        for grp in groups:
            g = grp["g"]
            u = grp["u"]
            mv = modv[i][g]
            if i % 2 == 0:
                ct, st = tabs["mla"][g]
                cq, ckv, ckv_b, kpe, kpe_b = _mla_in(u, w_in_pad, mla_q_norm[j], mla_kv_norm[j], ct, st,
                                                     ql, kvl, rope)
                q_lat, q_pe = _mla_q(cq, w_nope, w_pe, w_uk_t, ct, st, grp["nb"], grp["tb"], H, nope, rope, kvl)
                if g == 0:
                    o = _mla_prompt_attn(q_lat, q_pe, ckv_b.reshape(B, T, kvl), kpe_b.reshape(B, T, rope),
                                         w_uv, mla_scale)
                else:
                    def rows(a):
                        n = a.shape[-1]
                        return a.reshape(H, Bs, ts, n).transpose(1, 0, 2, 3).reshape(Bs, H * ts, n)
                    o = _mla_sample_attn(rows(q_lat), rows(q_pe), cache_mla_ckv, pool_kpe_t, j, page_table,
                                         ckv.reshape(Bs, ts, kvl), kpe.reshape(Bs, ts, rope), w_uv, mla_scale, H)
                    o = o.reshape(Ms, H * vh).astype(BF16)
                outs["ckv"][g].append(ckv)
                outs["kpe"][g].append(kpe)
            else:
                q = _mm(u, w_in, n_cols=dq, col0=0, out_dtype=BF16, name="dsa_q")
                k, k_b = _mm(u, w_in, n_cols=dkv, col0=dq, second_dtype=BF16, name="dsa_k")
                v, v_b = _mm(u, w_in, n_cols=dkv, col0=dq + dkv, second_dtype=BF16, name="dsa_v")
                ci, si = tabs["idx"][g]
                qi = _mm(u, w_in, n_cols=IH * ID, col0=dq + 2 * dkv, out_dtype=BF16,
                         rot=(ci, si, IDX_ROPE // 2), name="dsa_qi")
                ctl, stl = tabs["tail"][g]
                tail = _mm(u, w_tail, rot=(ctl, stl, IDX_ROPE // 2), name="dsa_tail")
                ki = tail[:, :ID]
                wi = tail[:, ID:ID + IH]
                if g == 0:
                    ki_t = jnp.swapaxes(ki.astype(BF16).reshape(B, T // tq_dsa, tq_dsa, ID), 2, 3)
                    o = _dsa_prompt(q, qi, wi, k_b, v_b, ki_t, t0, t1, far_bias, B, T,
                                    min(TOPK_MAX, T // 4), IH, ID, KVH, G, HD, dsa_scale, tq_dsa)
                else:
                    pad_t = ((0, 0), (0, 0), (0, ts8 - ts), (0, 0))
                    qi_rows = jnp.pad(qi.reshape(Bs, ts, IH, ID).transpose(0, 2, 1, 3), pad_t
                                      ).reshape(Bs, IH * ts8, ID)
                    wi_rows = jnp.pad(wi.reshape(Bs, ts, IH).transpose(0, 2, 1), ((0, 0), (0, 0), (0, ts8 - ts))
                                      ).reshape(Bs, IH * ts8, 1)
                    sc, sc_new = _dsa_sample_index(qi_rows, wi_rows, pool_ik_t, j, page_table,
                                                   ki.reshape(Bs, ts, ID), IH, ts, ts8)
                    thr = _topk_threshold(sc, sc_new, min(TOPK_MAX, (past + ts) // 4))
                    q_rows = jnp.pad(q.reshape(Bs, ts, NH, HD).transpose(0, 2, 1, 3), pad_t
                                     ).reshape(Bs, KVH, G * ts8, HD)
                    o = _dsa_sample_attn(q_rows, sc, sc_new, thr, pool_k, pool_v, j,
                                         page_table, k.reshape(Bs, ts, dkv), v.reshape(Bs, ts, dkv), tp, tn, far_bias,
                                         KVH, G, HD, ts, ts8, dsa_scale)
                    o = o.reshape(Ms, NH * HD).astype(BF16)
                outs["k"][g].append(k)
                outs["v"][g].append(v)
                outs["ik"][g].append(ki)
            y1, u2 = _mm_postnorm(o, w_o, grp["y"], mv[2], ln_g[i, 0], ln_b[i, 0], (mv[3], mv[4]), grp["rpg"],
                                  alpha, name="attn_out_postnorm")
            hmid = _swiglu_up(u2, ffn_w_gu, i)
            nxt = None
            if i + 1 < depth:
                nmv = modv[i + 1][g]
                nxt = (nmv[0], nmv[1])
            grp["y"], grp["u"] = _mm_postnorm(hmid, w_down, y1, mv[5], ln_g[i, 1], ln_b[i, 1], nxt, grp["rpg"],
                                              alpha, name="ffn_down_postnorm")

    def stack(key, g, shape):
        return jnp.stack([a.reshape(shape) for a in outs[key][g]])

    yp = groups[0]["y"].reshape(B, T, D)
    ys = groups[1]["y"].reshape(Bs, ts, D)
    return (yp, ys,
            stack("ckv", 0, (B, T, kvl)), stack("kpe", 0, (B, T, rope)),
            stack("k", 0, (B, T, KVH, HD)), stack("v", 0, (B, T, KVH, HD)), stack("ik", 0, (B, T, ID)),
            stack("ckv", 1, (Bs, ts, kvl)), stack("kpe", 1, (Bs, ts, rope)),
            stack("k", 1, (Bs, ts, KVH, HD)), stack("v", 1, (Bs, ts, KVH, HD)), stack("ik", 1, (Bs, ts, ID)))
```

```python
import functools
import math

import jax
import jax.numpy as jnp
from jax import lax
from jax.experimental import pallas as pl
from jax.experimental.pallas import tpu as pltpu

F32 = jnp.float32
BF16 = jnp.bfloat16
NEG_INF = float("-inf")

LANES = 128
ROPE_THETA = 10000.0
IDX_ROPE = 32
TOPK_MAX = 256
REL_MAX_DIST = 128
LN_EPS = 1e-5
RMS_EPS = 1e-6
BISECT_ITERS = 32
VMEM_LIMIT_MB = 56


def _cparams(sem, vmem_mb=VMEM_LIMIT_MB):
    return pltpu.CompilerParams(dimension_semantics=sem, vmem_limit_bytes=vmem_mb * 1024 * 1024)


def _dot(a, b):
    return jnp.dot(a, b, preferred_element_type=F32)


def _dot_nt(a, b):
    return lax.dot_general(a, b, (((1,), (1,)), ((), ())), preferred_element_type=F32)


def _rot(v, c, s, half):
    lane = lax.broadcasted_iota(jnp.int32, v.shape, 1)
    partner = jnp.where((lane % (2 * half)) < half,
                        pltpu.roll(v, LANES - half, 1), pltpu.roll(v, half, 1))
    return v * c + partner * s


def _mm_body(*refs, nk, act, rot_half, has_bias, second):
    it = iter(refs)
    x_ref = next(it)
    w_ref = next(it)
    b_ref = next(it) if has_bias else None
    c_ref = next(it) if rot_half else None
    s_ref = next(it) if rot_half else None
    o_ref = next(it)
    o2_ref = next(it) if second else None
    acc_ref = next(it) if nk > 1 else None

    x = x_ref[...]
    if act == "silu":
        xf = x.astype(F32)
        x = xf * jax.nn.sigmoid(xf)
    part = _dot(x.astype(BF16), w_ref[...].astype(BF16))

    def finish(acc):
        if has_bias:
            acc = acc + b_ref[...]
        if rot_half:
            c = c_ref[...]
            s = s_ref[...]
            for j in range(acc.shape[1] // LANES):
                sl = slice(j * LANES, (j + 1) * LANES)
                r = _rot(acc[:, sl], c, s, rot_half)
                o_ref[:, sl] = r.astype(o_ref.dtype)
                if second:
                    o2_ref[:, sl] = r.astype(o2_ref.dtype)
        else:
            o_ref[...] = acc.astype(o_ref.dtype)
            if second:
                o2_ref[...] = acc.astype(o2_ref.dtype)

    if nk == 1:
        finish(part)
    else:
        k = pl.program_id(2)

        @pl.when(k == 0)
        def _():
            acc_ref[...] = part

        @pl.when(k > 0)
        def _():
            acc_ref[...] += part

        @pl.when(k == nk - 1)
        def _():
            finish(acc_ref[...])


def _mm(x, w, *, layer=None, n_cols=None, col0=0, tm=512, tn=512, tk=None, out_dtype=F32, second_dtype=None,
        act=None, bias=None, rot=None, name="mm"):
    M, K = x.shape
    N = n_cols if n_cols is not None else w.shape[-1]
    tm = min(tm, M)
    tn = min(tn, N)
    while N % tn or col0 % tn:
        tn //= 2
    tk = K if tk is None else min(tk, K)
    assert M % tm == 0 and K % tk == 0 and tn % LANES == 0
    nk = K // tk
    jb = col0 // tn
    if layer is None:
        w_spec = pl.BlockSpec((tk, tn), lambda i, j, k: (k, j + jb))
    else:
        w_spec = pl.BlockSpec((None, tk, tn), lambda i, j, k: (layer, k, j + jb))
    in_specs = [pl.BlockSpec((tm, tk), lambda i, j, k: (i, k)), w_spec]
    args = [x, w]
    if bias is not None:
        in_specs.append(pl.BlockSpec((1, tn), lambda i, j, k: (0, j)))
        args.append(bias)
    rot_half = 0
    if rot is not None:
        c_tab, s_tab, rot_half = rot
        nr = c_tab.shape[0] // tm
        assert c_tab.shape[0] % tm == 0
        for t in (c_tab, s_tab):
            in_specs.append(pl.BlockSpec((tm, LANES), lambda i, j, k: (i % nr, 0)))
            args.append(t)
    out_shape = [jax.ShapeDtypeStruct((M, N), out_dtype)]
    out_specs = [pl.BlockSpec((tm, tn), lambda i, j, k: (i, j))]
    if second_dtype is not None:
        out_shape.append(jax.ShapeDtypeStruct((M, N), second_dtype))
        out_specs.append(pl.BlockSpec((tm, tn), lambda i, j, k: (i, j)))
    scratch = [pltpu.VMEM((tm, tn), F32)] if nk > 1 else []
    res = pl.pallas_call(
        functools.partial(_mm_body, nk=nk, act=act, rot_half=rot_half, has_bias=bias is not None,
                          second=second_dtype is not None),
        grid=(M // tm, N // tn, nk),
        in_specs=in_specs, out_specs=out_specs, out_shape=out_shape, scratch_shapes=scratch,
        compiler_params=_cparams(("parallel", "parallel", "arbitrary")), name=name)(*args)
    return res if second_dtype is not None else res[0]


def _modulate_body(y_ref, sh_ref, sc_ref, u_ref):
    u_ref[...] = (y_ref[...] * (1.0 + sc_ref[...]) + sh_ref[...]).astype(u_ref.dtype)


def _mod_spec(tm, n, rows_per_group):
    if rows_per_group is None:
        return pl.BlockSpec((tm, n), lambda i, *_: (i, 0))
    per = rows_per_group // tm
    return pl.BlockSpec((None, 1, n), lambda i, *_: (i // per, 0, 0))


def _modulate(y, sh, sc, rows_per_group, tm=512):
    M, D = y.shape
    tm = min(tm, M)
    return pl.pallas_call(
        _modulate_body, grid=(M // tm,),
        in_specs=[pl.BlockSpec((tm, D), lambda i: (i, 0)), _mod_spec(tm, D, rows_per_group),
                  _mod_spec(tm, D, rows_per_group)],
        out_specs=pl.BlockSpec((tm, D), lambda i: (i, 0)),
        out_shape=jax.ShapeDtypeStruct((M, D), BF16),
        compiler_params=_cparams(("parallel",)), name="modulate")(y, sh, sc)


def _mm_postnorm_body(*refs, nk, alpha, with_mod):
    if with_mod:
        x_ref, w_ref, y_ref, g_ref, lg_ref, lb_ref, sh_ref, sc_ref, yo_ref, u_ref, acc_ref = refs
    else:
        x_ref, w_ref, y_ref, g_ref, lg_ref, lb_ref, yo_ref, acc_ref = refs
    k = pl.program_id(1)
    part = _dot(x_ref[...].astype(BF16), w_ref[...].astype(BF16))

    @pl.when(k == 0)
    def _():
        acc_ref[...] = part

    @pl.when(k > 0)
    def _():
        acc_ref[...] += part

    @pl.when(k == nk - 1)
    def _():
        z = alpha * y_ref[...] + g_ref[...] * acc_ref[...]
        mu = jnp.mean(z, axis=-1, keepdims=True)
        zc = z - mu
        var = jnp.mean(zc * zc, axis=-1, keepdims=True)
        yn = zc * lax.rsqrt(var + LN_EPS) * lg_ref[...] + lb_ref[...]
        yo_ref[...] = yn
        if with_mod:
            u_ref[...] = (yn * (1.0 + sc_ref[...]) + sh_ref[...]).astype(u_ref.dtype)


def _mm_postnorm(x, w, y, gate, ln_g, ln_b, mod, rows_per_group, alpha, tm=512, tk=512, name="mm_postnorm"):
    M, K = x.shape
    N = w.shape[1]
    tm = min(tm, M)
    tk = min(tk, K)
    while K % tk:
        tk //= 2
    assert M % tm == 0 and tk % LANES == 0
    nk = K // tk
    row = pl.BlockSpec((tm, N), lambda i, k: (i, 0))
    vec = pl.BlockSpec((1, N), lambda i, k: (0, 0))
    in_specs = [pl.BlockSpec((tm, tk), lambda i, k: (i, k)), pl.BlockSpec((tk, N), lambda i, k: (k, 0)),
                row, _mod_spec(tm, N, rows_per_group), vec, vec]
    args = [x, w, y, gate, ln_g.reshape(1, N), ln_b.reshape(1, N)]
    out_shape = [jax.ShapeDtypeStruct((M, N), F32)]
    out_specs = [row]
    if mod is not None:
        in_specs += [_mod_spec(tm, N, rows_per_group)] * 2
        args += list(mod)
        out_shape.append(jax.ShapeDtypeStruct((M, N), BF16))
        out_specs.append(row)
    res = pl.pallas_call(
        functools.partial(_mm_postnorm_body, nk=nk, alpha=alpha, with_mod=mod is not None),
        grid=(M // tm, nk), in_specs=in_specs, out_specs=out_specs, out_shape=out_shape,
        scratch_shapes=[pltpu.VMEM((tm, N), F32)],
        compiler_params=_cparams(("parallel", "arbitrary")), name=name)(*args)
    return (res[0], res[1]) if mod is not None else (res[0], None)


def _swiglu_body(x_ref, wg_ref, wu_ref, o_ref):
    x = x_ref[...]
    g = _dot(x, wg_ref[...].astype(BF16))
    u = _dot(x, wu_ref[...].astype(BF16))
    o_ref[...] = (g * jax.nn.sigmoid(g) * u).astype(o_ref.dtype)


def _swiglu_up(x, w_gu, layer, tm=1024, tn=512):
    M, K = x.shape
    F = w_gu.shape[2] // 2
    tm = min(tm, M)
    tn = min(tn, F)
    while F % tn:
        tn //= 2
    nj = F // tn
    return pl.pallas_call(
        _swiglu_body, grid=(M // tm, nj),
        in_specs=[pl.BlockSpec((tm, K), lambda i, j: (i, 0)),
                  pl.BlockSpec((None, K, tn), lambda i, j: (layer, 0, j)),
                  pl.BlockSpec((None, K, tn), lambda i, j: (layer, 0, j + nj))],
        out_specs=pl.BlockSpec((tm, tn), lambda i, j: (i, j)),
        out_shape=jax.ShapeDtypeStruct((M, F), BF16),
        compiler_params=_cparams(("parallel", "parallel")), name="swiglu_up")(x, w_gu, w_gu)


def _mla_in_body(x_ref, w_ref, qg_ref, kvg_ref, c_ref, s_ref, cq_ref, ckv_ref, ckvb_ref, kpe_ref, kpeb_ref,
                 *, ql, kvl, rope):
    acc = _dot(x_ref[...], w_ref[...])

    def rms(v, g):
        return v * lax.rsqrt(jnp.mean(v * v, axis=-1, keepdims=True) + RMS_EPS) * g

    cq_ref[...] = rms(acc[:, :ql], qg_ref[...]).astype(cq_ref.dtype)
    ckv = rms(acc[:, ql:ql + kvl], kvg_ref[...])
    ckv_ref[...] = ckv
    ckvb_ref[...] = ckv.astype(ckvb_ref.dtype)
    kpe = _rot(acc[:, ql + kvl:], c_ref[...], s_ref[...], rope // 2)[:, :rope]
    kpe_ref[...] = kpe
    kpeb_ref[...] = kpe.astype(kpeb_ref.dtype)


def _mla_in(u, w_pad, q_g, kv_g, c_tab, s_tab, ql, kvl, rope, tm=512):
    M, K = u.shape
    N = w_pad.shape[1]
    tm = min(tm, M)
    nr = c_tab.shape[0] // tm
    tab = pl.BlockSpec((tm, LANES), lambda i: (i % nr, 0))

    def row(n):
        return pl.BlockSpec((tm, n), lambda i: (i, 0))

    return pl.pallas_call(
        functools.partial(_mla_in_body, ql=ql, kvl=kvl, rope=rope), grid=(M // tm,),
        in_specs=[row(K), pl.BlockSpec((K, N), lambda i: (0, 0)),
                  pl.BlockSpec((1, ql), lambda i: (0, 0)), pl.BlockSpec((1, kvl), lambda i: (0, 0)), tab, tab],
        out_specs=[row(ql), row(kvl), row(kvl), row(rope), row(rope)],
        out_shape=[jax.ShapeDtypeStruct((M, ql), BF16), jax.ShapeDtypeStruct((M, kvl), F32),
                   jax.ShapeDtypeStruct((M, kvl), BF16), jax.ShapeDtypeStruct((M, rope), F32),
                   jax.ShapeDtypeStruct((M, rope), BF16)],
        compiler_params=_cparams(("parallel",)), name="mla_in")(
            u, w_pad, q_g.reshape(1, ql), kv_g.reshape(1, kvl), c_tab, s_tab)


def _mla_q_body(cq_ref, wn_ref, wp_ref, wuk_ref, c_ref, s_ref, ql_ref, qp_ref, *, H, nope, rope):
    cq = cq_ref[...]
    qn = _dot(cq, wn_ref[...]).astype(BF16)
    for h in range(H):
        ql_ref[0, h] = _dot(qn[:, h * nope:(h + 1) * nope], wuk_ref[h]).astype(ql_ref.dtype)
    qp = _dot(cq, wp_ref[...])
    c = c_ref[...]
    s = s_ref[...]
    per = LANES // rope
    for j in range(H // per):
        r = _rot(qp[:, j * LANES:(j + 1) * LANES], c, s, rope // 2)
        for e in range(per):
            qp_ref[0, j * per + e] = r[:, e * rope:(e + 1) * rope].astype(qp_ref.dtype)


def _mla_q(cq, w_nope, w_pe, w_uk_t, c_tab, s_tab, nb, tb, H, nope, rope, kvl, tm=512):
    M, ql = cq.shape
    tm = min(tm, tb)
    per = tb // tm
    nr = c_tab.shape[0] // tm
    tab = pl.BlockSpec((tm, LANES), lambda i: (i % nr, 0))
    full2 = lambda a: pl.BlockSpec(a.shape, lambda i: (0, 0))
    return pl.pallas_call(
        functools.partial(_mla_q_body, H=H, nope=nope, rope=rope), grid=(M // tm,),
        in_specs=[pl.BlockSpec((tm, ql), lambda i: (i, 0)), full2(w_nope), full2(w_pe),
                  pl.BlockSpec(w_uk_t.shape, lambda i: (0, 0, 0)), tab, tab],
        out_specs=[pl.BlockSpec((1, H, tm, kvl), lambda i: (i // per, 0, i % per, 0)),
                   pl.BlockSpec((1, H, tm, rope), lambda i: (i // per, 0, i % per, 0))],
        out_shape=[jax.ShapeDtypeStruct((nb, H, tb, kvl), BF16), jax.ShapeDtypeStruct((nb, H, tb, rope), BF16)],
        compiler_params=_cparams(("parallel",)), name="mla_q")(cq, w_nope, w_pe, w_uk_t, c_tab, s_tab)


def _flash_update(s, v, m_ref, l_ref, acc_ref):
    m_old = m_ref[...]
    m_new = jnp.maximum(m_old, jnp.max(s, axis=-1, keepdims=True))
    m_safe = jnp.where(m_new == NEG_INF, 0.0, m_new)
    a = jnp.exp(m_old - m_safe)
    p = jnp.exp(s - m_safe)
    l_ref[...] = a * l_ref[...] + jnp.sum(p, axis=-1, keepdims=True)
    acc_ref[...] = a * acc_ref[...] + _dot(p.astype(BF16), v)
    m_ref[...] = m_new


def _lane_fold(x, op):
    r = x[:, :LANES]
    for j in range(1, x.shape[1] // LANES):
        r = op(r, x[:, j * LANES:(j + 1) * LANES])
    return r


def _mla_prompt_body(ql_ref, qp_ref, ckv_ref, kpe_ref, wuv_ref, o_ref, s_ref, mx_ref, lp_ref, acc_ref,
                     *, H, tq, tk, scale, vh):
    i = pl.program_id(1)
    R = H * tq
    n_blk = lax.div(i * tq + tq - 1, tk) + 1

    def score(c, masked):
        ql = ql_ref[0].reshape(R, ql_ref.shape[-1])
        qp = qp_ref[0].reshape(R, qp_ref.shape[-1])
        start = pl.multiple_of(c * tk, tk)
        s = (_dot_nt(ql, ckv_ref[0, pl.ds(start, tk), :]) + _dot_nt(qp, kpe_ref[0, pl.ds(start, tk), :])) * scale
        if masked:
            tok = i * tq + (lax.broadcasted_iota(jnp.int32, s.shape, 0) % tq)
            key = c * tk + lax.broadcasted_iota(jnp.int32, s.shape, 1)
            s = jnp.where(key <= tok, s, NEG_INF)
        s_ref[c] = s
        return _lane_fold(s, jnp.maximum)

    mx_ref[...] = jnp.full(mx_ref.shape, NEG_INF, F32)

    def far(c, _):
        mx_ref[...] = jnp.maximum(mx_ref[...], score(c, False))
        return 0

    lax.fori_loop(0, n_blk - 1, far, 0)
    mx = jnp.maximum(mx_ref[...], score(n_blk - 1, True))
    m = jnp.max(mx, axis=-1, keepdims=True)

    lp_ref[...] = jnp.zeros(lp_ref.shape, F32)
    acc_ref[...] = jnp.zeros(acc_ref.shape, F32)

    def pv(c, _):
        p = jnp.exp(s_ref[c] - m)
        lp_ref[...] += _lane_fold(p, jnp.add)
        acc_ref[...] += _dot(p.astype(BF16), ckv_ref[0, pl.ds(pl.multiple_of(c * tk, tk), tk), :])
        return 0

    lax.fori_loop(0, n_blk, pv, 0)
    o = (acc_ref[...] / jnp.sum(lp_ref[...], axis=-1, keepdims=True)).astype(BF16)
    for h in range(H):
        o_ref[:, h * vh:(h + 1) * vh] = _dot(o[h * tq:(h + 1) * tq], wuv_ref[:, h * vh:(h + 1) * vh]
                                             ).astype(o_ref.dtype)


def _mla_prompt_attn(q_lat, q_pe, ckv_b, kpe_b, w_uv, scale, tq=128, tk=256):
    B, H, T, C = q_lat.shape
    R = q_pe.shape[-1]
    tq = min(tq, T)
    tk = min(tk, T)
    nq, nk = T // tq, T // tk
    vh = w_uv.shape[1] // H
    return pl.pallas_call(
        functools.partial(_mla_prompt_body, H=H, tq=tq, tk=tk, scale=scale, vh=vh),
        grid=(B, nq),
        in_specs=[pl.BlockSpec((1, H, tq, C), lambda b, i: (b, 0, i, 0)),
                  pl.BlockSpec((1, H, tq, R), lambda b, i: (b, 0, i, 0)),
                  pl.BlockSpec((1, T, C), lambda b, i: (b, 0, 0)), pl.BlockSpec((1, T, R), lambda b, i: (b, 0, 0)),
                  pl.BlockSpec(w_uv.shape, lambda b, i: (0, 0))],
        out_specs=pl.BlockSpec((tq, H * vh), lambda b, i: (b * nq + i, 0)),
        out_shape=jax.ShapeDtypeStruct((B * T, H * vh), BF16),
        scratch_shapes=[pltpu.VMEM((nk, H * tq, tk), F32), pltpu.VMEM((H * tq, LANES), F32),
                        pltpu.VMEM((H * tq, LANES), F32), pltpu.VMEM((H * tq, C), F32)],
        compiler_params=_cparams(("parallel", "arbitrary")), name="mla_prompt_attn")(
            q_lat, q_pe, ckv_b, kpe_b, w_uv)


def _new_key_scores(q_parts, k_parts, n_new):
    cols = []
    for t in range(n_new):
        acc = None
        for q, k in zip(q_parts, k_parts):
            d = jnp.sum(q * k[t:t + 1, :], axis=-1, keepdims=True)
            acc = d if acc is None else acc + d
        cols.append(acc)
    return cols


def _cols_to_block(cols, rows):
    lane = lax.broadcasted_iota(jnp.int32, (rows, LANES), 1)
    blk = jnp.full((rows, LANES), NEG_INF, F32)
    for t, c in enumerate(cols):
        blk = jnp.where(lane == t, c, blk)
    return blk


def _flash_update_new(s_blk, v_new, n_new, m_ref, l_ref, acc_ref):
    m_old = m_ref[...]
    m_new = jnp.maximum(m_old, jnp.max(s_blk, axis=-1, keepdims=True))
    m_safe = jnp.where(m_new == NEG_INF, 0.0, m_new)
    a = jnp.exp(m_old - m_safe)
    p = jnp.exp(s_blk - m_safe)
    l_ref[...] = a * l_ref[...] + jnp.sum(p, axis=-1, keepdims=True)
    acc = a * acc_ref[...]
    for t in range(n_new):
        acc = acc + p[:, t:t + 1] * v_new[t:t + 1, :]
    acc_ref[...] = acc
    m_ref[...] = m_new


def _mla_sample_body(*refs, pps, H, ts, page, nc, scale, vh):
    pt_ref = refs[0]
    ql_ref, qp_ref = refs[1], refs[2]
    ckv_refs = refs[3:3 + pps]
    kpe_refs = refs[3 + pps:3 + 2 * pps]
    cn_ref, kn_ref, wuv_ref, o_ref, kc_ref, kp_ref, m_ref, l_ref, acc_ref = refs[3 + 2 * pps:]
    del pt_ref
    c = pl.program_id(1)
    R = H * ts

    @pl.when(c == 0)
    def _():
        m_ref[...] = jnp.full(m_ref.shape, NEG_INF, F32)
        l_ref[...] = jnp.zeros(l_ref.shape, F32)
        acc_ref[...] = jnp.zeros(acc_ref.shape, F32)

    for r in range(pps):
        kc_ref[r * page:(r + 1) * page, :] = ckv_refs[r][...].astype(BF16)
        kp_ref[:, r * page:(r + 1) * page] = kpe_refs[r][...].astype(BF16)
    ql = ql_ref[0]
    qp = qp_ref[0]
    ckv = kc_ref[...]
    s = (_dot_nt(ql, ckv) + _dot(qp, kp_ref[...])) * scale
    _flash_update(s, ckv, m_ref, l_ref, acc_ref)

    @pl.when(c == nc - 1)
    def _():
        cn = cn_ref[0]
        cols = _new_key_scores([ql.astype(F32), qp.astype(F32)], [cn, kn_ref[0]], ts)
        blk = _cols_to_block(cols, R) * scale
        qt = lax.broadcasted_iota(jnp.int32, (R, LANES), 0) % ts
        lane = lax.broadcasted_iota(jnp.int32, (R, LANES), 1)
        blk = jnp.where(lane <= qt, blk, NEG_INF)
        _flash_update_new(blk, cn, ts, m_ref, l_ref, acc_ref)
        o = (acc_ref[...] / l_ref[...]).astype(BF16)
        full = _dot(o, wuv_ref[...])
        o_ref[0] = jnp.concatenate([full[h * ts:(h + 1) * ts, h * vh:(h + 1) * vh] for h in range(H)], axis=1)


def _mla_sample_attn(q_lat, q_pe, pool_ckv, pool_kpe_t, layer, page_table, ckv_new, kpe_new, w_uv, scale, H,
                     pps=32):
    Bs, R, C = q_lat.shape
    Rr = q_pe.shape[-1]
    ts = R // H
    n_pages = page_table.shape[1]
    page = pool_ckv.shape[2]
    pps = min(pps, n_pages)
    nc = n_pages // pps
    vh = w_uv.shape[1] // H

    def pmap(r):
        return lambda b, c, pt: (layer, pt[b * n_pages + c * pps + r], 0, 0)

    in_specs = [pl.BlockSpec((1, R, C), lambda b, c, pt: (b, 0, 0)),
                pl.BlockSpec((1, R, Rr), lambda b, c, pt: (b, 0, 0))]
    in_specs += [pl.BlockSpec((None, None, page, C), pmap(r)) for r in range(pps)]
    in_specs += [pl.BlockSpec((None, None, Rr, page), pmap(r)) for r in range(pps)]
    in_specs += [pl.BlockSpec((1, ts, C), lambda b, c, pt: (b, 0, 0)),
                 pl.BlockSpec((1, ts, Rr), lambda b, c, pt: (b, 0, 0)),
                 pl.BlockSpec(w_uv.shape, lambda b, c, pt: (0, 0))]
    grid_spec = pltpu.PrefetchScalarGridSpec(
        num_scalar_prefetch=1, grid=(Bs, nc), in_specs=in_specs,
        out_specs=pl.BlockSpec((1, ts, H * vh), lambda b, c, pt: (b, 0, 0)),
        scratch_shapes=[pltpu.VMEM((pps * page, C), BF16), pltpu.VMEM((Rr, pps * page), BF16),
                        pltpu.VMEM((R, 1), F32), pltpu.VMEM((R, 1), F32), pltpu.VMEM((R, C), F32)])
    return pl.pallas_call(
        functools.partial(_mla_sample_body, pps=pps, H=H, ts=ts, page=page, nc=nc, scale=scale, vh=vh),
        grid_spec=grid_spec, out_shape=jax.ShapeDtypeStruct((Bs, ts, H * vh), F32),
        compiler_params=_cparams(("parallel", "arbitrary")), name="mla_sample_attn")(
            page_table.reshape(-1), q_lat, q_pe, *([pool_ckv] * pps), *([pool_kpe_t] * pps),
            ckv_new, kpe_new, w_uv)


def _t5_bias_of_dist(d, rb_ref, h, n_buckets):
    dist = jnp.maximum(d, 0)
    exact = n_buckets // 2
    df = jnp.maximum(dist, 1).astype(F32)
    large = exact + (jnp.log(df / exact) / math.log(REL_MAX_DIST / exact) * (n_buckets - exact)).astype(jnp.int32)
    large = jnp.minimum(large, n_buckets - 1)
    bucket = jnp.where(dist < exact, dist, large)
    out = jnp.zeros(d.shape, F32)
    for b in range(n_buckets):
        out = jnp.where(bucket == b, rb_ref[b, h], out)
    return out


def _bias_tables_body(rb_ref, t0_ref, t1_ref, tp_ref, tn_ref, *, NH, tq, ts8, n_buckets):
    r = lax.broadcasted_iota(jnp.int32, (tq, tq), 0)
    c = lax.broadcasted_iota(jnp.int32, (tq, tq), 1)
    r8 = lax.broadcasted_iota(jnp.int32, (ts8, LANES), 0)
    c8 = lax.broadcasted_iota(jnp.int32, (ts8, LANES), 1)
    for h in range(NH):
        t0_ref[h] = _t5_bias_of_dist(r - c, rb_ref, h, n_buckets)
        t1_ref[h] = _t5_bias_of_dist(r - c + tq, rb_ref, h, n_buckets)
        tp_ref[h] = _t5_bias_of_dist(r8 - c8 + LANES, rb_ref, h, n_buckets)
        tn_ref[h] = _t5_bias_of_dist(r8 - c8, rb_ref, h, n_buckets)


def _bias_tables(rel_bias, tq, ts8):
    nb, NH = rel_bias.shape
    return pl.pallas_call(
        functools.partial(_bias_tables_body, NH=NH, tq=tq, ts8=ts8, n_buckets=nb),
        in_specs=[pl.BlockSpec(memory_space=pltpu.SMEM)],
        out_shape=[jax.ShapeDtypeStruct((NH, tq, tq), F32), jax.ShapeDtypeStruct((NH, tq, tq), F32),
                   jax.ShapeDtypeStruct((NH, ts8, LANES), F32), jax.ShapeDtypeStruct((NH, ts8, LANES), F32)],
        compiler_params=pltpu.CompilerParams(vmem_limit_bytes=VMEM_LIMIT_MB * 1024 * 1024),
        name="t5_bias_tables")(rel_bias)


def _bisect(lo0, hi0, count_ge, k_sel):
    def body(_, lh):
        lo, hi = lh
        mid = 0.5 * (lo + hi)
        ge = count_ge(mid) >= k_sel
        return jnp.where(ge, mid, lo), jnp.where(ge, hi, mid)

    lo, _ = lax.fori_loop(0, BISECT_ITERS, body, (lo0, hi0))
    return lo


def _dsa_prompt_body(far_ref, q_ref, qi_ref, wi_ref, k_ref, v_ref, kit_ref, t0_ref, t1_ref, o_ref,
                     sc_ref, wb_ref, s_ref, lp_ref, acc_ref, *, tq, nq, k_sel, IH, ID, KVH, G, HD, scale):
    i = pl.program_id(1)
    row_tok = i * tq + lax.broadcasted_iota(jnp.int32, (tq, tq), 0)
    col_in = lax.broadcasted_iota(jnp.int32, (tq, tq), 1)
    qi = qi_ref[...]
    wi = wi_ref[...]
    for h in range(IH):
        wb_ref[h] = jnp.broadcast_to(wi[:, h:h + 1], (tq, tq))

    def idx_block(c, carry):
        lo, hi = carry
        kit = kit_ref[0, c]
        tot = jnp.zeros((tq, tq), F32)
        for h in range(IH):
            d = _dot(qi[:, h * ID:(h + 1) * ID], kit)
            tot = tot + jnp.maximum(d, 0.0) * wb_ref[h]
        ok = (c * tq + col_in) <= row_tok
        sc_ref[c] = jnp.where(ok, tot, NEG_INF)
        lo = jnp.minimum(lo, jnp.min(jnp.where(ok, tot, jnp.inf), axis=-1, keepdims=True))
        hi = jnp.maximum(hi, jnp.max(jnp.where(ok, tot, NEG_INF), axis=-1, keepdims=True))
        return lo, hi

    lo0, hi0 = lax.fori_loop(0, i + 1, idx_block,
                             (jnp.full((tq, 1), jnp.inf, F32), jnp.full((tq, 1), NEG_INF, F32)))

    def count_ge(mid):
        def blk(c, acc):
            t = sc_ref[c]
            part = jnp.zeros((tq, LANES), F32)
            for j in range(tq // LANES):
                part = part + jnp.where(t[:, j * LANES:(j + 1) * LANES] >= mid, 1.0, 0.0)
            return acc + part
        acc = lax.fori_loop(0, i + 1, blk, jnp.zeros((tq, LANES), F32))
        return jnp.sum(acc, axis=-1, keepdims=True)

    thr = _bisect(lo0, hi0, count_ge, float(k_sel))

    def to_mask(c, _):
        sc_ref[c] = jnp.where(sc_ref[c] >= thr, 0.0, NEG_INF)
        return 0

    lax.fori_loop(0, i + 1, to_mask, 0)

    q = q_ref[...]
    R = G * tq
    nlv = tq // LANES

    def lane_fold(x, op):
        r = x[:, :LANES]
        for jj in range(1, nlv):
            r = op(r, x[:, jj * LANES:(jj + 1) * LANES])
        return r

    for n in range(KVH):
        qn = jnp.concatenate([q[:, (n * G + g) * HD:(n * G + g + 1) * HD] for g in range(G)], axis=0)

        def score(c, slot, bias):
            kc = k_ref[0, pl.ds(pl.multiple_of(c * tq, tq), tq), n * HD:(n + 1) * HD]
            s = _dot_nt(qn, kc) * scale + bias + jnp.concatenate([sc_ref[c]] * G, axis=0)
            s_ref[slot] = s
            return lane_fold(s, jnp.maximum)

        far = jnp.concatenate([jnp.full((tq, 1), far_ref[n * G + g], F32) for g in range(G)], axis=0)
        mx = lax.fori_loop(0, jnp.maximum(i - 1, 0), lambda c, mx: jnp.maximum(mx, score(c, c, far)),
                           jnp.full((R, LANES), NEG_INF, F32))
        prev = score(jnp.maximum(i - 1, 0), jnp.where(i >= 1, i - 1, nq), t1_ref[n * G:(n + 1) * G].reshape(R, tq))
        mx = jnp.where(i >= 1, jnp.maximum(mx, prev), mx)
        mx = jnp.maximum(mx, score(i, i, t0_ref[n * G:(n + 1) * G].reshape(R, tq)))
        m = jnp.max(mx, axis=-1, keepdims=True)

        lp_ref[...] = jnp.zeros(lp_ref.shape, F32)
        acc_ref[...] = jnp.zeros(acc_ref.shape, F32)

        def pv(c, _):
            p = jnp.exp(s_ref[c] - m)
            lp_ref[...] += lane_fold(p, jnp.add)
            vc = v_ref[0, pl.ds(pl.multiple_of(c * tq, tq), tq), n * HD:(n + 1) * HD]
            acc_ref[...] += _dot(p.astype(BF16), vc)
            return 0

        lax.fori_loop(0, i + 1, pv, 0)
        o = acc_ref[...] / jnp.sum(lp_ref[...], axis=-1, keepdims=True)
        for g in range(G):
            o_ref[:, (n * G + g) * HD:(n * G + g + 1) * HD] = o[g * tq:(g + 1) * tq].astype(o_ref.dtype)


def _dsa_prompt(q, qi, wi, k_b, v_b, ki_t, t0, t1, far_bias, B, T, k_sel, IH, ID, KVH, G, HD, scale, tq):
    M = B * T
    nq = T // tq
    NH = KVH * G
    row = lambda n: pl.BlockSpec((tq, n), lambda b, i, f: (b * nq + i, 0))
    seq = lambda n: pl.BlockSpec((1, T, n), lambda b, i, f: (b, 0, 0))
    tab = pl.BlockSpec((NH, tq, tq), lambda b, i, f: (0, 0, 0))
    grid_spec = pltpu.PrefetchScalarGridSpec(
        num_scalar_prefetch=1, grid=(B, nq),
        in_specs=[row(NH * HD), row(IH * ID), row(IH), seq(KVH * HD), seq(KVH * HD),
                  pl.BlockSpec((1, nq, ID, tq), lambda b, i, f: (b, 0, 0, 0)), tab, tab],
        out_specs=row(NH * HD),
        scratch_shapes=[pltpu.VMEM((nq, tq, tq), F32), pltpu.VMEM((IH, tq, tq), F32),
                        pltpu.VMEM((nq + 1, G * tq, tq), F32), pltpu.VMEM((G * tq, LANES), F32),
                        pltpu.VMEM((G * tq, HD), F32)])
    return pl.pallas_call(
        functools.partial(_dsa_prompt_body, tq=tq, nq=nq, k_sel=k_sel, IH=IH, ID=ID, KVH=KVH, G=G, HD=HD,
                          scale=scale),
        grid_spec=grid_spec, out_shape=jax.ShapeDtypeStruct((M, NH * HD), BF16),
        compiler_params=_cparams(("parallel", "arbitrary")), name="dsa_prompt")(
            far_bias, q, qi, wi, k_b.reshape(B, T, -1), v_b.reshape(B, T, -1), ki_t, t0, t1)


def _dsa_sample_index_body(*refs, pps, IH, ts, ts8, page, nc):
    qi_ref, wi_ref = refs[1], refs[2]
    ik_refs = refs[3:3 + pps]
    kn_ref, sc_ref, scn_ref, kb_ref = refs[3 + pps:]
    c = pl.program_id(1)
    for r in range(pps):
        kb_ref[:, r * page:(r + 1) * page] = ik_refs[r][...].astype(BF16)
    qi = qi_ref[0]
    wcol = wi_ref[0]
    d = jnp.maximum(_dot(qi, kb_ref[...]), 0.0) * wcol
    tot = d[0:ts8]
    for h in range(1, IH):
        tot = tot + d[h * ts8:(h + 1) * ts8]
    sc_ref[0] = tot

    @pl.when(c == 0)
    def _():
        cols = _new_key_scores([qi.astype(F32)], [kn_ref[0]], ts)
        ncols = []
        for col in cols:
            col = jnp.maximum(col, 0.0) * wcol
            t = col[0:ts8]
            for h in range(1, IH):
                t = t + col[h * ts8:(h + 1) * ts8]
            ncols.append(t)
        blk = _cols_to_block(ncols, ts8)
        qt = lax.broadcasted_iota(jnp.int32, (ts8, LANES), 0)
        lane = lax.broadcasted_iota(jnp.int32, (ts8, LANES), 1)
        scn_ref[0] = jnp.where(lane <= qt, blk, NEG_INF)


def _dsa_sample_index(qi_rows, wi_rows, pool_ik_t, layer, page_table, ki_new, IH, ts, ts8, pps=32):
    Bs, R, ID = qi_rows.shape
    n_pages = page_table.shape[1]
    page = pool_ik_t.shape[3]
    pps = min(pps, n_pages)
    nc = n_pages // pps
    W = pps * page

    def pmap(r):
        return lambda b, c, pt: (layer, pt[b * n_pages + c * pps + r], 0, 0)

    in_specs = [pl.BlockSpec((1, R, ID), lambda b, c, pt: (b, 0, 0)),
                pl.BlockSpec((1, R, 1), lambda b, c, pt: (b, 0, 0))]
    in_specs += [pl.BlockSpec((None, None, ID, page), pmap(r)) for r in range(pps)]
    in_specs += [pl.BlockSpec((1, ts, ID), lambda b, c, pt: (b, 0, 0))]
    grid_spec = pltpu.PrefetchScalarGridSpec(
        num_scalar_prefetch=1, grid=(Bs, nc), in_specs=in_specs,
        out_specs=[pl.BlockSpec((1, ts8, W), lambda b, c, pt: (b, 0, c)),
                   pl.BlockSpec((1, ts8, LANES), lambda b, c, pt: (b, 0, 0))],
        scratch_shapes=[pltpu.VMEM((ID, W), BF16)])
    return pl.pallas_call(
        functools.partial(_dsa_sample_index_body, pps=pps, IH=IH, ts=ts, ts8=ts8, page=page, nc=nc),
        grid_spec=grid_spec,
        out_shape=[jax.ShapeDtypeStruct((Bs, ts8, nc * W), F32), jax.ShapeDtypeStruct((Bs, ts8, LANES), F32)],
        compiler_params=_cparams(("parallel", "arbitrary")), name="dsa_sample_index")(
            page_table.reshape(-1), qi_rows, wi_rows, *([pool_ik_t] * pps), ki_new)


def _topk_threshold_body(sc_ref, scn_ref, thr_ref, *, k_sel):
    n_groups = sc_ref.shape[2] // LANES
    scn = scn_ref[...]
    lo0 = jnp.min(jnp.where(scn > NEG_INF, scn, jnp.inf), axis=-1, keepdims=True)
    hi0 = jnp.max(scn, axis=-1, keepdims=True)
    lo_g = hi_g = sc_ref[:, :, :LANES]
    for j in range(1, n_groups):
        t = sc_ref[:, :, j * LANES:(j + 1) * LANES]
        lo_g = jnp.minimum(lo_g, t)
        hi_g = jnp.maximum(hi_g, t)
    lo0 = jnp.minimum(lo0, jnp.min(lo_g, axis=-1, keepdims=True))
    hi0 = jnp.maximum(hi0, jnp.max(hi_g, axis=-1, keepdims=True))

    def count_ge(mid):
        acc = jnp.where(scn >= mid, 1.0, 0.0)
        for j in range(n_groups):
            acc = acc + jnp.where(sc_ref[:, :, j * LANES:(j + 1) * LANES] >= mid, 1.0, 0.0)
        return jnp.sum(acc, axis=-1, keepdims=True)

    thr = _bisect(lo0, hi0, count_ge, float(k_sel))
    thr_ref[...] = jnp.broadcast_to(thr, thr_ref.shape)


def _topk_threshold(scores, scores_new, k_sel, bb=16):
    Bs, ts8, P = scores.shape
    bb = min(bb, Bs)
    assert Bs % bb == 0
    return pl.pallas_call(
        functools.partial(_topk_threshold_body, k_sel=k_sel), grid=(Bs // bb,),
        in_specs=[pl.BlockSpec((bb, ts8, P), lambda i: (i, 0, 0)),
                  pl.BlockSpec((bb, ts8, LANES), lambda i: (i, 0, 0))],
        out_specs=pl.BlockSpec((bb, ts8, LANES), lambda i: (i, 0, 0)),
        out_shape=jax.ShapeDtypeStruct((Bs, ts8, LANES), F32),
        compiler_params=_cparams(("parallel",)), name="topk_threshold")(scores, scores_new)


def _dsa_sample_attn_body(*refs, pps, KVH, G, HD, ts, ts8, page, nc, scale):
    far_ref = refs[1]
    q_ref, sc_ref, scn_ref, thr_ref = refs[2], refs[3], refs[4], refs[5]
    k_refs = refs[6:6 + pps]
    v_refs = refs[6 + pps:6 + 2 * pps]
    kn_ref, vn_ref, tp_ref, tn_ref, o_ref, kb_ref, vb_ref, m_ref, l_ref, acc_ref = refs[6 + 2 * pps:]
    c = pl.program_id(1)
    R = G * ts8
    W = pps * page

    @pl.when(c == 0)
    def _():
        m_ref[...] = jnp.full(m_ref.shape, NEG_INF, F32)
        l_ref[...] = jnp.zeros(l_ref.shape, F32)
        acc_ref[...] = jnp.zeros(acc_ref.shape, F32)

    for r in range(pps):
        for n in range(KVH):
            kb_ref[r * page:(r + 1) * page, n * HD:(n + 1) * HD] = k_refs[r][pl.ds(n, page, stride=KVH), :].astype(BF16)
            vb_ref[r * page:(r + 1) * page, n * HD:(n + 1) * HD] = v_refs[r][pl.ds(n, page, stride=KVH), :].astype(BF16)
    thr = thr_ref[0][:, :1]
    msk = jnp.concatenate([jnp.where(sc_ref[0] >= thr, 0.0, NEG_INF)] * G, axis=0)
    last = c == nc - 1
    for n in range(KVH):
        qn = q_ref[0, n]
        far = jnp.concatenate([jnp.full((ts8, 1), far_ref[n * G + g], F32) for g in range(G)], axis=0)
        near = tp_ref[n * G:(n + 1) * G].reshape(R, LANES)
        delta = jnp.where(last, near - far, 0.0)
        s = _dot_nt(qn, kb_ref[:, n * HD:(n + 1) * HD]) * scale + far + msk
        s = jnp.concatenate([s[:, :W - LANES], s[:, W - LANES:] + delta], axis=1)
        _flash_update(s, vb_ref[:, n * HD:(n + 1) * HD], m_ref.at[n], l_ref.at[n], acc_ref.at[n])

    @pl.when(last)
    def _():
        mskn = jnp.concatenate([jnp.where(scn_ref[0] >= thr, 0.0, NEG_INF)] * G, axis=0)
        outs = []
        for n in range(KVH):
            qn = q_ref[0, n].astype(F32)
            kn = kn_ref[0][:, n * HD:(n + 1) * HD]
            vn = vn_ref[0][:, n * HD:(n + 1) * HD]
            cols = _new_key_scores([qn], [kn], ts)
            blk = _cols_to_block(cols, R) * scale + tn_ref[n * G:(n + 1) * G].reshape(R, LANES) + mskn
            _flash_update_new(blk, vn, ts, m_ref.at[n], l_ref.at[n], acc_ref.at[n])
            o = acc_ref[n] / l_ref[n]
            outs += [o[g * ts8:g * ts8 + ts] for g in range(G)]
        o_ref[0] = jnp.concatenate(outs, axis=1)


def _dsa_sample_attn(q_rows, scores, scores_new, thr, pool_k, pool_v, layer, page_table, k_new, v_new, tp, tn,
                     far_bias, KVH, G, HD, ts, ts8, scale, pps=32):
    Bs = q_rows.shape[0]
    R = G * ts8
    n_pages = page_table.shape[1]
    page = pool_k.shape[2] // KVH
    pps = min(pps, n_pages)
    nc = n_pages // pps
    W = pps * page
    NH = KVH * G

    def pmap(r):
        return lambda b, c, pt, f: (layer, pt[b * n_pages + c * pps + r], 0, 0)

    in_specs = [pl.BlockSpec((1, KVH, R, HD), lambda b, c, pt, f: (b, 0, 0, 0)),
                pl.BlockSpec((1, ts8, W), lambda b, c, pt, f: (b, 0, c)),
                pl.BlockSpec((1, ts8, LANES), lambda b, c, pt, f: (b, 0, 0)),
                pl.BlockSpec((1, ts8, LANES), lambda b, c, pt, f: (b, 0, 0))]
    in_specs += [pl.BlockSpec((None, None, page * KVH, HD), pmap(r)) for r in range(pps)] * 2
    in_specs += [pl.BlockSpec((1, ts, KVH * HD), lambda b, c, pt, f: (b, 0, 0))] * 2
    in_specs += [pl.BlockSpec((NH, ts8, LANES), lambda b, c, pt, f: (0, 0, 0))] * 2
    grid_spec = pltpu.PrefetchScalarGridSpec(
        num_scalar_prefetch=2, grid=(Bs, nc), in_specs=in_specs,
        out_specs=pl.BlockSpec((1, ts, NH * HD), lambda b, c, pt, f: (b, 0, 0)),
        scratch_shapes=[pltpu.VMEM((W, KVH * HD), BF16), pltpu.VMEM((W, KVH * HD), BF16),
                        pltpu.VMEM((KVH, R, 1), F32), pltpu.VMEM((KVH, R, 1), F32), pltpu.VMEM((KVH, R, HD), F32)])
    return pl.pallas_call(
        functools.partial(_dsa_sample_attn_body, pps=pps, KVH=KVH, G=G, HD=HD, ts=ts, ts8=ts8, page=page,
                          nc=nc, scale=scale),
        grid_spec=grid_spec, out_shape=jax.ShapeDtypeStruct((Bs, ts, NH * HD), F32),
        compiler_params=_cparams(("parallel", "arbitrary")), name="dsa_sample_attn")(
            page_table.reshape(-1), far_bias, q_rows, scores, scores_new, thr, *([pool_k] * pps), *([pool_v] * pps),
            k_new, v_new, tp, tn)


def _rot_tables(pos, half, group, extra=None):
    lane = jnp.arange(LANES)
    g = lane % group
    rotary = g < 2 * half
    freq = ROPE_THETA ** (-jnp.arange(half, dtype=F32) / half)
    ang = pos.astype(F32)[:, None] * freq[None, :]
    cos = jnp.cos(ang)[:, g % half]
    sin = jnp.sin(ang)[:, g % half]
    c_tab = jnp.where(rotary[None, :], cos, 1.0)
    s_tab = jnp.where(rotary[None, :], jnp.where((g < half)[None, :], -sin, sin), 0.0)
    if extra is not None:
        a, b, val = extra
        c_tab = jnp.where(((lane >= a) & (lane < b))[None, :], val, c_tab)
    return c_tab.astype(F32), s_tab.astype(F32)


def kernel(x_prompt, x_sample, cache_mla_ckv, cache_mla_kpe, cache_dsa_k, cache_dsa_v, cache_dsa_idx_k,
           page_table, c_prompt, c_sample, w_ada, b_ada, ln_g, ln_b,
           mla_w_in, mla_q_norm, mla_kv_norm, mla_w_uq, mla_w_uk, mla_w_uv, mla_w_o,
           dsa_w_in, dsa_w_o, rel_bias, ffn_w_gu, ffn_w_down):
    B, T, D = x_prompt.shape
    Bs, ts, _ = x_sample.shape
    depth = w_ada.shape[0]
    alpha = (2 * depth) ** 0.25
    n_pages = page_table.shape[1]
    page = cache_mla_ckv.shape[2]
    past = n_pages * page
    Mp, Ms = B * T, Bs * ts
    ts8 = 8 * (-(-ts // 8))

    ql = mla_q_norm.shape[1]
    kvl = cache_mla_ckv.shape[-1]
    rope = cache_mla_kpe.shape[-1]
    H = mla_w_uk.shape[2]
    nope = mla_w_uk.shape[3]
    vh = mla_w_uv.shape[3]
    mla_scale = (nope + rope) ** -0.5
    KVH, HD = cache_dsa_k.shape[-2:]
    ID = cache_dsa_idx_k.shape[-1]
    NH = dsa_w_o.shape[1] // HD
    G = NH // KVH
    dq, dkv = NH * HD, KVH * HD
    IH = (dsa_w_in.shape[2] - dq - 2 * dkv - ID) // (ID + 1)
    dsa_scale = HD ** -0.5
    idx_wscale = (IH ** -0.5) * (ID ** -0.5)
    tq_dsa = min(256, T)

    pos_p = jnp.arange(T, dtype=jnp.int32)
    pos_s = jnp.tile(past + jnp.arange(ts, dtype=jnp.int32), Bs)
    tabs = {
        "mla": (_rot_tables(pos_p, rope // 2, rope), _rot_tables(pos_s, rope // 2, rope)),
        "idx": (_rot_tables(pos_p, IDX_ROPE // 2, ID), _rot_tables(pos_s, IDX_ROPE // 2, ID)),
        "tail": (_rot_tables(pos_p, IDX_ROPE // 2, LANES, (ID, ID + IH, idx_wscale)),
                 _rot_tables(pos_s, IDX_ROPE // 2, LANES, (ID, ID + IH, idx_wscale))),
    }

    c_all = jnp.concatenate([c_prompt, c_sample], axis=0)
    nc_rows = c_all.shape[0]
    c_all = jnp.pad(c_all, ((0, -nc_rows % 8), (0, 0)))
    mods = [_mm(c_all, w_ada, layer=i, bias=b_ada[i].reshape(1, -1), act="silu", tm=c_all.shape[0], tn=1024,
                name="ada_mod") for i in range(depth)]

    pool_kpe_t = jnp.swapaxes(cache_mla_kpe, 2, 3)
    pool_ik_t = jnp.swapaxes(cache_dsa_idx_k, 2, 3)
    pool_k = cache_dsa_k.reshape(cache_dsa_k.shape[0], -1, page * cache_dsa_k.shape[3], cache_dsa_k.shape[4])
    pool_v = cache_dsa_v.reshape(pool_k.shape)

    def mod_vectors(i):
        m = mods[i]
        mp = [m[:B, j * D:(j + 1) * D].reshape(B, 1, D) for j in range(6)]
        ms = [jnp.repeat(m[B:B + Bs, j * D:(j + 1) * D], ts, axis=0) for j in range(6)]
        return mp, ms

    groups = [dict(y=x_prompt.reshape(Mp, D), rpg=T, nb=B, tb=T, g=0),
              dict(y=x_sample.reshape(Ms, D), rpg=None, nb=1, tb=Ms, g=1)]
    modv = [mod_vectors(i) for i in range(depth)]
    for grp in groups:
        sh1, sc1 = modv[0][grp["g"]][0], modv[0][grp["g"]][1]
        grp["u"] = _modulate(grp["y"], sh1, sc1, grp["rpg"])

    t0 = t1 = tp = tn = far_bias = None
    if depth > 1:
        t0, t1, tp, tn = _bias_tables(rel_bias, tq_dsa, ts8)
        far_bias = rel_bias[-1]

    outs = {k: [[], []] for k in ("ckv", "kpe", "k", "v", "ik")}
    for i in range(depth):
        j = i // 2
        if i % 2 == 0:
            w_in_pad = jnp.pad(mla_w_in[j], ((0, 0), (0, LANES - rope))).astype(BF16)
            w_uq = mla_w_uq[j].reshape(ql, H, nope + rope)
            w_nope = w_uq[:, :, :nope].reshape(ql, H * nope).astype(BF16)
            w_pe = w_uq[:, :, nope:].reshape(ql, H * rope).astype(BF16)
            w_uk_t = jnp.transpose(mla_w_uk[j], (1, 2, 0)).astype(BF16)
            w_uv = mla_w_uv[j].reshape(kvl, H * vh).astype(BF16)
            w_o = mla_w_o[j].astype(BF16)
        else:
            w_in = dsa_w_in[j].astype(BF16)
            w_tail = jnp.pad(dsa_w_in[j][:, dq + 2 * dkv + IH * ID:], ((0, 0), (0, LANES - ID - IH))).astype(BF16)
            w_o = dsa_w_o[j].astype(BF16)
        w_down = ffn_w_down[i].astype(BF16)

        for grp in groups:
            g = grp["g"]
            u = grp["u"]
            mv = modv[i][g]
            if i % 2 == 0:
                ct, st = tabs["mla"][g]
                cq, ckv, ckv_b, kpe, kpe_b = _mla_in(u, w_in_pad, mla_q_norm[j], mla_kv_norm[j], ct, st,
                                                     ql, kvl, rope)
                q_lat, q_pe = _mla_q(cq, w_nope, w_pe, w_uk_t, ct, st, grp["nb"], grp["tb"], H, nope, rope, kvl)
                if g == 0:
                    o = _mla_prompt_attn(q_lat, q_pe, ckv_b.reshape(B, T, kvl), kpe_b.reshape(B, T, rope),
                                         w_uv, mla_scale)
                else:
                    def rows(a):
                        n = a.shape[-1]
                        return a.reshape(H, Bs, ts, n).transpose(1, 0, 2, 3).reshape(Bs, H * ts, n)
                    o = _mla_sample_attn(rows(q_lat), rows(q_pe), cache_mla_ckv, pool_kpe_t, j, page_table,
                                         ckv.reshape(Bs, ts, kvl), kpe.reshape(Bs, ts, rope), w_uv, mla_scale, H)
                    o = o.reshape(Ms, H * vh).astype(BF16)
                outs["ckv"][g].append(ckv)
                outs["kpe"][g].append(kpe)
            else:
                q = _mm(u, w_in, n_cols=dq, col0=0, out_dtype=BF16, name="dsa_q")
                k, k_b = _mm(u, w_in, n_cols=dkv, col0=dq, second_dtype=BF16, name="dsa_k")
                v, v_b = _mm(u, w_in, n_cols=dkv, col0=dq + dkv, second_dtype=BF16, name="dsa_v")
                ci, si = tabs["idx"][g]
                qi = _mm(u, w_in, n_cols=IH * ID, col0=dq + 2 * dkv, out_dtype=BF16,
                         rot=(ci, si, IDX_ROPE // 2), name="dsa_qi")
                ctl, stl = tabs["tail"][g]
                tail = _mm(u, w_tail, rot=(ctl, stl, IDX_ROPE // 2), name="dsa_tail")
                ki = tail[:, :ID]
                wi = tail[:, ID:ID + IH]
                if g == 0:
                    ki_t = jnp.swapaxes(ki.astype(BF16).reshape(B, T // tq_dsa, tq_dsa, ID), 2, 3)
                    o = _dsa_prompt(q, qi, wi, k_b, v_b, ki_t, t0, t1, far_bias, B, T,
                                    min(TOPK_MAX, T // 4), IH, ID, KVH, G, HD, dsa_scale, tq_dsa)
                else:
                    pad_t = ((0, 0), (0, 0), (0, ts8 - ts), (0, 0))
                    qi_rows = jnp.pad(qi.reshape(Bs, ts, IH, ID).transpose(0, 2, 1, 3), pad_t
                                      ).reshape(Bs, IH * ts8, ID)
                    wi_rows = jnp.pad(wi.reshape(Bs, ts, IH).transpose(0, 2, 1), ((0, 0), (0, 0), (0, ts8 - ts))
                                      ).reshape(Bs, IH * ts8, 1)
                    sc, sc_new = _dsa_sample_index(qi_rows, wi_rows, pool_ik_t, j, page_table,
                                                   ki.reshape(Bs, ts, ID), IH, ts, ts8)
                    thr = _topk_threshold(sc, sc_new, min(TOPK_MAX, (past + ts) // 4))
                    q_rows = jnp.pad(q.reshape(Bs, ts, NH, HD).transpose(0, 2, 1, 3), pad_t
                                     ).reshape(Bs, KVH, G * ts8, HD)
                    o = _dsa_sample_attn(q_rows, sc, sc_new, thr, pool_k, pool_v, j,
                                         page_table, k.reshape(Bs, ts, dkv), v.reshape(Bs, ts, dkv), tp, tn, far_bias,
                                         KVH, G, HD, ts, ts8, dsa_scale)
                    o = o.reshape(Ms, NH * HD).astype(BF16)
                outs["k"][g].append(k)
                outs["v"][g].append(v)
                outs["ik"][g].append(ki)
            y1, u2 = _mm_postnorm(o, w_o, grp["y"], mv[2], ln_g[i, 0], ln_b[i, 0], (mv[3], mv[4]), grp["rpg"],
                                  alpha, name="attn_out_postnorm")
            hmid = _swiglu_up(u2, ffn_w_gu, i)
            nxt = None
            if i + 1 < depth:
                nmv = modv[i + 1][g]
                nxt = (nmv[0], nmv[1])
            grp["y"], grp["u"] = _mm_postnorm(hmid, w_down, y1, mv[5], ln_g[i, 1], ln_b[i, 1], nxt, grp["rpg"],
                                              alpha, name="ffn_down_postnorm")

    def stack(key, g, shape):
        return jnp.stack([a.reshape(shape) for a in outs[key][g]])

    yp = groups[0]["y"].reshape(B, T, D)
    ys = groups[1]["y"].reshape(Bs, ts, D)
    return (yp, ys,
            stack("ckv", 0, (B, T, kvl)), stack("kpe", 0, (B, T, rope)),
            stack("k", 0, (B, T, KVH, HD)), stack("v", 0, (B, T, KVH, HD)), stack("ik", 0, (B, T, ID)),
            stack("ckv", 1, (Bs, ts, kvl)), stack("kpe", 1, (Bs, ts, rope)),
            stack("k", 1, (Bs, ts, KVH, HD)), stack("v", 1, (Bs, ts, KVH, HD)), stack("ik", 1, (Bs, ts, ID)))
```

```python
import functools
import math

import jax
import jax.numpy as jnp
from jax import lax
from jax.experimental import pallas as pl
from jax.experimental.pallas import tpu as pltpu

F32 = jnp.float32
BF16 = jnp.bfloat16
NEG_INF = float("-inf")

LANES = 128
ROPE_THETA = 10000.0
IDX_ROPE = 32
TOPK_MAX = 256
REL_MAX_DIST = 128
LN_EPS = 1e-5
RMS_EPS = 1e-6
BISECT_ITERS = 32
VMEM_LIMIT_MB = 56


def _cparams(sem, vmem_mb=VMEM_LIMIT_MB):
    return pltpu.CompilerParams(dimension_semantics=sem, vmem_limit_bytes=vmem_mb * 1024 * 1024)


def _dot(a, b):
    return jnp.dot(a, b, preferred_element_type=F32)


def _dot_nt(a, b):
    return lax.dot_general(a, b, (((1,), (1,)), ((), ())), preferred_element_type=F32)


def _rot(v, c, s, half):
    lane = lax.broadcasted_iota(jnp.int32, v.shape, 1)
    partner = jnp.where((lane % (2 * half)) < half,
                        pltpu.roll(v, LANES - half, 1), pltpu.roll(v, half, 1))
    return v * c + partner * s


def _mm_body(*refs, nk, act, rot_half, has_bias, second):
    it = iter(refs)
    x_ref = next(it)
    w_ref = next(it)
    b_ref = next(it) if has_bias else None
    c_ref = next(it) if rot_half else None
    s_ref = next(it) if rot_half else None
    o_ref = next(it)
    o2_ref = next(it) if second else None
    acc_ref = next(it) if nk > 1 else None

    x = x_ref[...]
    if act == "silu":
        xf = x.astype(F32)
        x = xf * jax.nn.sigmoid(xf)
    part = _dot(x.astype(BF16), w_ref[...].astype(BF16))

    def finish(acc):
        if has_bias:
            acc = acc + b_ref[...]
        if rot_half:
            c = c_ref[...]
            s = s_ref[...]
            for j in range(acc.shape[1] // LANES):
                sl = slice(j * LANES, (j + 1) * LANES)
                r = _rot(acc[:, sl], c, s, rot_half)
                o_ref[:, sl] = r.astype(o_ref.dtype)
                if second:
                    o2_ref[:, sl] = r.astype(o2_ref.dtype)
        else:
            o_ref[...] = acc.astype(o_ref.dtype)
            if second:
                o2_ref[...] = acc.astype(o2_ref.dtype)

    if nk == 1:
        finish(part)
    else:
        k = pl.program_id(2)

        @pl.when(k == 0)
        def _():
            acc_ref[...] = part

        @pl.when(k > 0)
        def _():
            acc_ref[...] += part

        @pl.when(k == nk - 1)
        def _():
            finish(acc_ref[...])


def _mm(x, w, *, layer=None, n_cols=None, col0=0, tm=512, tn=512, tk=None, out_dtype=F32, second_dtype=None,
        act=None, bias=None, rot=None, name="mm"):
    M, K = x.shape
    N = n_cols if n_cols is not None else w.shape[-1]
    tm = min(tm, M)
    tn = min(tn, N)
    while N % tn or col0 % tn:
        tn //= 2
    tk = K if tk is None else min(tk, K)
    assert M % tm == 0 and K % tk == 0 and tn % LANES == 0
    nk = K // tk
    jb = col0 // tn
    if layer is None:
        w_spec = pl.BlockSpec((tk, tn), lambda i, j, k: (k, j + jb))
    else:
        w_spec = pl.BlockSpec((None, tk, tn), lambda i, j, k: (layer, k, j + jb))
    in_specs = [pl.BlockSpec((tm, tk), lambda i, j, k: (i, k)), w_spec]
    args = [x, w]
    if bias is not None:
        in_specs.append(pl.BlockSpec((1, tn), lambda i, j, k: (0, j)))
        args.append(bias)
    rot_half = 0
    if rot is not None:
        c_tab, s_tab, rot_half = rot
        nr = c_tab.shape[0] // tm
        assert c_tab.shape[0] % tm == 0
        for t in (c_tab, s_tab):
            in_specs.append(pl.BlockSpec((tm, LANES), lambda i, j, k: (i % nr, 0)))
            args.append(t)
    out_shape = [jax.ShapeDtypeStruct((M, N), out_dtype)]
    out_specs = [pl.BlockSpec((tm, tn), lambda i, j, k: (i, j))]
    if second_dtype is not None:
        out_shape.append(jax.ShapeDtypeStruct((M, N), second_dtype))
        out_specs.append(pl.BlockSpec((tm, tn), lambda i, j, k: (i, j)))
    scratch = [pltpu.VMEM((tm, tn), F32)] if nk > 1 else []
    res = pl.pallas_call(
        functools.partial(_mm_body, nk=nk, act=act, rot_half=rot_half, has_bias=bias is not None,
                          second=second_dtype is not None),
        grid=(M // tm, N // tn, nk),
        in_specs=in_specs, out_specs=out_specs, out_shape=out_shape, scratch_shapes=scratch,
        compiler_params=_cparams(("parallel", "parallel", "arbitrary")), name=name)(*args)
    return res if second_dtype is not None else res[0]


def _modulate_body(y_ref, sh_ref, sc_ref, u_ref):
    u_ref[...] = (y_ref[...] * (1.0 + sc_ref[...]) + sh_ref[...]).astype(u_ref.dtype)


def _mod_spec(tm, n, rows_per_group):
    if rows_per_group is None:
        return pl.BlockSpec((tm, n), lambda i, *_: (i, 0))
    per = rows_per_group // tm
    return pl.BlockSpec((None, 1, n), lambda i, *_: (i // per, 0, 0))


def _modulate(y, sh, sc, rows_per_group, tm=512):
    M, D = y.shape
    tm = min(tm, M)
    return pl.pallas_call(
        _modulate_body, grid=(M // tm,),
        in_specs=[pl.BlockSpec((tm, D), lambda i: (i, 0)), _mod_spec(tm, D, rows_per_group),
                  _mod_spec(tm, D, rows_per_group)],
        out_specs=pl.BlockSpec((tm, D), lambda i: (i, 0)),
        out_shape=jax.ShapeDtypeStruct((M, D), BF16),
        compiler_params=_cparams(("parallel",)), name="modulate")(y, sh, sc)


def _mm_postnorm_body(*refs, nk, alpha, with_mod):
    if with_mod:
        x_ref, w_ref, y_ref, g_ref, lg_ref, lb_ref, sh_ref, sc_ref, yo_ref, u_ref, acc_ref = refs
    else:
        x_ref, w_ref, y_ref, g_ref, lg_ref, lb_ref, yo_ref, acc_ref = refs
    k = pl.program_id(1)
    part = _dot(x_ref[...].astype(BF16), w_ref[...].astype(BF16))

    @pl.when(k == 0)
    def _():
        acc_ref[...] = part

    @pl.when(k > 0)
    def _():
        acc_ref[...] += part

    @pl.when(k == nk - 1)
    def _():
        z = alpha * y_ref[...] + g_ref[...] * acc_ref[...]
        mu = jnp.mean(z, axis=-1, keepdims=True)
        zc = z - mu
        var = jnp.mean(zc * zc, axis=-1, keepdims=True)
        yn = zc * lax.rsqrt(var + LN_EPS) * lg_ref[...] + lb_ref[...]
        yo_ref[...] = yn
        if with_mod:
            u_ref[...] = (yn * (1.0 + sc_ref[...]) + sh_ref[...]).astype(u_ref.dtype)


def _mm_postnorm(x, w, y, gate, ln_g, ln_b, mod, rows_per_group, alpha, tm=512, tk=512, name="mm_postnorm"):
    M, K = x.shape
    N = w.shape[1]
    tm = min(tm, M)
    tk = min(tk, K)
    while K % tk:
        tk //= 2
    assert M % tm == 0 and tk % LANES == 0
    nk = K // tk
    row = pl.BlockSpec((tm, N), lambda i, k: (i, 0))
    vec = pl.BlockSpec((1, N), lambda i, k: (0, 0))
    in_specs = [pl.BlockSpec((tm, tk), lambda i, k: (i, k)), pl.BlockSpec((tk, N), lambda i, k: (k, 0)),
                row, _mod_spec(tm, N, rows_per_group), vec, vec]
    args = [x, w, y, gate, ln_g.reshape(1, N), ln_b.reshape(1, N)]
    out_shape = [jax.ShapeDtypeStruct((M, N), F32)]
    out_specs = [row]
    if mod is not None:
        in_specs += [_mod_spec(tm, N, rows_per_group)] * 2
        args += list(mod)
        out_shape.append(jax.ShapeDtypeStruct((M, N), BF16))
        out_specs.append(row)
    res = pl.pallas_call(
        functools.partial(_mm_postnorm_body, nk=nk, alpha=alpha, with_mod=mod is not None),
        grid=(M // tm, nk), in_specs=in_specs, out_specs=out_specs, out_shape=out_shape,
        scratch_shapes=[pltpu.VMEM((tm, N), F32)],
        compiler_params=_cparams(("parallel", "arbitrary")), name=name)(*args)
    return (res[0], res[1]) if mod is not None else (res[0], None)


def _swiglu_body(x_ref, wg_ref, wu_ref, o_ref):
    x = x_ref[...]
    g = _dot(x, wg_ref[...].astype(BF16))
    u = _dot(x, wu_ref[...].astype(BF16))
    o_ref[...] = (g * jax.nn.sigmoid(g) * u).astype(o_ref.dtype)


def _swiglu_up(x, w_gu, layer, tm=1024, tn=512):
    M, K = x.shape
    F = w_gu.shape[2] // 2
    tm = min(tm, M)
    tn = min(tn, F)
    while F % tn:
        tn //= 2
    nj = F // tn
    return pl.pallas_call(
        _swiglu_body, grid=(M // tm, nj),
        in_specs=[pl.BlockSpec((tm, K), lambda i, j: (i, 0)),
                  pl.BlockSpec((None, K, tn), lambda i, j: (layer, 0, j)),
                  pl.BlockSpec((None, K, tn), lambda i, j: (layer, 0, j + nj))],
        out_specs=pl.BlockSpec((tm, tn), lambda i, j: (i, j)),
        out_shape=jax.ShapeDtypeStruct((M, F), BF16),
        compiler_params=_cparams(("parallel", "parallel")), name="swiglu_up")(x, w_gu, w_gu)


def _mla_in_body(x_ref, w_ref, qg_ref, kvg_ref, c_ref, s_ref, cq_ref, ckv_ref, ckvb_ref, kpe_ref, kpeb_ref,
                 *, ql, kvl, rope):
    acc = _dot(x_ref[...], w_ref[...])

    def rms(v, g):
        return v * lax.rsqrt(jnp.mean(v * v, axis=-1, keepdims=True) + RMS_EPS) * g

    cq_ref[...] = rms(acc[:, :ql], qg_ref[...]).astype(cq_ref.dtype)
    ckv = rms(acc[:, ql:ql + kvl], kvg_ref[...])
    ckv_ref[...] = ckv
    ckvb_ref[...] = ckv.astype(ckvb_ref.dtype)
    kpe = _rot(acc[:, ql + kvl:], c_ref[...], s_ref[...], rope // 2)[:, :rope]
    kpe_ref[...] = kpe
    kpeb_ref[...] = kpe.astype(kpeb_ref.dtype)


def _mla_in(u, w_pad, q_g, kv_g, c_tab, s_tab, ql, kvl, rope, tm=512):
    M, K = u.shape
    N = w_pad.shape[1]
    tm = min(tm, M)
    nr = c_tab.shape[0] // tm
    tab = pl.BlockSpec((tm, LANES), lambda i: (i % nr, 0))

    def row(n):
        return pl.BlockSpec((tm, n), lambda i: (i, 0))

    return pl.pallas_call(
        functools.partial(_mla_in_body, ql=ql, kvl=kvl, rope=rope), grid=(M // tm,),
        in_specs=[row(K), pl.BlockSpec((K, N), lambda i: (0, 0)),
                  pl.BlockSpec((1, ql), lambda i: (0, 0)), pl.BlockSpec((1, kvl), lambda i: (0, 0)), tab, tab],
        out_specs=[row(ql), row(kvl), row(kvl), row(rope), row(rope)],
        out_shape=[jax.ShapeDtypeStruct((M, ql), BF16), jax.ShapeDtypeStruct((M, kvl), F32),
                   jax.ShapeDtypeStruct((M, kvl), BF16), jax.ShapeDtypeStruct((M, rope), F32),
                   jax.ShapeDtypeStruct((M, rope), BF16)],
        compiler_params=_cparams(("parallel",)), name="mla_in")(
            u, w_pad, q_g.reshape(1, ql), kv_g.reshape(1, kvl), c_tab, s_tab)


def _mla_q_body(cq_ref, wn_ref, wp_ref, wuk_ref, c_ref, s_ref, ql_ref, qp_ref, *, H, nope, rope):
    cq = cq_ref[...]
    qn = _dot(cq, wn_ref[...]).astype(BF16)
    for h in range(H):
        ql_ref[0, h] = _dot(qn[:, h * nope:(h + 1) * nope], wuk_ref[h]).astype(ql_ref.dtype)
    qp = _dot(cq, wp_ref[...])
    c = c_ref[...]
    s = s_ref[...]
    per = LANES // rope
    for j in range(H // per):
        r = _rot(qp[:, j * LANES:(j + 1) * LANES], c, s, rope // 2)
        for e in range(per):
            qp_ref[0, j * per + e] = r[:, e * rope:(e + 1) * rope].astype(qp_ref.dtype)


def _mla_q(cq, w_nope, w_pe, w_uk_t, c_tab, s_tab, nb, tb, H, nope, rope, kvl, tm=512):
    M, ql = cq.shape
    tm = min(tm, tb)
    per = tb // tm
    nr = c_tab.shape[0] // tm
    tab = pl.BlockSpec((tm, LANES), lambda i: (i % nr, 0))
    full2 = lambda a: pl.BlockSpec(a.shape, lambda i: (0, 0))
    return pl.pallas_call(
        functools.partial(_mla_q_body, H=H, nope=nope, rope=rope), grid=(M // tm,),
        in_specs=[pl.BlockSpec((tm, ql), lambda i: (i, 0)), full2(w_nope), full2(w_pe),
                  pl.BlockSpec(w_uk_t.shape, lambda i: (0, 0, 0)), tab, tab],
        out_specs=[pl.BlockSpec((1, H, tm, kvl), lambda i: (i // per, 0, i % per, 0)),
                   pl.BlockSpec((1, H, tm, rope), lambda i: (i // per, 0, i % per, 0))],
        out_shape=[jax.ShapeDtypeStruct((nb, H, tb, kvl), BF16), jax.ShapeDtypeStruct((nb, H, tb, rope), BF16)],
        compiler_params=_cparams(("parallel",)), name="mla_q")(cq, w_nope, w_pe, w_uk_t, c_tab, s_tab)


def _flash_update(s, v, m_ref, l_ref, acc_ref):
    m_old = m_ref[...]
    m_new = jnp.maximum(m_old, jnp.max(s, axis=-1, keepdims=True))
    m_safe = jnp.where(m_new == NEG_INF, 0.0, m_new)
    a = jnp.exp(m_old - m_safe)
    p = jnp.exp(s - m_safe)
    l_ref[...] = a * l_ref[...] + jnp.sum(p, axis=-1, keepdims=True)
    acc_ref[...] = a * acc_ref[...] + _dot(p.astype(BF16), v)
    m_ref[...] = m_new


def _lane_fold(x, op):
    r = x[:, :LANES]
    for j in range(1, x.shape[1] // LANES):
        r = op(r, x[:, j * LANES:(j + 1) * LANES])
    return r


def _mla_prompt_body(ql_ref, qp_ref, ckv_ref, kpe_ref, wuv_ref, o_ref, s_ref, mx_ref, lp_ref, acc_ref,
                     *, H, tq, tk, scale, vh):
    i = pl.program_id(1)
    R = H * tq
    n_blk = lax.div(i * tq + tq - 1, tk) + 1

    def score(c, masked):
        ql = ql_ref[0].reshape(R, ql_ref.shape[-1])
        qp = qp_ref[0].reshape(R, qp_ref.shape[-1])
        start = pl.multiple_of(c * tk, tk)
        s = (_dot_nt(ql, ckv_ref[0, pl.ds(start, tk), :]) + _dot_nt(qp, kpe_ref[0, pl.ds(start, tk), :])) * scale
        if masked:
            tok = i * tq + (lax.broadcasted_iota(jnp.int32, s.shape, 0) % tq)
            key = c * tk + lax.broadcasted_iota(jnp.int32, s.shape, 1)
            s = jnp.where(key <= tok, s, NEG_INF)
        s_ref[c] = s
        return _lane_fold(s, jnp.maximum)

    mx_ref[...] = jnp.full(mx_ref.shape, NEG_INF, F32)

    def far(c, _):
        mx_ref[...] = jnp.maximum(mx_ref[...], score(c, False))
        return 0

    lax.fori_loop(0, n_blk - 1, far, 0)
    mx = jnp.maximum(mx_ref[...], score(n_blk - 1, True))
    m = jnp.max(mx, axis=-1, keepdims=True)

    lp_ref[...] = jnp.zeros(lp_ref.shape, F32)
    acc_ref[...] = jnp.zeros(acc_ref.shape, F32)

    def pv(c, _):
        p = jnp.exp(s_ref[c] - m)
        lp_ref[...] += _lane_fold(p, jnp.add)
        acc_ref[...] += _dot(p.astype(BF16), ckv_ref[0, pl.ds(pl.multiple_of(c * tk, tk), tk), :])
        return 0

    lax.fori_loop(0, n_blk, pv, 0)
    o = (acc_ref[...] / jnp.sum(lp_ref[...], axis=-1, keepdims=True)).astype(BF16)
    for h in range(H):
        o_ref[:, h * vh:(h + 1) * vh] = _dot(o[h * tq:(h + 1) * tq], wuv_ref[:, h * vh:(h + 1) * vh]
                                             ).astype(o_ref.dtype)


def _mla_prompt_attn(q_lat, q_pe, ckv_b, kpe_b, w_uv, scale, tq=128, tk=256):
    B, H, T, C = q_lat.shape
    R = q_pe.shape[-1]
    tq = min(tq, T)
    tk = min(tk, T)
    nq, nk = T // tq, T // tk
    vh = w_uv.shape[1] // H
    return pl.pallas_call(
        functools.partial(_mla_prompt_body, H=H, tq=tq, tk=tk, scale=scale, vh=vh),
        grid=(B, nq),
        in_specs=[pl.BlockSpec((1, H, tq, C), lambda b, i: (b, 0, i, 0)),
                  pl.BlockSpec((1, H, tq, R), lambda b, i: (b, 0, i, 0)),
                  pl.BlockSpec((1, T, C), lambda b, i: (b, 0, 0)), pl.BlockSpec((1, T, R), lambda b, i: (b, 0, 0)),
                  pl.BlockSpec(w_uv.shape, lambda b, i: (0, 0))],
        out_specs=pl.BlockSpec((tq, H * vh), lambda b, i: (b * nq + i, 0)),
        out_shape=jax.ShapeDtypeStruct((B * T, H * vh), BF16),
        scratch_shapes=[pltpu.VMEM((nk, H * tq, tk), F32), pltpu.VMEM((H * tq, LANES), F32),
                        pltpu.VMEM((H * tq, LANES), F32), pltpu.VMEM((H * tq, C), F32)],
        compiler_params=_cparams(("parallel", "arbitrary")), name="mla_prompt_attn")(
            q_lat, q_pe, ckv_b, kpe_b, w_uv)


def _new_key_scores(q_parts, k_parts, n_new):
    cols = []
    for t in range(n_new):
        acc = None
        for q, k in zip(q_parts, k_parts):
            d = jnp.sum(q * k[t:t + 1, :], axis=-1, keepdims=True)
            acc = d if acc is None else acc + d
        cols.append(acc)
    return cols


def _cols_to_block(cols, rows):
    lane = lax.broadcasted_iota(jnp.int32, (rows, LANES), 1)
    blk = jnp.full((rows, LANES), NEG_INF, F32)
    for t, c in enumerate(cols):
        blk = jnp.where(lane == t, c, blk)
    return blk


def _flash_update_new(s_blk, v_new, n_new, m_ref, l_ref, acc_ref):
    m_old = m_ref[...]
    m_new = jnp.maximum(m_old, jnp.max(s_blk, axis=-1, keepdims=True))
    m_safe = jnp.where(m_new == NEG_INF, 0.0, m_new)
    a = jnp.exp(m_old - m_safe)
    p = jnp.exp(s_blk - m_safe)
    l_ref[...] = a * l_ref[...] + jnp.sum(p, axis=-1, keepdims=True)
    acc = a * acc_ref[...]
    for t in range(n_new):
        acc = acc + p[:, t:t + 1] * v_new[t:t + 1, :]
    acc_ref[...] = acc
    m_ref[...] = m_new


def _page_copies(pt_ref, step, slot, streams, sem_ref, pps, page, layer):
    copies = []
    for r in range(pps):
        pid = pt_ref[step * pps + r]
        for pool, buf, lanes in streams:
            if lanes:
                dst = buf.at[slot, :, pl.ds(r * page, page)]
            else:
                rows = pool.shape[2]
                dst = buf.at[slot, pl.ds(r * rows, rows)]
            copies.append(pltpu.make_async_copy(pool.at[layer, pid], dst, sem_ref.at[slot]))
    return copies


def _fetch_pages(pt_ref, streams, sem_ref, pps, page, layer, n_steps):
    s = pl.program_id(0) * pl.num_programs(1) + pl.program_id(1)
    slot = s % 2

    @pl.when(s == 0)
    def _():
        for cp in _page_copies(pt_ref, 0, 0, streams, sem_ref, pps, page, layer):
            cp.start()

    @pl.when(s + 1 < n_steps)
    def _():
        for cp in _page_copies(pt_ref, s + 1, 1 - slot, streams, sem_ref, pps, page, layer):
            cp.start()

    for cp in _page_copies(pt_ref, s, slot, streams, sem_ref, pps, page, layer):
        cp.wait()
    return slot


def _mla_sample_body(pt_ref, ql_ref, qp_ref, ckv_pool, kpe_pool, cn_ref, kn_ref, wuv_ref, o_ref,
                     cbuf, pbuf, sem, kc_ref, kp_ref, m_ref, l_ref, acc_ref,
                     *, pps, H, ts, page, nc, n_steps, layer, scale, vh):
    c = pl.program_id(1)
    R = H * ts
    slot = _fetch_pages(pt_ref, [(ckv_pool, cbuf, False), (kpe_pool, pbuf, True)], sem, pps, page, layer, n_steps)

    @pl.when(c == 0)
    def _():
        m_ref[...] = jnp.full(m_ref.shape, NEG_INF, F32)
        l_ref[...] = jnp.zeros(l_ref.shape, F32)
        acc_ref[...] = jnp.zeros(acc_ref.shape, F32)

    kc_ref[...] = cbuf[slot].astype(BF16)
    kp_ref[...] = pbuf[slot].astype(BF16)
    ql = ql_ref[0]
    qp = qp_ref[0]
    ckv = kc_ref[...]
    s = (_dot_nt(ql, ckv) + _dot(qp, kp_ref[...])) * scale
    _flash_update(s, ckv, m_ref, l_ref, acc_ref)

    @pl.when(c == nc - 1)
    def _():
        cn = cn_ref[0]
        cols = _new_key_scores([ql.astype(F32), qp.astype(F32)], [cn, kn_ref[0]], ts)
        blk = _cols_to_block(cols, R) * scale
        qt = lax.broadcasted_iota(jnp.int32, (R, LANES), 0) % ts
        lane = lax.broadcasted_iota(jnp.int32, (R, LANES), 1)
        blk = jnp.where(lane <= qt, blk, NEG_INF)
        _flash_update_new(blk, cn, ts, m_ref, l_ref, acc_ref)
        o = (acc_ref[...] / l_ref[...]).astype(BF16)
        full = _dot(o, wuv_ref[...])
        o_ref[0] = jnp.concatenate([full[h * ts:(h + 1) * ts, h * vh:(h + 1) * vh] for h in range(H)], axis=1)


def _mla_sample_attn(q_lat, q_pe, pool_ckv, pool_kpe_t, layer, page_table, ckv_new, kpe_new, w_uv, scale, H,
                     pps=32):
    Bs, R, C = q_lat.shape
    Rr = q_pe.shape[-1]
    ts = R // H
    n_pages = page_table.shape[1]
    page = pool_ckv.shape[2]
    pps = min(pps, n_pages)
    nc = n_pages // pps
    vh = w_uv.shape[1] // H

    W = pps * page
    hbm = pl.BlockSpec(memory_space=pl.ANY)
    in_specs = [pl.BlockSpec((1, R, C), lambda b, c, pt: (b, 0, 0)),
                pl.BlockSpec((1, R, Rr), lambda b, c, pt: (b, 0, 0)),
                hbm, hbm,
                pl.BlockSpec((1, ts, C), lambda b, c, pt: (b, 0, 0)),
                pl.BlockSpec((1, ts, Rr), lambda b, c, pt: (b, 0, 0)),
                pl.BlockSpec(w_uv.shape, lambda b, c, pt: (0, 0))]
    grid_spec = pltpu.PrefetchScalarGridSpec(
        num_scalar_prefetch=1, grid=(Bs, nc), in_specs=in_specs,
        out_specs=pl.BlockSpec((1, ts, H * vh), lambda b, c, pt: (b, 0, 0)),
        scratch_shapes=[pltpu.VMEM((2, W, C), F32), pltpu.VMEM((2, Rr, W), F32), pltpu.SemaphoreType.DMA((2,)),
                        pltpu.VMEM((W, C), BF16), pltpu.VMEM((Rr, W), BF16),
                        pltpu.VMEM((R, 1), F32), pltpu.VMEM((R, 1), F32), pltpu.VMEM((R, C), F32)])
    return pl.pallas_call(
        functools.partial(_mla_sample_body, pps=pps, H=H, ts=ts, page=page, nc=nc, n_steps=Bs * nc, layer=layer,
                          scale=scale, vh=vh),
        grid_spec=grid_spec, out_shape=jax.ShapeDtypeStruct((Bs, ts, H * vh), F32),
        compiler_params=_cparams(("arbitrary", "arbitrary")), name="mla_sample_attn")(
            page_table.reshape(-1), q_lat, q_pe, pool_ckv, pool_kpe_t, ckv_new, kpe_new, w_uv)


def _t5_bias_of_dist(d, rb_ref, h, n_buckets):
    dist = jnp.maximum(d, 0)
    exact = n_buckets // 2
    df = jnp.maximum(dist, 1).astype(F32)
    large = exact + (jnp.log(df / exact) / math.log(REL_MAX_DIST / exact) * (n_buckets - exact)).astype(jnp.int32)
    large = jnp.minimum(large, n_buckets - 1)
    bucket = jnp.where(dist < exact, dist, large)
    out = jnp.zeros(d.shape, F32)
    for b in range(n_buckets):
        out = jnp.where(bucket == b, rb_ref[b, h], out)
    return out


def _bias_tables_body(rb_ref, t0_ref, t1_ref, tp_ref, tn_ref, *, NH, tq, ts8, n_buckets):
    r = lax.broadcasted_iota(jnp.int32, (tq, tq), 0)
    c = lax.broadcasted_iota(jnp.int32, (tq, tq), 1)
    r8 = lax.broadcasted_iota(jnp.int32, (ts8, LANES), 0)
    c8 = lax.broadcasted_iota(jnp.int32, (ts8, LANES), 1)
    for h in range(NH):
        t0_ref[h] = _t5_bias_of_dist(r - c, rb_ref, h, n_buckets)
        t1_ref[h] = _t5_bias_of_dist(r - c + tq, rb_ref, h, n_buckets)
        tp_ref[h] = _t5_bias_of_dist(r8 - c8 + LANES, rb_ref, h, n_buckets)
        tn_ref[h] = _t5_bias_of_dist(r8 - c8, rb_ref, h, n_buckets)


def _bias_tables(rel_bias, tq, ts8):
    nb, NH = rel_bias.shape
    return pl.pallas_call(
        functools.partial(_bias_tables_body, NH=NH, tq=tq, ts8=ts8, n_buckets=nb),
        in_specs=[pl.BlockSpec(memory_space=pltpu.SMEM)],
        out_shape=[jax.ShapeDtypeStruct((NH, tq, tq), F32), jax.ShapeDtypeStruct((NH, tq, tq), F32),
                   jax.ShapeDtypeStruct((NH, ts8, LANES), F32), jax.ShapeDtypeStruct((NH, ts8, LANES), F32)],
        compiler_params=pltpu.CompilerParams(vmem_limit_bytes=VMEM_LIMIT_MB * 1024 * 1024),
        name="t5_bias_tables")(rel_bias)


def _bisect(lo0, hi0, count_ge, k_sel):
    def body(_, lh):
        lo, hi = lh
        mid = 0.5 * (lo + hi)
        ge = count_ge(mid) >= k_sel
        return jnp.where(ge, mid, lo), jnp.where(ge, hi, mid)

    lo, _ = lax.fori_loop(0, BISECT_ITERS, body, (lo0, hi0))
    return lo


def _dsa_prompt_body(far_ref, q_ref, qi_ref, wi_ref, k_ref, v_ref, kit_ref, t0_ref, t1_ref, o_ref,
                     sc_ref, wb_ref, s_ref, lp_ref, acc_ref, *, tq, nq, k_sel, IH, ID, KVH, G, HD, scale):
    i = pl.program_id(1)
    row_tok = i * tq + lax.broadcasted_iota(jnp.int32, (tq, tq), 0)
    col_in = lax.broadcasted_iota(jnp.int32, (tq, tq), 1)
    qi = qi_ref[...]
    wi = wi_ref[...]
    for h in range(IH):
        wb_ref[h] = jnp.broadcast_to(wi[:, h:h + 1], (tq, tq))

    def idx_block(c, carry):
        lo, hi = carry
        kit = kit_ref[0, c]
        tot = jnp.zeros((tq, tq), F32)
        for h in range(IH):
            d = _dot(qi[:, h * ID:(h + 1) * ID], kit)
            tot = tot + jnp.maximum(d, 0.0) * wb_ref[h]
        ok = (c * tq + col_in) <= row_tok
        sc_ref[c] = jnp.where(ok, tot, NEG_INF)
        lo = jnp.minimum(lo, jnp.min(jnp.where(ok, tot, jnp.inf), axis=-1, keepdims=True))
        hi = jnp.maximum(hi, jnp.max(jnp.where(ok, tot, NEG_INF), axis=-1, keepdims=True))
        return lo, hi

    lo0, hi0 = lax.fori_loop(0, i + 1, idx_block,
                             (jnp.full((tq, 1), jnp.inf, F32), jnp.full((tq, 1), NEG_INF, F32)))

    def count_ge(mid):
        def blk(c, acc):
            t = sc_ref[c]
            part = jnp.zeros((tq, LANES), F32)
            for j in range(tq // LANES):
                part = part + jnp.where(t[:, j * LANES:(j + 1) * LANES] >= mid, 1.0, 0.0)
            return acc + part
        acc = lax.fori_loop(0, i + 1, blk, jnp.zeros((tq, LANES), F32))
        return jnp.sum(acc, axis=-1, keepdims=True)

    thr = _bisect(lo0, hi0, count_ge, float(k_sel))

    def to_mask(c, _):
        sc_ref[c] = jnp.where(sc_ref[c] >= thr, 0.0, NEG_INF)
        return 0

    lax.fori_loop(0, i + 1, to_mask, 0)

    q = q_ref[...]
    R = G * tq
    nlv = tq // LANES

    def lane_fold(x, op):
        r = x[:, :LANES]
        for jj in range(1, nlv):
            r = op(r, x[:, jj * LANES:(jj + 1) * LANES])
        return r

    for n in range(KVH):
        qn = jnp.concatenate([q[:, (n * G + g) * HD:(n * G + g + 1) * HD] for g in range(G)], axis=0)

        def score(c, slot, bias):
            kc = k_ref[0, pl.ds(pl.multiple_of(c * tq, tq), tq), n * HD:(n + 1) * HD]
            s = _dot_nt(qn, kc) * scale + bias + jnp.concatenate([sc_ref[c]] * G, axis=0)
            s_ref[slot] = s
            return lane_fold(s, jnp.maximum)

        far = jnp.concatenate([jnp.full((tq, 1), far_ref[n * G + g], F32) for g in range(G)], axis=0)
        mx = lax.fori_loop(0, jnp.maximum(i - 1, 0), lambda c, mx: jnp.maximum(mx, score(c, c, far)),
                           jnp.full((R, LANES), NEG_INF, F32))
        prev = score(jnp.maximum(i - 1, 0), jnp.where(i >= 1, i - 1, nq), t1_ref[n * G:(n + 1) * G].reshape(R, tq))
        mx = jnp.where(i >= 1, jnp.maximum(mx, prev), mx)
        mx = jnp.maximum(mx, score(i, i, t0_ref[n * G:(n + 1) * G].reshape(R, tq)))
        m = jnp.max(mx, axis=-1, keepdims=True)

        lp_ref[...] = jnp.zeros(lp_ref.shape, F32)
        acc_ref[...] = jnp.zeros(acc_ref.shape, F32)

        def pv(c, _):
            p = jnp.exp(s_ref[c] - m)
            lp_ref[...] += lane_fold(p, jnp.add)
            vc = v_ref[0, pl.ds(pl.multiple_of(c * tq, tq), tq), n * HD:(n + 1) * HD]
            acc_ref[...] += _dot(p.astype(BF16), vc)
            return 0

        lax.fori_loop(0, i + 1, pv, 0)
        o = acc_ref[...] / jnp.sum(lp_ref[...], axis=-1, keepdims=True)
        for g in range(G):
            o_ref[:, (n * G + g) * HD:(n * G + g + 1) * HD] = o[g * tq:(g + 1) * tq].astype(o_ref.dtype)


def _dsa_prompt(q, qi, wi, k_b, v_b, ki_t, t0, t1, far_bias, B, T, k_sel, IH, ID, KVH, G, HD, scale, tq):
    M = B * T
    nq = T // tq
    NH = KVH * G
    row = lambda n: pl.BlockSpec((tq, n), lambda b, i, f: (b * nq + i, 0))
    seq = lambda n: pl.BlockSpec((1, T, n), lambda b, i, f: (b, 0, 0))
    tab = pl.BlockSpec((NH, tq, tq), lambda b, i, f: (0, 0, 0))
    grid_spec = pltpu.PrefetchScalarGridSpec(
        num_scalar_prefetch=1, grid=(B, nq),
        in_specs=[row(NH * HD), row(IH * ID), row(IH), seq(KVH * HD), seq(KVH * HD),
                  pl.BlockSpec((1, nq, ID, tq), lambda b, i, f: (b, 0, 0, 0)), tab, tab],
        out_specs=row(NH * HD),
        scratch_shapes=[pltpu.VMEM((nq, tq, tq), F32), pltpu.VMEM((IH, tq, tq), F32),
                        pltpu.VMEM((nq + 1, G * tq, tq), F32), pltpu.VMEM((G * tq, LANES), F32),
                        pltpu.VMEM((G * tq, HD), F32)])
    return pl.pallas_call(
        functools.partial(_dsa_prompt_body, tq=tq, nq=nq, k_sel=k_sel, IH=IH, ID=ID, KVH=KVH, G=G, HD=HD,
                          scale=scale),
        grid_spec=grid_spec, out_shape=jax.ShapeDtypeStruct((M, NH * HD), BF16),
        compiler_params=_cparams(("parallel", "arbitrary")), name="dsa_prompt")(
            far_bias, q, qi, wi, k_b.reshape(B, T, -1), v_b.reshape(B, T, -1), ki_t, t0, t1)


def _dsa_sample_index_body(pt_ref, qi_ref, wi_ref, ik_pool, kn_ref, sc_ref, scn_ref, ibuf, sem,
                           *, pps, IH, ts, ts8, page, n_steps, layer):
    c = pl.program_id(1)
    slot = _fetch_pages(pt_ref, [(ik_pool, ibuf, True)], sem, pps, page, layer, n_steps)
    qi = qi_ref[0]
    wcol = wi_ref[0]
    kit = ibuf[slot].astype(BF16)
    d = jnp.maximum(_dot(qi, kit), 0.0) * wcol
    tot = d[0:ts8]
    for h in range(1, IH):
        tot = tot + d[h * ts8:(h + 1) * ts8]
    sc_ref[0] = tot

    @pl.when(c == 0)
    def _():
        cols = _new_key_scores([qi.astype(F32)], [kn_ref[0]], ts)
        ncols = []
        for col in cols:
            col = jnp.maximum(col, 0.0) * wcol
            t = col[0:ts8]
            for h in range(1, IH):
                t = t + col[h * ts8:(h + 1) * ts8]
            ncols.append(t)
        blk = _cols_to_block(ncols, ts8)
        qt = lax.broadcasted_iota(jnp.int32, (ts8, LANES), 0)
        lane = lax.broadcasted_iota(jnp.int32, (ts8, LANES), 1)
        scn_ref[0] = jnp.where(lane <= qt, blk, NEG_INF)


def _dsa_sample_index(qi_rows, wi_rows, pool_ik_t, layer, page_table, ki_new, IH, ts, ts8, pps=32):
    Bs, R, ID = qi_rows.shape
    n_pages = page_table.shape[1]
    page = pool_ik_t.shape[3]
    pps = min(pps, n_pages)
    nc = n_pages // pps
    W = pps * page

    in_specs = [pl.BlockSpec((1, R, ID), lambda b, c, pt: (b, 0, 0)),
                pl.BlockSpec((1, R, 1), lambda b, c, pt: (b, 0, 0)),
                pl.BlockSpec(memory_space=pl.ANY),
                pl.BlockSpec((1, ts, ID), lambda b, c, pt: (b, 0, 0))]
    grid_spec = pltpu.PrefetchScalarGridSpec(
        num_scalar_prefetch=1, grid=(Bs, nc), in_specs=in_specs,
        out_specs=[pl.BlockSpec((1, ts8, W), lambda b, c, pt: (b, 0, c)),
                   pl.BlockSpec((1, ts8, LANES), lambda b, c, pt: (b, 0, 0))],
        scratch_shapes=[pltpu.VMEM((2, ID, W), F32), pltpu.SemaphoreType.DMA((2,))])
    return pl.pallas_call(
        functools.partial(_dsa_sample_index_body, pps=pps, IH=IH, ts=ts, ts8=ts8, page=page, n_steps=Bs * nc,
                          layer=layer),
        grid_spec=grid_spec,
        out_shape=[jax.ShapeDtypeStruct((Bs, ts8, nc * W), F32), jax.ShapeDtypeStruct((Bs, ts8, LANES), F32)],
        compiler_params=_cparams(("arbitrary", "arbitrary")), name="dsa_sample_index")(
            page_table.reshape(-1), qi_rows, wi_rows, pool_ik_t, ki_new)


def _topk_threshold_body(sc_ref, scn_ref, thr_ref, *, k_sel):
    n_groups = sc_ref.shape[2] // LANES
    scn = scn_ref[...]
    lo0 = jnp.min(jnp.where(scn > NEG_INF, scn, jnp.inf), axis=-1, keepdims=True)
    hi0 = jnp.max(scn, axis=-1, keepdims=True)
    lo_g = hi_g = sc_ref[:, :, :LANES]
    for j in range(1, n_groups):
        t = sc_ref[:, :, j * LANES:(j + 1) * LANES]
        lo_g = jnp.minimum(lo_g, t)
        hi_g = jnp.maximum(hi_g, t)
    lo0 = jnp.minimum(lo0, jnp.min(lo_g, axis=-1, keepdims=True))
    hi0 = jnp.maximum(hi0, jnp.max(hi_g, axis=-1, keepdims=True))

    def count_ge(mid):
        acc = jnp.where(scn >= mid, 1.0, 0.0)
        for j in range(n_groups):
            acc = acc + jnp.where(sc_ref[:, :, j * LANES:(j + 1) * LANES] >= mid, 1.0, 0.0)
        return jnp.sum(acc, axis=-1, keepdims=True)

    thr = _bisect(lo0, hi0, count_ge, float(k_sel))
    thr_ref[...] = jnp.broadcast_to(thr, thr_ref.shape)


def _topk_threshold(scores, scores_new, k_sel, bb=16):
    Bs, ts8, P = scores.shape
    bb = min(bb, Bs)
    assert Bs % bb == 0
    return pl.pallas_call(
        functools.partial(_topk_threshold_body, k_sel=k_sel), grid=(Bs // bb,),
        in_specs=[pl.BlockSpec((bb, ts8, P), lambda i: (i, 0, 0)),
                  pl.BlockSpec((bb, ts8, LANES), lambda i: (i, 0, 0))],
        out_specs=pl.BlockSpec((bb, ts8, LANES), lambda i: (i, 0, 0)),
        out_shape=jax.ShapeDtypeStruct((Bs, ts8, LANES), F32),
        compiler_params=_cparams(("parallel",)), name="topk_threshold")(scores, scores_new)


def _dsa_sample_attn_body(pt_ref, far_ref, q_ref, sc_ref, scn_ref, thr_ref, k_pool, v_pool, kn_ref, vn_ref,
                          tp_ref, tn_ref, o_ref, kbuf, vbuf, sem, kb_ref, vb_ref, m_ref, l_ref, acc_ref,
                          *, pps, KVH, G, HD, ts, ts8, page, nc, n_steps, layer, scale):
    c = pl.program_id(1)
    R = G * ts8
    W = pps * page
    slot = _fetch_pages(pt_ref, [(k_pool, kbuf, False), (v_pool, vbuf, False)], sem, pps, page, layer, n_steps)

    @pl.when(c == 0)
    def _():
        m_ref[...] = jnp.full(m_ref.shape, NEG_INF, F32)
        l_ref[...] = jnp.zeros(l_ref.shape, F32)
        acc_ref[...] = jnp.zeros(acc_ref.shape, F32)

    for n in range(KVH):
        kb_ref[:, n * HD:(n + 1) * HD] = kbuf[slot, pl.ds(n, W, stride=KVH), :].astype(BF16)
        vb_ref[:, n * HD:(n + 1) * HD] = vbuf[slot, pl.ds(n, W, stride=KVH), :].astype(BF16)
    thr = thr_ref[0][:, :1]
    msk = jnp.concatenate([jnp.where(sc_ref[0] >= thr, 0.0, NEG_INF)] * G, axis=0)
    last = c == nc - 1
    for n in range(KVH):
        qn = q_ref[0, n]
        far = jnp.concatenate([jnp.full((ts8, 1), far_ref[n * G + g], F32) for g in range(G)], axis=0)
        near = tp_ref[n * G:(n + 1) * G].reshape(R, LANES)
        delta = jnp.where(last, near - far, 0.0)
        s = _dot_nt(qn, kb_ref[:, n * HD:(n + 1) * HD]) * scale + far + msk
        s = jnp.concatenate([s[:, :W - LANES], s[:, W - LANES:] + delta], axis=1)
        _flash_update(s, vb_ref[:, n * HD:(n + 1) * HD], m_ref.at[n], l_ref.at[n], acc_ref.at[n])

    @pl.when(last)
    def _():
        mskn = jnp.concatenate([jnp.where(scn_ref[0] >= thr, 0.0, NEG_INF)] * G, axis=0)
        outs = []
        for n in range(KVH):
            qn = q_ref[0, n].astype(F32)
            kn = kn_ref[0][:, n * HD:(n + 1) * HD]
            vn = vn_ref[0][:, n * HD:(n + 1) * HD]
            cols = _new_key_scores([qn], [kn], ts)
            blk = _cols_to_block(cols, R) * scale + tn_ref[n * G:(n + 1) * G].reshape(R, LANES) + mskn
            _flash_update_new(blk, vn, ts, m_ref.at[n], l_ref.at[n], acc_ref.at[n])
            o = acc_ref[n] / l_ref[n]
            outs += [o[g * ts8:g * ts8 + ts] for g in range(G)]
        o_ref[0] = jnp.concatenate(outs, axis=1)


def _dsa_sample_attn(q_rows, scores, scores_new, thr, pool_k, pool_v, layer, page_table, k_new, v_new, tp, tn,
                     far_bias, KVH, G, HD, ts, ts8, scale, pps=32):
    Bs = q_rows.shape[0]
    R = G * ts8
    n_pages = page_table.shape[1]
    page = pool_k.shape[2] // KVH
    pps = min(pps, n_pages)
    nc = n_pages // pps
    W = pps * page
    NH = KVH * G

    hbm = pl.BlockSpec(memory_space=pl.ANY)
    in_specs = [pl.BlockSpec((1, KVH, R, HD), lambda b, c, pt, f: (b, 0, 0, 0)),
                pl.BlockSpec((1, ts8, W), lambda b, c, pt, f: (b, 0, c)),
                pl.BlockSpec((1, ts8, LANES), lambda b, c, pt, f: (b, 0, 0)),
                pl.BlockSpec((1, ts8, LANES), lambda b, c, pt, f: (b, 0, 0)),
                hbm, hbm]
    in_specs += [pl.BlockSpec((1, ts, KVH * HD), lambda b, c, pt, f: (b, 0, 0))] * 2
    in_specs += [pl.BlockSpec((NH, ts8, LANES), lambda b, c, pt, f: (0, 0, 0))] * 2
    grid_spec = pltpu.PrefetchScalarGridSpec(
        num_scalar_prefetch=2, grid=(Bs, nc), in_specs=in_specs,
        out_specs=pl.BlockSpec((1, ts, NH * HD), lambda b, c, pt, f: (b, 0, 0)),
        scratch_shapes=[pltpu.VMEM((2, W * KVH, HD), F32), pltpu.VMEM((2, W * KVH, HD), F32),
                        pltpu.SemaphoreType.DMA((2,)),
                        pltpu.VMEM((W, KVH * HD), BF16), pltpu.VMEM((W, KVH * HD), BF16),
                        pltpu.VMEM((KVH, R, 1), F32), pltpu.VMEM((KVH, R, 1), F32), pltpu.VMEM((KVH, R, HD), F32)])
    return pl.pallas_call(
        functools.partial(_dsa_sample_attn_body, pps=pps, KVH=KVH, G=G, HD=HD, ts=ts, ts8=ts8, page=page,
                          nc=nc, n_steps=Bs * nc, layer=layer, scale=scale),
        grid_spec=grid_spec, out_shape=jax.ShapeDtypeStruct((Bs, ts, NH * HD), F32),
        compiler_params=_cparams(("arbitrary", "arbitrary")), name="dsa_sample_attn")(
            page_table.reshape(-1), far_bias, q_rows, scores, scores_new, thr, pool_k, pool_v,
            k_new, v_new, tp, tn)


def _rot_tables(pos, half, group, extra=None):
    lane = jnp.arange(LANES)
    g = lane % group
    rotary = g < 2 * half
    freq = ROPE_THETA ** (-jnp.arange(half, dtype=F32) / half)
    ang = pos.astype(F32)[:, None] * freq[None, :]
    cos = jnp.cos(ang)[:, g % half]
    sin = jnp.sin(ang)[:, g % half]
    c_tab = jnp.where(rotary[None, :], cos, 1.0)
    s_tab = jnp.where(rotary[None, :], jnp.where((g < half)[None, :], -sin, sin), 0.0)
    if extra is not None:
        a, b, val = extra
        c_tab = jnp.where(((lane >= a) & (lane < b))[None, :], val, c_tab)
    return c_tab.astype(F32), s_tab.astype(F32)


def kernel(x_prompt, x_sample, cache_mla_ckv, cache_mla_kpe, cache_dsa_k, cache_dsa_v, cache_dsa_idx_k,
           page_table, c_prompt, c_sample, w_ada, b_ada, ln_g, ln_b,
           mla_w_in, mla_q_norm, mla_kv_norm, mla_w_uq, mla_w_uk, mla_w_uv, mla_w_o,
           dsa_w_in, dsa_w_o, rel_bias, ffn_w_gu, ffn_w_down):
    B, T, D = x_prompt.shape
    Bs, ts, _ = x_sample.shape
    depth = w_ada.shape[0]
    alpha = (2 * depth) ** 0.25
    n_pages = page_table.shape[1]
    page = cache_mla_ckv.shape[2]
    past = n_pages * page
    Mp, Ms = B * T, Bs * ts
    ts8 = 8 * (-(-ts // 8))

    ql = mla_q_norm.shape[1]
    kvl = cache_mla_ckv.shape[-1]
    rope = cache_mla_kpe.shape[-1]
    H = mla_w_uk.shape[2]
    nope = mla_w_uk.shape[3]
    vh = mla_w_uv.shape[3]
    mla_scale = (nope + rope) ** -0.5
    KVH, HD = cache_dsa_k.shape[-2:]
    ID = cache_dsa_idx_k.shape[-1]
    NH = dsa_w_o.shape[1] // HD
    G = NH // KVH
    dq, dkv = NH * HD, KVH * HD
    IH = (dsa_w_in.shape[2] - dq - 2 * dkv - ID) // (ID + 1)
    dsa_scale = HD ** -0.5
    idx_wscale = (IH ** -0.5) * (ID ** -0.5)
    tq_dsa = min(256, T)

    pos_p = jnp.arange(T, dtype=jnp.int32)
    pos_s = jnp.tile(past + jnp.arange(ts, dtype=jnp.int32), Bs)
    tabs = {
        "mla": (_rot_tables(pos_p, rope // 2, rope), _rot_tables(pos_s, rope // 2, rope)),
        "idx": (_rot_tables(pos_p, IDX_ROPE // 2, ID), _rot_tables(pos_s, IDX_ROPE // 2, ID)),
        "tail": (_rot_tables(pos_p, IDX_ROPE // 2, LANES, (ID, ID + IH, idx_wscale)),
                 _rot_tables(pos_s, IDX_ROPE // 2, LANES, (ID, ID + IH, idx_wscale))),
    }

    c_all = jnp.concatenate([c_prompt, c_sample], axis=0)
    nc_rows = c_all.shape[0]
    c_all = jnp.pad(c_all, ((0, -nc_rows % 8), (0, 0)))
    mods = [_mm(c_all, w_ada, layer=i, bias=b_ada[i].reshape(1, -1), act="silu", tm=c_all.shape[0], tn=1024,
                name="ada_mod") for i in range(depth)]

    pool_kpe_t = jnp.swapaxes(cache_mla_kpe, 2, 3)
    pool_ik_t = jnp.swapaxes(cache_dsa_idx_k, 2, 3)
    pool_k = cache_dsa_k.reshape(cache_dsa_k.shape[0], -1, page * cache_dsa_k.shape[3], cache_dsa_k.shape[4])
    pool_v = cache_dsa_v.reshape(pool_k.shape)

    def mod_vectors(i):
        m = mods[i]
        mp = [m[:B, j * D:(j + 1) * D].reshape(B, 1, D) for j in range(6)]
        ms = [jnp.repeat(m[B:B + Bs, j * D:(j + 1) * D], ts, axis=0) for j in range(6)]
        return mp, ms

    groups = [dict(y=x_prompt.reshape(Mp, D), rpg=T, nb=B, tb=T, g=0),
              dict(y=x_sample.reshape(Ms, D), rpg=None, nb=1, tb=Ms, g=1)]
    modv = [mod_vectors(i) for i in range(depth)]
    for grp in groups:
        sh1, sc1 = modv[0][grp["g"]][0], modv[0][grp["g"]][1]
        grp["u"] = _modulate(grp["y"], sh1, sc1, grp["rpg"])

    t0 = t1 = tp = tn = far_bias = None
    if depth > 1:
        t0, t1, tp, tn = _bias_tables(rel_bias, tq_dsa, ts8)
        far_bias = rel_bias[-1]

    outs = {k: [[], []] for k in ("ckv", "kpe", "k", "v", "ik")}
    for i in range(depth):
        j = i // 2
        if i % 2 == 0:
            w_in_pad = jnp.pad(mla_w_in[j], ((0, 0), (0, LANES - rope))).astype(BF16)
            w_uq = mla_w_uq[j].reshape(ql, H, nope + rope)
            w_nope = w_uq[:, :, :nope].reshape(ql, H * nope).astype(BF16)
            w_pe = w_uq[:, :, nope:].reshape(ql, H * rope).astype(BF16)
            w_uk_t = jnp.transpose(mla_w_uk[j], (1, 2, 0)).astype(BF16)
            w_uv = mla_w_uv[j].reshape(kvl, H * vh).astype(BF16)
            w_o = mla_w_o[j].astype(BF16)
        else:
            w_in = dsa_w_in[j].astype(BF16)
            w_tail = jnp.pad(dsa_w_in[j][:, dq + 2 * dkv + IH * ID:], ((0, 0), (0, LANES - ID - IH))).astype(BF16)
            w_o = dsa_w_o[j].astype(BF16)
        w_down = ffn_w_down[i].astype(BF16)

        for grp in groups:
            g = grp["g"]
            u = grp["u"]
            mv = modv[i][g]
            if i % 2 == 0:
                ct, st = tabs["mla"][g]
                cq, ckv, ckv_b, kpe, kpe_b = _mla_in(u, w_in_pad, mla_q_norm[j], mla_kv_norm[j], ct, st,
                                                     ql, kvl, rope)
                q_lat, q_pe = _mla_q(cq, w_nope, w_pe, w_uk_t, ct, st, grp["nb"], grp["tb"], H, nope, rope, kvl)
                if g == 0:
                    o = _mla_prompt_attn(q_lat, q_pe, ckv_b.reshape(B, T, kvl), kpe_b.reshape(B, T, rope),
                                         w_uv, mla_scale)
                else:
                    def rows(a):
                        n = a.shape[-1]
                        return a.reshape(H, Bs, ts, n).transpose(1, 0, 2, 3).reshape(Bs, H * ts, n)
                    o = _mla_sample_attn(rows(q_lat), rows(q_pe), cache_mla_ckv, pool_kpe_t, j, page_table,
                                         ckv.reshape(Bs, ts, kvl), kpe.reshape(Bs, ts, rope), w_uv, mla_scale, H)
                    o = o.reshape(Ms, H * vh).astype(BF16)
                outs["ckv"][g].append(ckv)
                outs["kpe"][g].append(kpe)
            else:
                q = _mm(u, w_in, n_cols=dq, col0=0, out_dtype=BF16, name="dsa_q")
                k, k_b = _mm(u, w_in, n_cols=dkv, col0=dq, second_dtype=BF16, name="dsa_k")
                v, v_b = _mm(u, w_in, n_cols=dkv, col0=dq + dkv, second_dtype=BF16, name="dsa_v")
                ci, si = tabs["idx"][g]
                qi = _mm(u, w_in, n_cols=IH * ID, col0=dq + 2 * dkv, out_dtype=BF16,
                         rot=(ci, si, IDX_ROPE // 2), name="dsa_qi")
                ctl, stl = tabs["tail"][g]
                tail = _mm(u, w_tail, rot=(ctl, stl, IDX_ROPE // 2), name="dsa_tail")
                ki = tail[:, :ID]
                wi = tail[:, ID:ID + IH]
                if g == 0:
                    ki_t = jnp.swapaxes(ki.astype(BF16).reshape(B, T // tq_dsa, tq_dsa, ID), 2, 3)
                    o = _dsa_prompt(q, qi, wi, k_b, v_b, ki_t, t0, t1, far_bias, B, T,
                                    min(TOPK_MAX, T // 4), IH, ID, KVH, G, HD, dsa_scale, tq_dsa)
                else:
                    pad_t = ((0, 0), (0, 0), (0, ts8 - ts), (0, 0))
                    qi_rows = jnp.pad(qi.reshape(Bs, ts, IH, ID).transpose(0, 2, 1, 3), pad_t
                                      ).reshape(Bs, IH * ts8, ID)
                    wi_rows = jnp.pad(wi.reshape(Bs, ts, IH).transpose(0, 2, 1), ((0, 0), (0, 0), (0, ts8 - ts))
                                      ).reshape(Bs, IH * ts8, 1)
                    sc, sc_new = _dsa_sample_index(qi_rows, wi_rows, pool_ik_t, j, page_table,
                                                   ki.reshape(Bs, ts, ID), IH, ts, ts8)
                    thr = _topk_threshold(sc, sc_new, min(TOPK_MAX, (past + ts) // 4))
                    q_rows = jnp.pad(q.reshape(Bs, ts, NH, HD).transpose(0, 2, 1, 3), pad_t
                                     ).reshape(Bs, KVH, G * ts8, HD)
                    o = _dsa_sample_attn(q_rows, sc, sc_new, thr, pool_k, pool_v, j,
                                         page_table, k.reshape(Bs, ts, dkv), v.reshape(Bs, ts, dkv), tp, tn, far_bias,
                                         KVH, G, HD, ts, ts8, dsa_scale)
                    o = o.reshape(Ms, NH * HD).astype(BF16)
                outs["k"][g].append(k)
                outs["v"][g].append(v)
                outs["ik"][g].append(ki)
            y1, u2 = _mm_postnorm(o, w_o, grp["y"], mv[2], ln_g[i, 0], ln_b[i, 0], (mv[3], mv[4]), grp["rpg"],
                                  alpha, name="attn_out_postnorm")
            hmid = _swiglu_up(u2, ffn_w_gu, i)
            nxt = None
            if i + 1 < depth:
                nmv = modv[i + 1][g]
                nxt = (nmv[0], nmv[1])
            grp["y"], grp["u"] = _mm_postnorm(hmid, w_down, y1, mv[5], ln_g[i, 1], ln_b[i, 1], nxt, grp["rpg"],
                                              alpha, name="ffn_down_postnorm")

    def stack(key, g, shape):
        return jnp.stack([a.reshape(shape) for a in outs[key][g]])

    yp = groups[0]["y"].reshape(B, T, D)
    ys = groups[1]["y"].reshape(Bs, ts, D)
    return (yp, ys,
            stack("ckv", 0, (B, T, kvl)), stack("kpe", 0, (B, T, rope)),
            stack("k", 0, (B, T, KVH, HD)), stack("v", 0, (B, T, KVH, HD)), stack("ik", 0, (B, T, ID)),
            stack("ckv", 1, (Bs, ts, kvl)), stack("kpe", 1, (Bs, ts, rope)),
            stack("k", 1, (Bs, ts, KVH, HD)), stack("v", 1, (Bs, ts, KVH, HD)), stack("ik", 1, (Bs, ts, ID)))
```

```python
import functools
import math

import jax
import jax.numpy as jnp
from jax import lax
from jax.experimental import pallas as pl
from jax.experimental.pallas import tpu as pltpu

F32 = jnp.float32
BF16 = jnp.bfloat16
NEG_INF = float("-inf")

LANES = 128
ROPE_THETA = 10000.0
IDX_ROPE = 32
TOPK_MAX = 256
REL_MAX_DIST = 128
LN_EPS = 1e-5
RMS_EPS = 1e-6
BISECT_ITERS = 32
VMEM_LIMIT_MB = 56
POSTNORM_VMEM_BUDGET_MB = 46


def _cparams(sem, vmem_mb=VMEM_LIMIT_MB):
    return pltpu.CompilerParams(dimension_semantics=sem, vmem_limit_bytes=vmem_mb * 1024 * 1024)


def _dot(a, b):
    return jnp.dot(a, b, preferred_element_type=F32)


def _dot_nt(a, b):
    return lax.dot_general(a, b, (((1,), (1,)), ((), ())), preferred_element_type=F32)


def _rot(v, c, s, half):
    lane = lax.broadcasted_iota(jnp.int32, v.shape, 1)
    partner = jnp.where((lane % (2 * half)) < half,
                        pltpu.roll(v, LANES - half, 1), pltpu.roll(v, half, 1))
    return v * c + partner * s


def _mm_body(*refs, nk, act, rot_half, has_bias, second):
    it = iter(refs)
    x_ref = next(it)
    w_ref = next(it)
    b_ref = next(it) if has_bias else None
    c_ref = next(it) if rot_half else None
    s_ref = next(it) if rot_half else None
    o_ref = next(it)
    o2_ref = next(it) if second else None
    acc_ref = next(it) if nk > 1 else None

    x = x_ref[...]
    if act == "silu":
        xf = x.astype(F32)
        x = xf * jax.nn.sigmoid(xf)
    part = _dot(x.astype(BF16), w_ref[...].astype(BF16))

    def finish(acc):
        if has_bias:
            acc = acc + b_ref[...]
        if rot_half:
            c = c_ref[...]
            s = s_ref[...]
            for j in range(acc.shape[1] // LANES):
                sl = slice(j * LANES, (j + 1) * LANES)
                r = _rot(acc[:, sl], c, s, rot_half)
                o_ref[:, sl] = r.astype(o_ref.dtype)
                if second:
                    o2_ref[:, sl] = r.astype(o2_ref.dtype)
        else:
            o_ref[...] = acc.astype(o_ref.dtype)
            if second:
                o2_ref[...] = acc.astype(o2_ref.dtype)

    if nk == 1:
        finish(part)
    else:
        k = pl.program_id(2)

        @pl.when(k == 0)
        def _():
            acc_ref[...] = part

        @pl.when(k > 0)
        def _():
            acc_ref[...] += part

        @pl.when(k == nk - 1)
        def _():
            finish(acc_ref[...])


def _mm(x, w, *, layer=None, n_cols=None, col0=0, tm=1024, tn=512, tk=None, out_dtype=F32, second_dtype=None,
        act=None, bias=None, rot=None, name="mm"):
    M, K = x.shape
    N = n_cols if n_cols is not None else w.shape[-1]
    tm = min(tm, M) if rot is None else min(tm, M, rot[0].shape[0])
    tn = min(tn, N)
    while N % tn or col0 % tn:
        tn //= 2
    tk = K if tk is None else min(tk, K)
    assert M % tm == 0 and K % tk == 0 and tn % LANES == 0
    nk = K // tk
    jb = col0 // tn
    if layer is None:
        w_spec = pl.BlockSpec((tk, tn), lambda i, j, k: (k, j + jb))
    else:
        w_spec = pl.BlockSpec((None, tk, tn), lambda i, j, k: (layer, k, j + jb))
    in_specs = [pl.BlockSpec((tm, tk), lambda i, j, k: (i, k)), w_spec]
    args = [x, w]
    if bias is not None:
        in_specs.append(pl.BlockSpec((1, tn), lambda i, j, k: (0, j)))
        args.append(bias)
    rot_half = 0
    if rot is not None:
        c_tab, s_tab, rot_half = rot
        nr = c_tab.shape[0] // tm
        assert c_tab.shape[0] % tm == 0
        for t in (c_tab, s_tab):
            in_specs.append(pl.BlockSpec((tm, LANES), lambda i, j, k: (i % nr, 0)))
            args.append(t)
    out_shape = [jax.ShapeDtypeStruct((M, N), out_dtype)]
    out_specs = [pl.BlockSpec((tm, tn), lambda i, j, k: (i, j))]
    if second_dtype is not None:
        out_shape.append(jax.ShapeDtypeStruct((M, N), second_dtype))
        out_specs.append(pl.BlockSpec((tm, tn), lambda i, j, k: (i, j)))
    scratch = [pltpu.VMEM((tm, tn), F32)] if nk > 1 else []
    res = pl.pallas_call(
        functools.partial(_mm_body, nk=nk, act=act, rot_half=rot_half, has_bias=bias is not None,
                          second=second_dtype is not None),
        grid=(M // tm, N // tn, nk),
        in_specs=in_specs, out_specs=out_specs, out_shape=out_shape, scratch_shapes=scratch,
        compiler_params=_cparams(("parallel", "parallel", "arbitrary")), name=name)(*args)
    return res if second_dtype is not None else res[0]


def _modulate_body(y_ref, sh_ref, sc_ref, u_ref):
    u_ref[...] = (y_ref[...] * (1.0 + sc_ref[...]) + sh_ref[...]).astype(u_ref.dtype)


def _mod_spec(tm, n, rows_per_group):
    if rows_per_group is None:
        return pl.BlockSpec((tm, n), lambda i, *_: (i, 0))
    per = rows_per_group // tm
    return pl.BlockSpec((None, 1, n), lambda i, *_: (i // per, 0, 0))


def _modulate(y, sh, sc, rows_per_group, tm=512):
    M, D = y.shape
    tm = min(tm, M)
    return pl.pallas_call(
        _modulate_body, grid=(M // tm,),
        in_specs=[pl.BlockSpec((tm, D), lambda i: (i, 0)), _mod_spec(tm, D, rows_per_group),
                  _mod_spec(tm, D, rows_per_group)],
        out_specs=pl.BlockSpec((tm, D), lambda i: (i, 0)),
        out_shape=jax.ShapeDtypeStruct((M, D), BF16),
        compiler_params=_cparams(("parallel",)), name="modulate")(y, sh, sc)


def _mm_postnorm_body(*refs, alpha, with_mod):
    if with_mod:
        x_ref, w_ref, y_ref, g_ref, lg_ref, lb_ref, sh_ref, sc_ref, yo_ref, u_ref = refs
    else:
        x_ref, w_ref, y_ref, g_ref, lg_ref, lb_ref, yo_ref = refs
    h = _dot(x_ref[...].astype(BF16), w_ref[...].astype(BF16))
    z = alpha * y_ref[...] + g_ref[...] * h
    mu = jnp.mean(z, axis=-1, keepdims=True)
    zc = z - mu
    var = jnp.mean(zc * zc, axis=-1, keepdims=True)
    yn = zc * lax.rsqrt(var + LN_EPS) * lg_ref[...] + lb_ref[...]
    yo_ref[...] = yn
    if with_mod:
        u_ref[...] = (yn * (1.0 + sc_ref[...]) + sh_ref[...]).astype(u_ref.dtype)


def _mm_postnorm(x, w, y, gate, ln_g, ln_b, mod, rows_per_group, alpha, name="mm_postnorm"):
    M, K = x.shape
    N = w.shape[1]

    def vmem_bytes(tm):
        return K * N * 2 + 2 * tm * K * 2 + tm * N * (2 * 4 + 2 * 4 + 2 * 2 + 4 + 4)

    tm = min(512, M)
    while tm > 8 and vmem_bytes(tm) > POSTNORM_VMEM_BUDGET_MB * 1024 * 1024:
        tm //= 2
    assert M % tm == 0
    row = pl.BlockSpec((tm, N), lambda i: (i, 0))
    vec = pl.BlockSpec((1, N), lambda i: (0, 0))
    in_specs = [pl.BlockSpec((tm, K), lambda i: (i, 0)), pl.BlockSpec((K, N), lambda i: (0, 0)),
                row, _mod_spec(tm, N, rows_per_group), vec, vec]
    args = [x, w, y, gate, ln_g.reshape(1, N), ln_b.reshape(1, N)]
    out_shape = [jax.ShapeDtypeStruct((M, N), F32)]
    out_specs = [row]
    if mod is not None:
        in_specs += [_mod_spec(tm, N, rows_per_group)] * 2
        args += list(mod)
        out_shape.append(jax.ShapeDtypeStruct((M, N), BF16))
        out_specs.append(row)
    res = pl.pallas_call(
        functools.partial(_mm_postnorm_body, alpha=alpha, with_mod=mod is not None),
        grid=(M // tm,), in_specs=in_specs, out_specs=out_specs, out_shape=out_shape,
        compiler_params=_cparams(("parallel",)), name=name)(*args)
    return (res[0], res[1]) if mod is not None else (res[0], None)


def _swiglu_body(x_ref, wg_ref, wu_ref, o_ref):
    x = x_ref[...]
    g = _dot(x, wg_ref[...].astype(BF16))
    u = _dot(x, wu_ref[...].astype(BF16))
    o_ref[...] = (g * jax.nn.sigmoid(g) * u).astype(o_ref.dtype)


def _swiglu_up(x, w_gu, layer, tm=1024, tn=512):
    M, K = x.shape
    F = w_gu.shape[2] // 2
    tm = min(tm, M)
    tn = min(tn, F)
    while F % tn:
        tn //= 2
    nj = F // tn
    return pl.pallas_call(
        _swiglu_body, grid=(M // tm, nj),
        in_specs=[pl.BlockSpec((tm, K), lambda i, j: (i, 0)),
                  pl.BlockSpec((None, K, tn), lambda i, j: (layer, 0, j)),
                  pl.BlockSpec((None, K, tn), lambda i, j: (layer, 0, j + nj))],
        out_specs=pl.BlockSpec((tm, tn), lambda i, j: (i, j)),
        out_shape=jax.ShapeDtypeStruct((M, F), BF16),
        compiler_params=_cparams(("parallel", "parallel")), name="swiglu_up")(x, w_gu, w_gu)


def _mla_in_body(x_ref, w_ref, qg_ref, kvg_ref, c_ref, s_ref, cq_ref, ckv_ref, ckvb_ref, kpe_ref, kpeb_ref,
                 *, ql, kvl, rope):
    acc = _dot(x_ref[...], w_ref[...])

    def rms(v, g):
        return v * lax.rsqrt(jnp.mean(v * v, axis=-1, keepdims=True) + RMS_EPS) * g

    cq_ref[...] = rms(acc[:, :ql], qg_ref[...]).astype(cq_ref.dtype)
    ckv = rms(acc[:, ql:ql + kvl], kvg_ref[...])
    ckv_ref[...] = ckv
    ckvb_ref[...] = ckv.astype(ckvb_ref.dtype)
    kpe = _rot(acc[:, ql + kvl:], c_ref[...], s_ref[...], rope // 2)[:, :rope]
    kpe_ref[...] = kpe
    kpeb_ref[...] = kpe.astype(kpeb_ref.dtype)


def _mla_in(u, w_pad, q_g, kv_g, c_tab, s_tab, ql, kvl, rope, tm=512):
    M, K = u.shape
    N = w_pad.shape[1]
    tm = min(tm, M)
    nr = c_tab.shape[0] // tm
    tab = pl.BlockSpec((tm, LANES), lambda i: (i % nr, 0))

    def row(n):
        return pl.BlockSpec((tm, n), lambda i: (i, 0))

    return pl.pallas_call(
        functools.partial(_mla_in_body, ql=ql, kvl=kvl, rope=rope), grid=(M // tm,),
        in_specs=[row(K), pl.BlockSpec((K, N), lambda i: (0, 0)),
                  pl.BlockSpec((1, ql), lambda i: (0, 0)), pl.BlockSpec((1, kvl), lambda i: (0, 0)), tab, tab],
        out_specs=[row(ql), row(kvl), row(kvl), row(rope), row(rope)],
        out_shape=[jax.ShapeDtypeStruct((M, ql), BF16), jax.ShapeDtypeStruct((M, kvl), F32),
                   jax.ShapeDtypeStruct((M, kvl), BF16), jax.ShapeDtypeStruct((M, rope), F32),
                   jax.ShapeDtypeStruct((M, rope), BF16)],
        compiler_params=_cparams(("parallel",)), name="mla_in")(
            u, w_pad, q_g.reshape(1, ql), kv_g.reshape(1, kvl), c_tab, s_tab)


def _mla_q_body(cq_ref, wn_ref, wp_ref, wuk_ref, c_ref, s_ref, ql_ref, qp_ref, *, H, nope, rope):
    cq = cq_ref[...]
    qn = _dot(cq, wn_ref[...]).astype(BF16)
    for h in range(H):
        ql_ref[0, h] = _dot(qn[:, h * nope:(h + 1) * nope], wuk_ref[h]).astype(ql_ref.dtype)
    qp = _dot(cq, wp_ref[...])
    c = c_ref[...]
    s = s_ref[...]
    per = LANES // rope
    for j in range(H // per):
        r = _rot(qp[:, j * LANES:(j + 1) * LANES], c, s, rope // 2)
        for e in range(per):
            qp_ref[0, j * per + e] = r[:, e * rope:(e + 1) * rope].astype(qp_ref.dtype)


def _mla_q(cq, w_nope, w_pe, w_uk_t, c_tab, s_tab, nb, tb, H, nope, rope, kvl, tm=512):
    M, ql = cq.shape
    tm = min(tm, tb)
    per = tb // tm
    nr = c_tab.shape[0] // tm
    tab = pl.BlockSpec((tm, LANES), lambda i: (i % nr, 0))
    full2 = lambda a: pl.BlockSpec(a.shape, lambda i: (0, 0))
    return pl.pallas_call(
        functools.partial(_mla_q_body, H=H, nope=nope, rope=rope), grid=(M // tm,),
        in_specs=[pl.BlockSpec((tm, ql), lambda i: (i, 0)), full2(w_nope), full2(w_pe),
                  pl.BlockSpec(w_uk_t.shape, lambda i: (0, 0, 0)), tab, tab],
        out_specs=[pl.BlockSpec((1, H, tm, kvl), lambda i: (i // per, 0, i % per, 0)),
                   pl.BlockSpec((1, H, tm, rope), lambda i: (i // per, 0, i % per, 0))],
        out_shape=[jax.ShapeDtypeStruct((nb, H, tb, kvl), BF16), jax.ShapeDtypeStruct((nb, H, tb, rope), BF16)],
        compiler_params=_cparams(("parallel",)), name="mla_q")(cq, w_nope, w_pe, w_uk_t, c_tab, s_tab)


def _flash_update(s, v, m_ref, l_ref, acc_ref):
    m_old = m_ref[...]
    m_new = jnp.maximum(m_old, jnp.max(s, axis=-1, keepdims=True))
    m_safe = jnp.where(m_new == NEG_INF, 0.0, m_new)
    a = jnp.exp(m_old - m_safe)
    p = jnp.exp(s - m_safe)
    l_ref[...] = a * l_ref[...] + jnp.sum(p, axis=-1, keepdims=True)
    acc_ref[...] = a * acc_ref[...] + _dot(p.astype(BF16), v)
    m_ref[...] = m_new


def _lane_fold(x, op):
    r = x[:, :LANES]
    for j in range(1, x.shape[1] // LANES):
        r = op(r, x[:, j * LANES:(j + 1) * LANES])
    return r


def _mla_prompt_body(ql_ref, qp_ref, ckv_ref, kpe_ref, wuv_ref, o_ref, s_ref, mx_ref, lp_ref, acc_ref,
                     *, H, tq, tk, scale, vh):
    i = pl.program_id(1)
    R = H * tq
    n_blk = lax.div(i * tq + tq - 1, tk) + 1

    def score(c, masked):
        ql = ql_ref[0].reshape(R, ql_ref.shape[-1])
        qp = qp_ref[0].reshape(R, qp_ref.shape[-1])
        start = pl.multiple_of(c * tk, tk)
        s = (_dot_nt(ql, ckv_ref[0, pl.ds(start, tk), :]) + _dot_nt(qp, kpe_ref[0, pl.ds(start, tk), :])) * scale
        if masked:
            tok = i * tq + (lax.broadcasted_iota(jnp.int32, s.shape, 0) % tq)
            key = c * tk + lax.broadcasted_iota(jnp.int32, s.shape, 1)
            s = jnp.where(key <= tok, s, NEG_INF)
        s_ref[c] = s
        return _lane_fold(s, jnp.maximum)

    mx_ref[...] = jnp.full(mx_ref.shape, NEG_INF, F32)

    def far(c, _):
        mx_ref[...] = jnp.maximum(mx_ref[...], score(c, False))
        return 0

    lax.fori_loop(0, n_blk - 1, far, 0)
    mx = jnp.maximum(mx_ref[...], score(n_blk - 1, True))
    m = jnp.max(mx, axis=-1, keepdims=True)

    lp_ref[...] = jnp.zeros(lp_ref.shape, F32)
    acc_ref[...] = jnp.zeros(acc_ref.shape, F32)

    def pv(c, _):
        p = jnp.exp(s_ref[c] - m)
        lp_ref[...] += _lane_fold(p, jnp.add)
        acc_ref[...] += _dot(p.astype(BF16), ckv_ref[0, pl.ds(pl.multiple_of(c * tk, tk), tk), :])
        return 0

    lax.fori_loop(0, n_blk, pv, 0)
    o = (acc_ref[...] / jnp.sum(lp_ref[...], axis=-1, keepdims=True)).astype(BF16)
    for h in range(H):
        o_ref[:, h * vh:(h + 1) * vh] = _dot(o[h * tq:(h + 1) * tq], wuv_ref[:, h * vh:(h + 1) * vh]
                                             ).astype(o_ref.dtype)


def _mla_prompt_attn(q_lat, q_pe, ckv_b, kpe_b, w_uv, scale, tq=128, tk=256):
    B, H, T, C = q_lat.shape
    R = q_pe.shape[-1]
    tq = min(tq, T)
    tk = min(tk, T)
    nq, nk = T // tq, T // tk
    vh = w_uv.shape[1] // H
    return pl.pallas_call(
        functools.partial(_mla_prompt_body, H=H, tq=tq, tk=tk, scale=scale, vh=vh),
        grid=(B, nq),
        in_specs=[pl.BlockSpec((1, H, tq, C), lambda b, i: (b, 0, i, 0)),
                  pl.BlockSpec((1, H, tq, R), lambda b, i: (b, 0, i, 0)),
                  pl.BlockSpec((1, T, C), lambda b, i: (b, 0, 0)), pl.BlockSpec((1, T, R), lambda b, i: (b, 0, 0)),
                  pl.BlockSpec(w_uv.shape, lambda b, i: (0, 0))],
        out_specs=pl.BlockSpec((tq, H * vh), lambda b, i: (b * nq + i, 0)),
        out_shape=jax.ShapeDtypeStruct((B * T, H * vh), BF16),
        scratch_shapes=[pltpu.VMEM((nk, H * tq, tk), F32), pltpu.VMEM((H * tq, LANES), F32),
                        pltpu.VMEM((H * tq, LANES), F32), pltpu.VMEM((H * tq, C), F32)],
        compiler_params=_cparams(("parallel", "arbitrary")), name="mla_prompt_attn")(
            q_lat, q_pe, ckv_b, kpe_b, w_uv)


def _new_key_scores(q_parts, k_parts, n_new):
    cols = []
    for t in range(n_new):
        acc = None
        for q, k in zip(q_parts, k_parts):
            d = jnp.sum(q * k[t:t + 1, :], axis=-1, keepdims=True)
            acc = d if acc is None else acc + d
        cols.append(acc)
    return cols


def _cols_to_block(cols, rows):
    lane = lax.broadcasted_iota(jnp.int32, (rows, LANES), 1)
    blk = jnp.full((rows, LANES), NEG_INF, F32)
    for t, c in enumerate(cols):
        blk = jnp.where(lane == t, c, blk)
    return blk


def _flash_update_new(s_blk, v_new, n_new, m_ref, l_ref, acc_ref):
    m_old = m_ref[...]
    m_new = jnp.maximum(m_old, jnp.max(s_blk, axis=-1, keepdims=True))
    m_safe = jnp.where(m_new == NEG_INF, 0.0, m_new)
    a = jnp.exp(m_old - m_safe)
    p = jnp.exp(s_blk - m_safe)
    l_ref[...] = a * l_ref[...] + jnp.sum(p, axis=-1, keepdims=True)
    acc = a * acc_ref[...]
    for t in range(n_new):
        acc = acc + p[:, t:t + 1] * v_new[t:t + 1, :]
    acc_ref[...] = acc
    m_ref[...] = m_new


def _page_copies(pt_ref, step, slot, streams, sem_ref, pps, page, layer):
    copies = []
    for r in range(pps):
        pid = pt_ref[step * pps + r]
        for pool, buf, lanes in streams:
            if lanes:
                dst = buf.at[slot, :, pl.ds(r * page, page)]
            else:
                rows = pool.shape[2]
                dst = buf.at[slot, pl.ds(r * rows, rows)]
            copies.append(pltpu.make_async_copy(pool.at[layer, pid], dst, sem_ref.at[slot]))
    return copies


def _fetch_pages(pt_ref, streams, sem_ref, pps, page, layer, n_steps):
    s = pl.program_id(0) * pl.num_programs(1) + pl.program_id(1)
    slot = s % 2

    @pl.when(s == 0)
    def _():
        for cp in _page_copies(pt_ref, 0, 0, streams, sem_ref, pps, page, layer):
            cp.start()

    @pl.when(s + 1 < n_steps)
    def _():
        for cp in _page_copies(pt_ref, s + 1, 1 - slot, streams, sem_ref, pps, page, layer):
            cp.start()

    for cp in _page_copies(pt_ref, s, slot, streams, sem_ref, pps, page, layer):
        cp.wait()
    return slot


def _mla_sample_body(pt_ref, ql_ref, qp_ref, ckv_pool, kpe_pool, cn_ref, kn_ref, wuv_ref, o_ref,
                     cbuf, pbuf, sem, kc_ref, kp_ref, m_ref, l_ref, acc_ref,
                     *, pps, H, ts, page, nc, n_steps, layer, scale, vh):
    c = pl.program_id(1)
    R = H * ts
    slot = _fetch_pages(pt_ref, [(ckv_pool, cbuf, False), (kpe_pool, pbuf, True)], sem, pps, page, layer, n_steps)

    @pl.when(c == 0)
    def _():
        m_ref[...] = jnp.full(m_ref.shape, NEG_INF, F32)
        l_ref[...] = jnp.zeros(l_ref.shape, F32)
        acc_ref[...] = jnp.zeros(acc_ref.shape, F32)

    kc_ref[...] = cbuf[slot].astype(BF16)
    kp_ref[...] = pbuf[slot].astype(BF16)
    ql = ql_ref[0]
    qp = qp_ref[0]
    ckv = kc_ref[...]
    s = (_dot_nt(ql, ckv) + _dot(qp, kp_ref[...])) * scale
    _flash_update(s, ckv, m_ref, l_ref, acc_ref)

    @pl.when(c == nc - 1)
    def _():
        cn = cn_ref[0]
        cols = _new_key_scores([ql.astype(F32), qp.astype(F32)], [cn, kn_ref[0]], ts)
        blk = _cols_to_block(cols, R) * scale
        qt = lax.broadcasted_iota(jnp.int32, (R, LANES), 0) % ts
        lane = lax.broadcasted_iota(jnp.int32, (R, LANES), 1)
        blk = jnp.where(lane <= qt, blk, NEG_INF)
        _flash_update_new(blk, cn, ts, m_ref, l_ref, acc_ref)
        o = (acc_ref[...] / l_ref[...]).astype(BF16)
        full = _dot(o, wuv_ref[...])
        o_ref[0] = jnp.concatenate([full[h * ts:(h + 1) * ts, h * vh:(h + 1) * vh] for h in range(H)], axis=1)


def _mla_sample_attn(q_lat, q_pe, pool_ckv, pool_kpe_t, layer, page_table, ckv_new, kpe_new, w_uv, scale, H,
                     pps=32):
    Bs, R, C = q_lat.shape
    Rr = q_pe.shape[-1]
    ts = R // H
    n_pages = page_table.shape[1]
    page = pool_ckv.shape[2]
    pps = min(pps, n_pages)
    nc = n_pages // pps
    vh = w_uv.shape[1] // H

    W = pps * page
    hbm = pl.BlockSpec(memory_space=pl.ANY)
    in_specs = [pl.BlockSpec((1, R, C), lambda b, c, pt: (b, 0, 0)),
                pl.BlockSpec((1, R, Rr), lambda b, c, pt: (b, 0, 0)),
                hbm, hbm,
                pl.BlockSpec((1, ts, C), lambda b, c, pt: (b, 0, 0)),
                pl.BlockSpec((1, ts, Rr), lambda b, c, pt: (b, 0, 0)),
                pl.BlockSpec(w_uv.shape, lambda b, c, pt: (0, 0))]
    grid_spec = pltpu.PrefetchScalarGridSpec(
        num_scalar_prefetch=1, grid=(Bs, nc), in_specs=in_specs,
        out_specs=pl.BlockSpec((1, ts, H * vh), lambda b, c, pt: (b, 0, 0)),
        scratch_shapes=[pltpu.VMEM((2, W, C), F32), pltpu.VMEM((2, Rr, W), F32), pltpu.SemaphoreType.DMA((2,)),
                        pltpu.VMEM((W, C), BF16), pltpu.VMEM((Rr, W), BF16),
                        pltpu.VMEM((R, 1), F32), pltpu.VMEM((R, 1), F32), pltpu.VMEM((R, C), F32)])
    return pl.pallas_call(
        functools.partial(_mla_sample_body, pps=pps, H=H, ts=ts, page=page, nc=nc, n_steps=Bs * nc, layer=layer,
                          scale=scale, vh=vh),
        grid_spec=grid_spec, out_shape=jax.ShapeDtypeStruct((Bs, ts, H * vh), F32),
        compiler_params=_cparams(("arbitrary", "arbitrary")), name="mla_sample_attn")(
            page_table.reshape(-1), q_lat, q_pe, pool_ckv, pool_kpe_t, ckv_new, kpe_new, w_uv)


def _t5_bias_of_dist(d, rb_ref, h, n_buckets):
    dist = jnp.maximum(d, 0)
    exact = n_buckets // 2
    df = jnp.maximum(dist, 1).astype(F32)
    large = exact + (jnp.log(df / exact) / math.log(REL_MAX_DIST / exact) * (n_buckets - exact)).astype(jnp.int32)
    large = jnp.minimum(large, n_buckets - 1)
    bucket = jnp.where(dist < exact, dist, large)
    out = jnp.zeros(d.shape, F32)
    for b in range(n_buckets):
        out = jnp.where(bucket == b, rb_ref[b, h], out)
    return out


def _bias_tables_body(rb_ref, t0_ref, t1_ref, tp_ref, tn_ref, *, NH, tq, ts8, n_buckets):
    r = lax.broadcasted_iota(jnp.int32, (tq, tq), 0)
    c = lax.broadcasted_iota(jnp.int32, (tq, tq), 1)
    r8 = lax.broadcasted_iota(jnp.int32, (ts8, LANES), 0)
    c8 = lax.broadcasted_iota(jnp.int32, (ts8, LANES), 1)
    for h in range(NH):
        t0_ref[h] = _t5_bias_of_dist(r - c, rb_ref, h, n_buckets)
        t1_ref[h] = _t5_bias_of_dist(r - c + tq, rb_ref, h, n_buckets)
        tp_ref[h] = _t5_bias_of_dist(r8 - c8 + LANES, rb_ref, h, n_buckets)
        tn_ref[h] = _t5_bias_of_dist(r8 - c8, rb_ref, h, n_buckets)


def _bias_tables(rel_bias, tq, ts8):
    nb, NH = rel_bias.shape
    return pl.pallas_call(
        functools.partial(_bias_tables_body, NH=NH, tq=tq, ts8=ts8, n_buckets=nb),
        in_specs=[pl.BlockSpec(memory_space=pltpu.SMEM)],
        out_shape=[jax.ShapeDtypeStruct((NH, tq, tq), F32), jax.ShapeDtypeStruct((NH, tq, tq), F32),
                   jax.ShapeDtypeStruct((NH, ts8, LANES), F32), jax.ShapeDtypeStruct((NH, ts8, LANES), F32)],
        compiler_params=pltpu.CompilerParams(vmem_limit_bytes=VMEM_LIMIT_MB * 1024 * 1024),
        name="t5_bias_tables")(rel_bias)


def _bisect(lo0, hi0, count_ge, k_sel):
    def body(_, lh):
        lo, hi = lh
        mid = 0.5 * (lo + hi)
        ge = count_ge(mid) >= k_sel
        return jnp.where(ge, mid, lo), jnp.where(ge, hi, mid)

    lo, _ = lax.fori_loop(0, BISECT_ITERS, body, (lo0, hi0))
    return lo


def _dsa_prompt_body(far_ref, q_ref, qi_ref, wi_ref, k_ref, v_ref, kit_ref, t0_ref, t1_ref, o_ref,
                     sc_ref, wb_ref, s_ref, lp_ref, acc_ref, *, tq, nq, k_sel, IH, ID, KVH, G, HD, scale):
    i = pl.program_id(1)
    row_tok = i * tq + lax.broadcasted_iota(jnp.int32, (tq, tq), 0)
    col_in = lax.broadcasted_iota(jnp.int32, (tq, tq), 1)
    qi = qi_ref[...]
    wi = wi_ref[...]
    for h in range(IH):
        wb_ref[h] = jnp.broadcast_to(wi[:, h:h + 1], (tq, tq))

    def idx_block(c, carry):
        lo, hi = carry
        kit = kit_ref[0, c]
        tot = jnp.zeros((tq, tq), F32)
        for h in range(IH):
            d = _dot(qi[:, h * ID:(h + 1) * ID], kit)
            tot = tot + jnp.maximum(d, 0.0) * wb_ref[h]
        ok = (c * tq + col_in) <= row_tok
        sc_ref[c] = jnp.where(ok, tot, NEG_INF)
        lo = jnp.minimum(lo, jnp.min(jnp.where(ok, tot, jnp.inf), axis=-1, keepdims=True))
        hi = jnp.maximum(hi, jnp.max(jnp.where(ok, tot, NEG_INF), axis=-1, keepdims=True))
        return lo, hi

    lo0, hi0 = lax.fori_loop(0, i + 1, idx_block,
                             (jnp.full((tq, 1), jnp.inf, F32), jnp.full((tq, 1), NEG_INF, F32)))

    def count_ge(mid):
        def blk(c, acc):
            t = sc_ref[c]
            part = jnp.zeros((tq, LANES), F32)
            for j in range(tq // LANES):
                part = part + jnp.where(t[:, j * LANES:(j + 1) * LANES] >= mid, 1.0, 0.0)
            return acc + part
        acc = lax.fori_loop(0, i + 1, blk, jnp.zeros((tq, LANES), F32))
        return jnp.sum(acc, axis=-1, keepdims=True)

    thr = _bisect(lo0, hi0, count_ge, float(k_sel))

    def to_mask(c, _):
        sc_ref[c] = jnp.where(sc_ref[c] >= thr, 0.0, NEG_INF)
        return 0

    lax.fori_loop(0, i + 1, to_mask, 0)

    q = q_ref[...]
    R = G * tq
    nlv = tq // LANES

    def lane_fold(x, op):
        r = x[:, :LANES]
        for jj in range(1, nlv):
            r = op(r, x[:, jj * LANES:(jj + 1) * LANES])
        return r

    for n in range(KVH):
        qn = jnp.concatenate([q[:, (n * G + g) * HD:(n * G + g + 1) * HD] for g in range(G)], axis=0)

        def score(c, slot, bias):
            kc = k_ref[0, pl.ds(pl.multiple_of(c * tq, tq), tq), n * HD:(n + 1) * HD]
            s = _dot_nt(qn, kc) * scale + bias + jnp.concatenate([sc_ref[c]] * G, axis=0)
            s_ref[slot] = s
            return lane_fold(s, jnp.maximum)

        far = jnp.concatenate([jnp.full((tq, 1), far_ref[n * G + g], F32) for g in range(G)], axis=0)
        mx = lax.fori_loop(0, jnp.maximum(i - 1, 0), lambda c, mx: jnp.maximum(mx, score(c, c, far)),
                           jnp.full((R, LANES), NEG_INF, F32))
        prev = score(jnp.maximum(i - 1, 0), jnp.where(i >= 1, i - 1, nq), t1_ref[n * G:(n + 1) * G].reshape(R, tq))
        mx = jnp.where(i >= 1, jnp.maximum(mx, prev), mx)
        mx = jnp.maximum(mx, score(i, i, t0_ref[n * G:(n + 1) * G].reshape(R, tq)))
        m = jnp.max(mx, axis=-1, keepdims=True)

        lp_ref[...] = jnp.zeros(lp_ref.shape, F32)
        acc_ref[...] = jnp.zeros(acc_ref.shape, F32)

        def pv(c, _):
            p = jnp.exp(s_ref[c] - m)
            lp_ref[...] += lane_fold(p, jnp.add)
            vc = v_ref[0, pl.ds(pl.multiple_of(c * tq, tq), tq), n * HD:(n + 1) * HD]
            acc_ref[...] += _dot(p.astype(BF16), vc)
            return 0

        lax.fori_loop(0, i + 1, pv, 0)
        o = acc_ref[...] / jnp.sum(lp_ref[...], axis=-1, keepdims=True)
        for g in range(G):
            o_ref[:, (n * G + g) * HD:(n * G + g + 1) * HD] = o[g * tq:(g + 1) * tq].astype(o_ref.dtype)


def _dsa_prompt(q, qi, wi, k_b, v_b, ki_t, t0, t1, far_bias, B, T, k_sel, IH, ID, KVH, G, HD, scale, tq):
    M = B * T
    nq = T // tq
    NH = KVH * G
    row = lambda n: pl.BlockSpec((tq, n), lambda b, i, f: (b * nq + i, 0))
    seq = lambda n: pl.BlockSpec((1, T, n), lambda b, i, f: (b, 0, 0))
    tab = pl.BlockSpec((NH, tq, tq), lambda b, i, f: (0, 0, 0))
    grid_spec = pltpu.PrefetchScalarGridSpec(
        num_scalar_prefetch=1, grid=(B, nq),
        in_specs=[row(NH * HD), row(IH * ID), row(IH), seq(KVH * HD), seq(KVH * HD),
                  pl.BlockSpec((1, nq, ID, tq), lambda b, i, f: (b, 0, 0, 0)), tab, tab],
        out_specs=row(NH * HD),
        scratch_shapes=[pltpu.VMEM((nq, tq, tq), F32), pltpu.VMEM((IH, tq, tq), F32),
                        pltpu.VMEM((nq + 1, G * tq, tq), F32), pltpu.VMEM((G * tq, LANES), F32),
                        pltpu.VMEM((G * tq, HD), F32)])
    return pl.pallas_call(
        functools.partial(_dsa_prompt_body, tq=tq, nq=nq, k_sel=k_sel, IH=IH, ID=ID, KVH=KVH, G=G, HD=HD,
                          scale=scale),
        grid_spec=grid_spec, out_shape=jax.ShapeDtypeStruct((M, NH * HD), BF16),
        compiler_params=_cparams(("parallel", "arbitrary")), name="dsa_prompt")(
            far_bias, q, qi, wi, k_b.reshape(B, T, -1), v_b.reshape(B, T, -1), ki_t, t0, t1)


def _dsa_sample_index_body(pt_ref, qi_ref, wi_ref, ik_pool, kn_ref, sc_ref, scn_ref, ibuf, sem,
                           *, pps, IH, ts, ts8, page, n_steps, layer):
    c = pl.program_id(1)
    slot = _fetch_pages(pt_ref, [(ik_pool, ibuf, True)], sem, pps, page, layer, n_steps)
    qi = qi_ref[0]
    wcol = wi_ref[0]
    kit = ibuf[slot].astype(BF16)
    d = jnp.maximum(_dot(qi, kit), 0.0) * wcol
    tot = d[0:ts8]
    for h in range(1, IH):
        tot = tot + d[h * ts8:(h + 1) * ts8]
    sc_ref[0] = tot

    @pl.when(c == 0)
    def _():
        cols = _new_key_scores([qi.astype(F32)], [kn_ref[0]], ts)
        ncols = []
        for col in cols:
            col = jnp.maximum(col, 0.0) * wcol
            t = col[0:ts8]
            for h in range(1, IH):
                t = t + col[h * ts8:(h + 1) * ts8]
            ncols.append(t)
        blk = _cols_to_block(ncols, ts8)
        qt = lax.broadcasted_iota(jnp.int32, (ts8, LANES), 0)
        lane = lax.broadcasted_iota(jnp.int32, (ts8, LANES), 1)
        scn_ref[0] = jnp.where(lane <= qt, blk, NEG_INF)


def _dsa_sample_index(qi_rows, wi_rows, pool_ik_t, layer, page_table, ki_new, IH, ts, ts8, pps=32):
    Bs, R, ID = qi_rows.shape
    n_pages = page_table.shape[1]
    page = pool_ik_t.shape[3]
    pps = min(pps, n_pages)
    nc = n_pages // pps
    W = pps * page

    in_specs = [pl.BlockSpec((1, R, ID), lambda b, c, pt: (b, 0, 0)),
                pl.BlockSpec((1, R, 1), lambda b, c, pt: (b, 0, 0)),
                pl.BlockSpec(memory_space=pl.ANY),
                pl.BlockSpec((1, ts, ID), lambda b, c, pt: (b, 0, 0))]
    grid_spec = pltpu.PrefetchScalarGridSpec(
        num_scalar_prefetch=1, grid=(Bs, nc), in_specs=in_specs,
        out_specs=[pl.BlockSpec((1, ts8, W), lambda b, c, pt: (b, 0, c)),
                   pl.BlockSpec((1, ts8, LANES), lambda b, c, pt: (b, 0, 0))],
        scratch_shapes=[pltpu.VMEM((2, ID, W), F32), pltpu.SemaphoreType.DMA((2,))])
    return pl.pallas_call(
        functools.partial(_dsa_sample_index_body, pps=pps, IH=IH, ts=ts, ts8=ts8, page=page, n_steps=Bs * nc,
                          layer=layer),
        grid_spec=grid_spec,
        out_shape=[jax.ShapeDtypeStruct((Bs, ts8, nc * W), F32), jax.ShapeDtypeStruct((Bs, ts8, LANES), F32)],
        compiler_params=_cparams(("arbitrary", "arbitrary")), name="dsa_sample_index")(
            page_table.reshape(-1), qi_rows, wi_rows, pool_ik_t, ki_new)


def _topk_threshold_body(sc_ref, scn_ref, thr_ref, *, k_sel):
    n_groups = sc_ref.shape[2] // LANES
    scn = scn_ref[...]
    lo0 = jnp.min(jnp.where(scn > NEG_INF, scn, jnp.inf), axis=-1, keepdims=True)
    hi0 = jnp.max(scn, axis=-1, keepdims=True)
    lo_g = hi_g = sc_ref[:, :, :LANES]
    for j in range(1, n_groups):
        t = sc_ref[:, :, j * LANES:(j + 1) * LANES]
        lo_g = jnp.minimum(lo_g, t)
        hi_g = jnp.maximum(hi_g, t)
    lo0 = jnp.minimum(lo0, jnp.min(lo_g, axis=-1, keepdims=True))
    hi0 = jnp.maximum(hi0, jnp.max(hi_g, axis=-1, keepdims=True))

    def count_ge(mid):
        acc = jnp.where(scn >= mid, 1.0, 0.0)
        for j in range(n_groups):
            acc = acc + jnp.where(sc_ref[:, :, j * LANES:(j + 1) * LANES] >= mid, 1.0, 0.0)
        return jnp.sum(acc, axis=-1, keepdims=True)

    thr = _bisect(lo0, hi0, count_ge, float(k_sel))
    thr_ref[...] = jnp.broadcast_to(thr, thr_ref.shape)


def _topk_threshold(scores, scores_new, k_sel, bb=16):
    Bs, ts8, P = scores.shape
    bb = min(bb, Bs)
    assert Bs % bb == 0
    return pl.pallas_call(
        functools.partial(_topk_threshold_body, k_sel=k_sel), grid=(Bs // bb,),
        in_specs=[pl.BlockSpec((bb, ts8, P), lambda i: (i, 0, 0)),
                  pl.BlockSpec((bb, ts8, LANES), lambda i: (i, 0, 0))],
        out_specs=pl.BlockSpec((bb, ts8, LANES), lambda i: (i, 0, 0)),
        out_shape=jax.ShapeDtypeStruct((Bs, ts8, LANES), F32),
        compiler_params=_cparams(("parallel",)), name="topk_threshold")(scores, scores_new)


def _dsa_sample_attn_body(pt_ref, far_ref, q_ref, sc_ref, scn_ref, thr_ref, k_pool, v_pool, kn_ref, vn_ref,
                          tp_ref, tn_ref, o_ref, kbuf, vbuf, sem, kb_ref, vb_ref, m_ref, l_ref, acc_ref,
                          *, pps, KVH, G, HD, ts, ts8, page, nc, n_steps, layer, scale):
    c = pl.program_id(1)
    R = G * ts8
    W = pps * page
    slot = _fetch_pages(pt_ref, [(k_pool, kbuf, False), (v_pool, vbuf, False)], sem, pps, page, layer, n_steps)

    @pl.when(c == 0)
    def _():
        m_ref[...] = jnp.full(m_ref.shape, NEG_INF, F32)
        l_ref[...] = jnp.zeros(l_ref.shape, F32)
        acc_ref[...] = jnp.zeros(acc_ref.shape, F32)

    for n in range(KVH):
        kb_ref[:, n * HD:(n + 1) * HD] = kbuf[slot, pl.ds(n, W, stride=KVH), :].astype(BF16)
        vb_ref[:, n * HD:(n + 1) * HD] = vbuf[slot, pl.ds(n, W, stride=KVH), :].astype(BF16)
    thr = thr_ref[0][:, :1]
    msk = jnp.concatenate([jnp.where(sc_ref[0] >= thr, 0.0, NEG_INF)] * G, axis=0)
    last = c == nc - 1
    for n in range(KVH):
        qn = q_ref[0, n]
        far = jnp.concatenate([jnp.full((ts8, 1), far_ref[n * G + g], F32) for g in range(G)], axis=0)
        near = tp_ref[n * G:(n + 1) * G].reshape(R, LANES)
        delta = jnp.where(last, near - far, 0.0)
        s = _dot_nt(qn, kb_ref[:, n * HD:(n + 1) * HD]) * scale + far + msk
        s = jnp.concatenate([s[:, :W - LANES], s[:, W - LANES:] + delta], axis=1)
        _flash_update(s, vb_ref[:, n * HD:(n + 1) * HD], m_ref.at[n], l_ref.at[n], acc_ref.at[n])

    @pl.when(last)
    def _():
        mskn = jnp.concatenate([jnp.where(scn_ref[0] >= thr, 0.0, NEG_INF)] * G, axis=0)
        outs = []
        for n in range(KVH):
            qn = q_ref[0, n].astype(F32)
            kn = kn_ref[0][:, n * HD:(n + 1) * HD]
            vn = vn_ref[0][:, n * HD:(n + 1) * HD]
            cols = _new_key_scores([qn], [kn], ts)
            blk = _cols_to_block(cols, R) * scale + tn_ref[n * G:(n + 1) * G].reshape(R, LANES) + mskn
            _flash_update_new(blk, vn, ts, m_ref.at[n], l_ref.at[n], acc_ref.at[n])
            o = acc_ref[n] / l_ref[n]
            outs += [o[g * ts8:g * ts8 + ts] for g in range(G)]
        o_ref[0] = jnp.concatenate(outs, axis=1)


def _dsa_sample_attn(q_rows, scores, scores_new, thr, pool_k, pool_v, layer, page_table, k_new, v_new, tp, tn,
                     far_bias, KVH, G, HD, ts, ts8, scale, pps=32):
    Bs = q_rows.shape[0]
    R = G * ts8
    n_pages = page_table.shape[1]
    page = pool_k.shape[2] // KVH
    pps = min(pps, n_pages)
    nc = n_pages // pps
    W = pps * page
    NH = KVH * G

    hbm = pl.BlockSpec(memory_space=pl.ANY)
    in_specs = [pl.BlockSpec((1, KVH, R, HD), lambda b, c, pt, f: (b, 0, 0, 0)),
                pl.BlockSpec((1, ts8, W), lambda b, c, pt, f: (b, 0, c)),
                pl.BlockSpec((1, ts8, LANES), lambda b, c, pt, f: (b, 0, 0)),
                pl.BlockSpec((1, ts8, LANES), lambda b, c, pt, f: (b, 0, 0)),
                hbm, hbm]
    in_specs += [pl.BlockSpec((1, ts, KVH * HD), lambda b, c, pt, f: (b, 0, 0))] * 2
    in_specs += [pl.BlockSpec((NH, ts8, LANES), lambda b, c, pt, f: (0, 0, 0))] * 2
    grid_spec = pltpu.PrefetchScalarGridSpec(
        num_scalar_prefetch=2, grid=(Bs, nc), in_specs=in_specs,
        out_specs=pl.BlockSpec((1, ts, NH * HD), lambda b, c, pt, f: (b, 0, 0)),
        scratch_shapes=[pltpu.VMEM((2, W * KVH, HD), F32), pltpu.VMEM((2, W * KVH, HD), F32),
                        pltpu.SemaphoreType.DMA((2,)),
                        pltpu.VMEM((W, KVH * HD), BF16), pltpu.VMEM((W, KVH * HD), BF16),
                        pltpu.VMEM((KVH, R, 1), F32), pltpu.VMEM((KVH, R, 1), F32), pltpu.VMEM((KVH, R, HD), F32)])
    return pl.pallas_call(
        functools.partial(_dsa_sample_attn_body, pps=pps, KVH=KVH, G=G, HD=HD, ts=ts, ts8=ts8, page=page,
                          nc=nc, n_steps=Bs * nc, layer=layer, scale=scale),
        grid_spec=grid_spec, out_shape=jax.ShapeDtypeStruct((Bs, ts, NH * HD), F32),
        compiler_params=_cparams(("arbitrary", "arbitrary")), name="dsa_sample_attn")(
            page_table.reshape(-1), far_bias, q_rows, scores, scores_new, thr, pool_k, pool_v,
            k_new, v_new, tp, tn)


def _rot_tables(pos, half, group, extra=None):
    lane = jnp.arange(LANES)
    g = lane % group
    rotary = g < 2 * half
    freq = ROPE_THETA ** (-jnp.arange(half, dtype=F32) / half)
    ang = pos.astype(F32)[:, None] * freq[None, :]
    cos = jnp.cos(ang)[:, g % half]
    sin = jnp.sin(ang)[:, g % half]
    c_tab = jnp.where(rotary[None, :], cos, 1.0)
    s_tab = jnp.where(rotary[None, :], jnp.where((g < half)[None, :], -sin, sin), 0.0)
    if extra is not None:
        a, b, val = extra
        c_tab = jnp.where(((lane >= a) & (lane < b))[None, :], val, c_tab)
    return c_tab.astype(F32), s_tab.astype(F32)


def kernel(x_prompt, x_sample, cache_mla_ckv, cache_mla_kpe, cache_dsa_k, cache_dsa_v, cache_dsa_idx_k,
           page_table, c_prompt, c_sample, w_ada, b_ada, ln_g, ln_b,
           mla_w_in, mla_q_norm, mla_kv_norm, mla_w_uq, mla_w_uk, mla_w_uv, mla_w_o,
           dsa_w_in, dsa_w_o, rel_bias, ffn_w_gu, ffn_w_down):
    B, T, D = x_prompt.shape
    Bs, ts, _ = x_sample.shape
    depth = w_ada.shape[0]
    alpha = (2 * depth) ** 0.25
    n_pages = page_table.shape[1]
    page = cache_mla_ckv.shape[2]
    past = n_pages * page
    Mp, Ms = B * T, Bs * ts
    ts8 = 8 * (-(-ts // 8))

    ql = mla_q_norm.shape[1]
    kvl = cache_mla_ckv.shape[-1]
    rope = cache_mla_kpe.shape[-1]
    H = mla_w_uk.shape[2]
    nope = mla_w_uk.shape[3]
    vh = mla_w_uv.shape[3]
    mla_scale = (nope + rope) ** -0.5
    KVH, HD = cache_dsa_k.shape[-2:]
    ID = cache_dsa_idx_k.shape[-1]
    NH = dsa_w_o.shape[1] // HD
    G = NH // KVH
    dq, dkv = NH * HD, KVH * HD
    IH = (dsa_w_in.shape[2] - dq - 2 * dkv - ID) // (ID + 1)
    dsa_scale = HD ** -0.5
    idx_wscale = (IH ** -0.5) * (ID ** -0.5)
    tq_dsa = min(256, T)

    pos_p = jnp.arange(T, dtype=jnp.int32)
    pos_s = jnp.tile(past + jnp.arange(ts, dtype=jnp.int32), Bs)
    tabs = {
        "mla": (_rot_tables(pos_p, rope // 2, rope), _rot_tables(pos_s, rope // 2, rope)),
        "idx": (_rot_tables(pos_p, IDX_ROPE // 2, ID), _rot_tables(pos_s, IDX_ROPE // 2, ID)),
        "tail": (_rot_tables(pos_p, IDX_ROPE // 2, LANES, (ID, ID + IH, idx_wscale)),
                 _rot_tables(pos_s, IDX_ROPE // 2, LANES, (ID, ID + IH, idx_wscale))),
    }

    c_all = jnp.concatenate([c_prompt, c_sample], axis=0)
    nc_rows = c_all.shape[0]
    c_all = jnp.pad(c_all, ((0, -nc_rows % 8), (0, 0)))
    mods = [_mm(c_all, w_ada, layer=i, bias=b_ada[i].reshape(1, -1), act="silu", tm=c_all.shape[0], tn=1024,
                name="ada_mod") for i in range(depth)]

    pool_kpe_t = jnp.swapaxes(cache_mla_kpe, 2, 3)
    pool_ik_t = jnp.swapaxes(cache_dsa_idx_k, 2, 3)
    pool_k = cache_dsa_k.reshape(cache_dsa_k.shape[0], -1, page * cache_dsa_k.shape[3], cache_dsa_k.shape[4])
    pool_v = cache_dsa_v.reshape(pool_k.shape)

    def mod_vectors(i):
        m = mods[i]
        mp = [m[:B, j * D:(j + 1) * D].reshape(B, 1, D) for j in range(6)]
        ms = [jnp.repeat(m[B:B + Bs, j * D:(j + 1) * D], ts, axis=0) for j in range(6)]
        return mp, ms

    groups = [dict(y=x_prompt.reshape(Mp, D), rpg=T, nb=B, tb=T, g=0),
              dict(y=x_sample.reshape(Ms, D), rpg=None, nb=1, tb=Ms, g=1)]
    modv = [mod_vectors(i) for i in range(depth)]
    for grp in groups:
        sh1, sc1 = modv[0][grp["g"]][0], modv[0][grp["g"]][1]
        grp["u"] = _modulate(grp["y"], sh1, sc1, grp["rpg"])

    t0 = t1 = tp = tn = far_bias = None
    if depth > 1:
        t0, t1, tp, tn = _bias_tables(rel_bias, tq_dsa, ts8)
        far_bias = rel_bias[-1]

    outs = {k: [[], []] for k in ("ckv", "kpe", "k", "v", "ik")}
    for i in range(depth):
        j = i // 2
        if i % 2 == 0:
            w_in_pad = jnp.pad(mla_w_in[j], ((0, 0), (0, LANES - rope))).astype(BF16)
            w_uq = mla_w_uq[j].reshape(ql, H, nope + rope)
            w_nope = w_uq[:, :, :nope].reshape(ql, H * nope).astype(BF16)
            w_pe = w_uq[:, :, nope:].reshape(ql, H * rope).astype(BF16)
            w_uk_t = jnp.transpose(mla_w_uk[j], (1, 2, 0)).astype(BF16)
            w_uv = mla_w_uv[j].reshape(kvl, H * vh).astype(BF16)
            w_o = mla_w_o[j].astype(BF16)
        else:
            w_in = dsa_w_in[j].astype(BF16)
            w_tail = jnp.pad(dsa_w_in[j][:, dq + 2 * dkv + IH * ID:], ((0, 0), (0, LANES - ID - IH))).astype(BF16)
            w_o = dsa_w_o[j].astype(BF16)
        w_down = ffn_w_down[i].astype(BF16)

        for grp in groups:
            g = grp["g"]
            u = grp["u"]
            mv = modv[i][g]
            if i % 2 == 0:
                ct, st = tabs["mla"][g]
                cq, ckv, ckv_b, kpe, kpe_b = _mla_in(u, w_in_pad, mla_q_norm[j], mla_kv_norm[j], ct, st,
                                                     ql, kvl, rope)
                q_lat, q_pe = _mla_q(cq, w_nope, w_pe, w_uk_t, ct, st, grp["nb"], grp["tb"], H, nope, rope, kvl)
                if g == 0:
                    o = _mla_prompt_attn(q_lat, q_pe, ckv_b.reshape(B, T, kvl), kpe_b.reshape(B, T, rope),
                                         w_uv, mla_scale)
                else:
                    def rows(a):
                        n = a.shape[-1]
                        return a.reshape(H, Bs, ts, n).transpose(1, 0, 2, 3).reshape(Bs, H * ts, n)
                    o = _mla_sample_attn(rows(q_lat), rows(q_pe), cache_mla_ckv, pool_kpe_t, j, page_table,
                                         ckv.reshape(Bs, ts, kvl), kpe.reshape(Bs, ts, rope), w_uv, mla_scale, H)
                    o = o.reshape(Ms, H * vh).astype(BF16)
                outs["ckv"][g].append(ckv)
                outs["kpe"][g].append(kpe)
            else:
                q = _mm(u, w_in, n_cols=dq, col0=0, out_dtype=BF16, name="dsa_q")
                k, k_b = _mm(u, w_in, n_cols=dkv, col0=dq, second_dtype=BF16, name="dsa_k")
                v, v_b = _mm(u, w_in, n_cols=dkv, col0=dq + dkv, second_dtype=BF16, name="dsa_v")
                ci, si = tabs["idx"][g]
                qi = _mm(u, w_in, n_cols=IH * ID, col0=dq + 2 * dkv, out_dtype=BF16,
                         rot=(ci, si, IDX_ROPE // 2), name="dsa_qi")
                ctl, stl = tabs["tail"][g]
                tail = _mm(u, w_tail, rot=(ctl, stl, IDX_ROPE // 2), name="dsa_tail")
                ki = tail[:, :ID]
                wi = tail[:, ID:ID + IH]
                if g == 0:
                    ki_t = jnp.swapaxes(ki.astype(BF16).reshape(B, T // tq_dsa, tq_dsa, ID), 2, 3)
                    o = _dsa_prompt(q, qi, wi, k_b, v_b, ki_t, t0, t1, far_bias, B, T,
                                    min(TOPK_MAX, T // 4), IH, ID, KVH, G, HD, dsa_scale, tq_dsa)
                else:
                    pad_t = ((0, 0), (0, 0), (0, ts8 - ts), (0, 0))
                    qi_rows = jnp.pad(qi.reshape(Bs, ts, IH, ID).transpose(0, 2, 1, 3), pad_t
                                      ).reshape(Bs, IH * ts8, ID)
                    wi_rows = jnp.pad(wi.reshape(Bs, ts, IH).transpose(0, 2, 1), ((0, 0), (0, 0), (0, ts8 - ts))
                                      ).reshape(Bs, IH * ts8, 1)
                    sc, sc_new = _dsa_sample_index(qi_rows, wi_rows, pool_ik_t, j, page_table,
                                                   ki.reshape(Bs, ts, ID), IH, ts, ts8)
                    thr = _topk_threshold(sc, sc_new, min(TOPK_MAX, (past + ts) // 4))
                    q_rows = jnp.pad(q.reshape(Bs, ts, NH, HD).transpose(0, 2, 1, 3), pad_t
                                     ).reshape(Bs, KVH, G * ts8, HD)
                    o = _dsa_sample_attn(q_rows, sc, sc_new, thr, pool_k, pool_v, j,
                                         page_table, k.reshape(Bs, ts, dkv), v.reshape(Bs, ts, dkv), tp, tn, far_bias,
                                         KVH, G, HD, ts, ts8, dsa_scale)
                    o = o.reshape(Ms, NH * HD).astype(BF16)
                outs["k"][g].append(k)
                outs["v"][g].append(v)
                outs["ik"][g].append(ki)
            y1, u2 = _mm_postnorm(o, w_o, grp["y"], mv[2], ln_g[i, 0], ln_b[i, 0], (mv[3], mv[4]), grp["rpg"],
                                  alpha, name="attn_out_postnorm")
            hmid = _swiglu_up(u2, ffn_w_gu, i)
            nxt = None
            if i + 1 < depth:
                nmv = modv[i + 1][g]
                nxt = (nmv[0], nmv[1])
            grp["y"], grp["u"] = _mm_postnorm(hmid, w_down, y1, mv[5], ln_g[i, 1], ln_b[i, 1], nxt, grp["rpg"],
                                              alpha, name="ffn_down_postnorm")

    def stack(key, g, shape):
        return jnp.stack([a.reshape(shape) for a in outs[key][g]])

    yp = groups[0]["y"].reshape(B, T, D)
    ys = groups[1]["y"].reshape(Bs, ts, D)
    return (yp, ys,
            stack("ckv", 0, (B, T, kvl)), stack("kpe", 0, (B, T, rope)),
            stack("k", 0, (B, T, KVH, HD)), stack("v", 0, (B, T, KVH, HD)), stack("ik", 0, (B, T, ID)),
            stack("ckv", 1, (Bs, ts, kvl)), stack("kpe", 1, (Bs, ts, rope)),
            stack("k", 1, (Bs, ts, KVH, HD)), stack("v", 1, (Bs, ts, KVH, HD)), stack("ik", 1, (Bs, ts, ID)))
```

```python
import functools
import math

import jax
import jax.numpy as jnp
import numpy as np
from jax import lax
from jax.experimental import pallas as pl
from jax.experimental.pallas import tpu as pltpu

F32 = jnp.float32
BF16 = jnp.bfloat16
NEG_INF = float("-inf")

LANES = 128
ROPE_THETA = 10000.0
IDX_ROPE = 32
TOPK_MAX = 256
REL_MAX_DIST = 128
LN_EPS = 1e-5
RMS_EPS = 1e-6
BISECT_ITERS = 32
VMEM_LIMIT_MB = 56
POSTNORM_VMEM_BUDGET_MB = 46


def _cparams(sem, vmem_mb=VMEM_LIMIT_MB):
    return pltpu.CompilerParams(dimension_semantics=sem, vmem_limit_bytes=vmem_mb * 1024 * 1024)


def _dot(a, b):
    return jnp.dot(a, b, preferred_element_type=F32)


def _dot_nt(a, b):
    return lax.dot_general(a, b, (((1,), (1,)), ((), ())), preferred_element_type=F32)


def _rot(v, c, s, half):
    lane = lax.broadcasted_iota(jnp.int32, v.shape, 1)
    partner = jnp.where((lane % (2 * half)) < half,
                        pltpu.roll(v, LANES - half, 1), pltpu.roll(v, half, 1))
    return v * c + partner * s


def _mm_body(*refs, nk, act, rot_half, has_bias, second):
    it = iter(refs)
    x_ref = next(it)
    w_ref = next(it)
    b_ref = next(it) if has_bias else None
    c_ref = next(it) if rot_half else None
    s_ref = next(it) if rot_half else None
    o_ref = next(it)
    o2_ref = next(it) if second else None
    acc_ref = next(it) if nk > 1 else None

    x = x_ref[...]
    if act == "silu":
        xf = x.astype(F32)
        x = xf * jax.nn.sigmoid(xf)
    part = _dot(x.astype(BF16), w_ref[...].astype(BF16))

    def finish(acc):
        if has_bias:
            acc = acc + b_ref[...]
        if rot_half:
            c = c_ref[...]
            s = s_ref[...]
            for j in range(acc.shape[1] // LANES):
                sl = slice(j * LANES, (j + 1) * LANES)
                r = _rot(acc[:, sl], c, s, rot_half)
                o_ref[:, sl] = r.astype(o_ref.dtype)
                if second:
                    o2_ref[:, sl] = r.astype(o2_ref.dtype)
        else:
            o_ref[...] = acc.astype(o_ref.dtype)
            if second:
                o2_ref[...] = acc.astype(o2_ref.dtype)

    if nk == 1:
        finish(part)
    else:
        k = pl.program_id(2)

        @pl.when(k == 0)
        def _():
            acc_ref[...] = part

        @pl.when(k > 0)
        def _():
            acc_ref[...] += part

        @pl.when(k == nk - 1)
        def _():
            finish(acc_ref[...])


def _mm(x, w, *, layer=None, n_cols=None, col0=0, tm=1024, tn=512, tk=None, out_dtype=F32, second_dtype=None,
        act=None, bias=None, rot=None, name="mm"):
    M, K = x.shape
    N = n_cols if n_cols is not None else w.shape[-1]
    tm = min(tm, M) if rot is None else min(tm, M, rot[0].shape[0])
    tn = min(tn, N)
    while N % tn or col0 % tn:
        tn //= 2
    tk = K if tk is None else min(tk, K)
    assert M % tm == 0 and K % tk == 0 and tn % LANES == 0
    nk = K // tk
    jb = col0 // tn
    if layer is None:
        w_spec = pl.BlockSpec((tk, tn), lambda i, j, k: (k, j + jb))
    else:
        w_spec = pl.BlockSpec((None, tk, tn), lambda i, j, k: (layer, k, j + jb))
    in_specs = [pl.BlockSpec((tm, tk), lambda i, j, k: (i, k)), w_spec]
    args = [x, w]
    if bias is not None:
        in_specs.append(pl.BlockSpec((1, tn), lambda i, j, k: (0, j)))
        args.append(bias)
    rot_half = 0
    if rot is not None:
        c_tab, s_tab, rot_half = rot
        nr = c_tab.shape[0] // tm
        assert c_tab.shape[0] % tm == 0
        for t in (c_tab, s_tab):
            in_specs.append(pl.BlockSpec((tm, LANES), lambda i, j, k: (i % nr, 0)))
            args.append(t)
    out_shape = [jax.ShapeDtypeStruct((M, N), out_dtype)]
    out_specs = [pl.BlockSpec((tm, tn), lambda i, j, k: (i, j))]
    if second_dtype is not None:
        out_shape.append(jax.ShapeDtypeStruct((M, N), second_dtype))
        out_specs.append(pl.BlockSpec((tm, tn), lambda i, j, k: (i, j)))
    scratch = [pltpu.VMEM((tm, tn), F32)] if nk > 1 else []
    res = pl.pallas_call(
        functools.partial(_mm_body, nk=nk, act=act, rot_half=rot_half, has_bias=bias is not None,
                          second=second_dtype is not None),
        grid=(M // tm, N // tn, nk),
        in_specs=in_specs, out_specs=out_specs, out_shape=out_shape, scratch_shapes=scratch,
        compiler_params=_cparams(("parallel", "parallel", "arbitrary")), name=name)(*args)
    return res if second_dtype is not None else res[0]


def _modulate_body(y_ref, sh_ref, sc_ref, u_ref):
    u_ref[...] = (y_ref[...] * (1.0 + sc_ref[...]) + sh_ref[...]).astype(u_ref.dtype)


def _mod_spec(tm, n, rows_per_group):
    if rows_per_group is None:
        return pl.BlockSpec((tm, n), lambda i, *_: (i, 0))
    per = rows_per_group // tm
    return pl.BlockSpec((None, 1, n), lambda i, *_: (i // per, 0, 0))


def _modulate(y, sh, sc, rows_per_group, tm=512):
    M, D = y.shape
    tm = min(tm, M)
    return pl.pallas_call(
        _modulate_body, grid=(M // tm,),
        in_specs=[pl.BlockSpec((tm, D), lambda i: (i, 0)), _mod_spec(tm, D, rows_per_group),
                  _mod_spec(tm, D, rows_per_group)],
        out_specs=pl.BlockSpec((tm, D), lambda i: (i, 0)),
        out_shape=jax.ShapeDtypeStruct((M, D), BF16),
        compiler_params=_cparams(("parallel",)), name="modulate")(y, sh, sc)


def _mm_postnorm_body(*refs, alpha, with_mod):
    if with_mod:
        x_ref, w_ref, y_ref, g_ref, lg_ref, lb_ref, sh_ref, sc_ref, yo_ref, u_ref = refs
    else:
        x_ref, w_ref, y_ref, g_ref, lg_ref, lb_ref, yo_ref = refs
    h = _dot(x_ref[...].astype(BF16), w_ref[...].astype(BF16))
    z = alpha * y_ref[...] + g_ref[...] * h
    mu = jnp.mean(z, axis=-1, keepdims=True)
    zc = z - mu
    var = jnp.mean(zc * zc, axis=-1, keepdims=True)
    yn = zc * lax.rsqrt(var + LN_EPS) * lg_ref[...] + lb_ref[...]
    yo_ref[...] = yn
    if with_mod:
        u_ref[...] = (yn * (1.0 + sc_ref[...]) + sh_ref[...]).astype(u_ref.dtype)


def _mm_postnorm(x, w, y, gate, ln_g, ln_b, mod, rows_per_group, alpha, name="mm_postnorm"):
    M, K = x.shape
    N = w.shape[1]

    def vmem_bytes(tm):
        return K * N * 2 + 2 * tm * K * 2 + tm * N * (2 * 4 + 2 * 4 + 2 * 2 + 4 + 4)

    tm = min(512, M)
    while tm > 8 and vmem_bytes(tm) > POSTNORM_VMEM_BUDGET_MB * 1024 * 1024:
        tm //= 2
    assert M % tm == 0
    row = pl.BlockSpec((tm, N), lambda i: (i, 0))
    vec = pl.BlockSpec((1, N), lambda i: (0, 0))
    in_specs = [pl.BlockSpec((tm, K), lambda i: (i, 0)), pl.BlockSpec((K, N), lambda i: (0, 0)),
                row, _mod_spec(tm, N, rows_per_group), vec, vec]
    args = [x, w, y, gate, ln_g.reshape(1, N), ln_b.reshape(1, N)]
    out_shape = [jax.ShapeDtypeStruct((M, N), F32)]
    out_specs = [row]
    if mod is not None:
        in_specs += [_mod_spec(tm, N, rows_per_group)] * 2
        args += list(mod)
        out_shape.append(jax.ShapeDtypeStruct((M, N), BF16))
        out_specs.append(row)
    res = pl.pallas_call(
        functools.partial(_mm_postnorm_body, alpha=alpha, with_mod=mod is not None),
        grid=(M // tm,), in_specs=in_specs, out_specs=out_specs, out_shape=out_shape,
        compiler_params=_cparams(("parallel",)), name=name)(*args)
    return (res[0], res[1]) if mod is not None else (res[0], None)


def _swiglu_body(x_ref, wg_ref, wu_ref, o_ref):
    x = x_ref[...]
    g = _dot(x, wg_ref[...].astype(BF16))
    u = _dot(x, wu_ref[...].astype(BF16))
    o_ref[...] = (g * jax.nn.sigmoid(g) * u).astype(o_ref.dtype)


def _swiglu_up(x, w_gu, layer, tm=1024, tn=512):
    M, K = x.shape
    F = w_gu.shape[2] // 2
    tm = min(tm, M)
    tn = min(tn, F)
    while F % tn:
        tn //= 2
    nj = F // tn
    return pl.pallas_call(
        _swiglu_body, grid=(M // tm, nj),
        in_specs=[pl.BlockSpec((tm, K), lambda i, j: (i, 0)),
                  pl.BlockSpec((None, K, tn), lambda i, j: (layer, 0, j)),
                  pl.BlockSpec((None, K, tn), lambda i, j: (layer, 0, j + nj))],
        out_specs=pl.BlockSpec((tm, tn), lambda i, j: (i, j)),
        out_shape=jax.ShapeDtypeStruct((M, F), BF16),
        compiler_params=_cparams(("parallel", "parallel")), name="swiglu_up")(x, w_gu, w_gu)


def _mla_in_body(x_ref, w_ref, qg_ref, kvg_ref, c_ref, s_ref, cq_ref, ckv_ref, ckvb_ref, kpe_ref, kpeb_ref,
                 *, ql, kvl, rope):
    acc = _dot(x_ref[...], w_ref[...])

    def rms(v, g):
        return v * lax.rsqrt(jnp.mean(v * v, axis=-1, keepdims=True) + RMS_EPS) * g

    cq_ref[...] = rms(acc[:, :ql], qg_ref[...]).astype(cq_ref.dtype)
    ckv = rms(acc[:, ql:ql + kvl], kvg_ref[...])
    ckv_ref[...] = ckv
    ckvb_ref[...] = ckv.astype(ckvb_ref.dtype)
    kpe = _rot(acc[:, ql + kvl:], c_ref[...], s_ref[...], rope // 2)[:, :rope]
    kpe_ref[...] = kpe
    kpeb_ref[...] = kpe.astype(kpeb_ref.dtype)


def _mla_in(u, w_pad, q_g, kv_g, c_tab, s_tab, ql, kvl, rope, tm=512):
    M, K = u.shape
    N = w_pad.shape[1]
    tm = min(tm, M)
    nr = c_tab.shape[0] // tm
    tab = pl.BlockSpec((tm, LANES), lambda i: (i % nr, 0))

    def row(n):
        return pl.BlockSpec((tm, n), lambda i: (i, 0))

    return pl.pallas_call(
        functools.partial(_mla_in_body, ql=ql, kvl=kvl, rope=rope), grid=(M // tm,),
        in_specs=[row(K), pl.BlockSpec((K, N), lambda i: (0, 0)),
                  pl.BlockSpec((1, ql), lambda i: (0, 0)), pl.BlockSpec((1, kvl), lambda i: (0, 0)), tab, tab],
        out_specs=[row(ql), row(kvl), row(kvl), row(rope), row(rope)],
        out_shape=[jax.ShapeDtypeStruct((M, ql), BF16), jax.ShapeDtypeStruct((M, kvl), F32),
                   jax.ShapeDtypeStruct((M, kvl), BF16), jax.ShapeDtypeStruct((M, rope), F32),
                   jax.ShapeDtypeStruct((M, rope), BF16)],
        compiler_params=_cparams(("parallel",)), name="mla_in")(
            u, w_pad, q_g.reshape(1, ql), kv_g.reshape(1, kvl), c_tab, s_tab)


def _mla_q_body(cq_ref, wn_ref, wp_ref, wuk_ref, c_ref, s_ref, ql_ref, qp_ref, *, H, nope, rope):
    cq = cq_ref[...]
    qn = _dot(cq, wn_ref[...]).astype(BF16)
    for h in range(H):
        ql_ref[0, h] = _dot(qn[:, h * nope:(h + 1) * nope], wuk_ref[h]).astype(ql_ref.dtype)
    qp = _dot(cq, wp_ref[...])
    c = c_ref[...]
    s = s_ref[...]
    per = LANES // rope
    for j in range(H // per):
        r = _rot(qp[:, j * LANES:(j + 1) * LANES], c, s, rope // 2)
        for e in range(per):
            qp_ref[0, j * per + e] = r[:, e * rope:(e + 1) * rope].astype(qp_ref.dtype)


def _mla_q(cq, w_nope, w_pe, w_uk_t, c_tab, s_tab, nb, tb, H, nope, rope, kvl, tm=512):
    M, ql = cq.shape
    tm = min(tm, tb)
    per = tb // tm
    nr = c_tab.shape[0] // tm
    tab = pl.BlockSpec((tm, LANES), lambda i: (i % nr, 0))
    full2 = lambda a: pl.BlockSpec(a.shape, lambda i: (0, 0))
    return pl.pallas_call(
        functools.partial(_mla_q_body, H=H, nope=nope, rope=rope), grid=(M // tm,),
        in_specs=[pl.BlockSpec((tm, ql), lambda i: (i, 0)), full2(w_nope), full2(w_pe),
                  pl.BlockSpec(w_uk_t.shape, lambda i: (0, 0, 0)), tab, tab],
        out_specs=[pl.BlockSpec((1, H, tm, kvl), lambda i: (i // per, 0, i % per, 0)),
                   pl.BlockSpec((1, H, tm, rope), lambda i: (i // per, 0, i % per, 0))],
        out_shape=[jax.ShapeDtypeStruct((nb, H, tb, kvl), BF16), jax.ShapeDtypeStruct((nb, H, tb, rope), BF16)],
        compiler_params=_cparams(("parallel",)), name="mla_q")(cq, w_nope, w_pe, w_uk_t, c_tab, s_tab)


def _flash_update(s, v, m_ref, l_ref, acc_ref):
    m_old = m_ref[...]
    m_new = jnp.maximum(m_old, jnp.max(s, axis=-1, keepdims=True))
    m_safe = jnp.where(m_new == NEG_INF, 0.0, m_new)
    a = jnp.exp(m_old - m_safe)
    p = jnp.exp(s - m_safe)
    l_ref[...] = a * l_ref[...] + jnp.sum(p, axis=-1, keepdims=True)
    acc_ref[...] = a * acc_ref[...] + _dot(p.astype(BF16), v)
    m_ref[...] = m_new


def _lane_fold(x, op):
    r = x[:, :LANES]
    for j in range(1, x.shape[1] // LANES):
        r = op(r, x[:, j * LANES:(j + 1) * LANES])
    return r


def _mla_prompt_body(ql_ref, qp_ref, ckv_ref, kpe_ref, wuv_ref, o_ref, s_ref, mx_ref, lp_ref, acc_ref,
                     *, H, tq, tk, scale, vh):
    i = pl.program_id(1)
    R = H * tq
    n_blk = lax.div(i * tq + tq - 1, tk) + 1

    def score(c, masked):
        ql = ql_ref[0].reshape(R, ql_ref.shape[-1])
        qp = qp_ref[0].reshape(R, qp_ref.shape[-1])
        start = pl.multiple_of(c * tk, tk)
        s = (_dot_nt(ql, ckv_ref[0, pl.ds(start, tk), :]) + _dot_nt(qp, kpe_ref[0, pl.ds(start, tk), :])) * scale
        if masked:
            tok = i * tq + (lax.broadcasted_iota(jnp.int32, s.shape, 0) % tq)
            key = c * tk + lax.broadcasted_iota(jnp.int32, s.shape, 1)
            s = jnp.where(key <= tok, s, NEG_INF)
        s_ref[c] = s
        return _lane_fold(s, jnp.maximum)

    mx_ref[...] = jnp.full(mx_ref.shape, NEG_INF, F32)

    def far(c, _):
        mx_ref[...] = jnp.maximum(mx_ref[...], score(c, False))
        return 0

    lax.fori_loop(0, n_blk - 1, far, 0)
    mx = jnp.maximum(mx_ref[...], score(n_blk - 1, True))
    m = jnp.max(mx, axis=-1, keepdims=True)

    lp_ref[...] = jnp.zeros(lp_ref.shape, F32)
    acc_ref[...] = jnp.zeros(acc_ref.shape, F32)

    def pv(c, _):
        p = jnp.exp(s_ref[c] - m)
        lp_ref[...] += _lane_fold(p, jnp.add)
        acc_ref[...] += _dot(p.astype(BF16), ckv_ref[0, pl.ds(pl.multiple_of(c * tk, tk), tk), :])
        return 0

    lax.fori_loop(0, n_blk, pv, 0)
    o = (acc_ref[...] / jnp.sum(lp_ref[...], axis=-1, keepdims=True)).astype(BF16)
    for h in range(H):
        o_ref[:, h * vh:(h + 1) * vh] = _dot(o[h * tq:(h + 1) * tq], wuv_ref[:, h * vh:(h + 1) * vh]
                                             ).astype(o_ref.dtype)


def _mla_prompt_attn(q_lat, q_pe, ckv_b, kpe_b, w_uv, scale, tq=128, tk=256):
    B, H, T, C = q_lat.shape
    R = q_pe.shape[-1]
    tq = min(tq, T)
    tk = min(tk, T)
    nq, nk = T // tq, T // tk
    vh = w_uv.shape[1] // H
    return pl.pallas_call(
        functools.partial(_mla_prompt_body, H=H, tq=tq, tk=tk, scale=scale, vh=vh),
        grid=(B, nq),
        in_specs=[pl.BlockSpec((1, H, tq, C), lambda b, i: (b, 0, i, 0)),
                  pl.BlockSpec((1, H, tq, R), lambda b, i: (b, 0, i, 0)),
                  pl.BlockSpec((1, T, C), lambda b, i: (b, 0, 0)), pl.BlockSpec((1, T, R), lambda b, i: (b, 0, 0)),
                  pl.BlockSpec(w_uv.shape, lambda b, i: (0, 0))],
        out_specs=pl.BlockSpec((tq, H * vh), lambda b, i: (b * nq + i, 0)),
        out_shape=jax.ShapeDtypeStruct((B * T, H * vh), BF16),
        scratch_shapes=[pltpu.VMEM((nk, H * tq, tk), F32), pltpu.VMEM((H * tq, LANES), F32),
                        pltpu.VMEM((H * tq, LANES), F32), pltpu.VMEM((H * tq, C), F32)],
        compiler_params=_cparams(("parallel", "arbitrary")), name="mla_prompt_attn")(
            q_lat, q_pe, ckv_b, kpe_b, w_uv)


def _new_key_scores(q_parts, k_parts, n_new):
    cols = []
    for t in range(n_new):
        acc = None
        for q, k in zip(q_parts, k_parts):
            d = jnp.sum(q * k[t:t + 1, :], axis=-1, keepdims=True)
            acc = d if acc is None else acc + d
        cols.append(acc)
    return cols


def _cols_to_block(cols, rows):
    lane = lax.broadcasted_iota(jnp.int32, (rows, LANES), 1)
    blk = jnp.full((rows, LANES), NEG_INF, F32)
    for t, c in enumerate(cols):
        blk = jnp.where(lane == t, c, blk)
    return blk


def _flash_update_new(s_blk, v_new, n_new, m_ref, l_ref, acc_ref):
    m_old = m_ref[...]
    m_new = jnp.maximum(m_old, jnp.max(s_blk, axis=-1, keepdims=True))
    m_safe = jnp.where(m_new == NEG_INF, 0.0, m_new)
    a = jnp.exp(m_old - m_safe)
    p = jnp.exp(s_blk - m_safe)
    l_ref[...] = a * l_ref[...] + jnp.sum(p, axis=-1, keepdims=True)
    acc = a * acc_ref[...]
    for t in range(n_new):
        acc = acc + p[:, t:t + 1] * v_new[t:t + 1, :]
    acc_ref[...] = acc
    m_ref[...] = m_new


def _page_copies(pt_ref, step, slot, streams, sem_ref, pps, page, layer):
    copies = []
    for r in range(pps):
        pid = pt_ref[step * pps + r]
        for pool, buf, lanes in streams:
            if lanes:
                dst = buf.at[slot, :, pl.ds(r * page, page)]
            else:
                rows = pool.shape[2]
                dst = buf.at[slot, pl.ds(r * rows, rows)]
            copies.append(pltpu.make_async_copy(pool.at[layer, pid], dst, sem_ref.at[slot]))
    return copies


def _fetch_pages(pt_ref, streams, sem_ref, pps, page, layer, n_steps):
    n_slots = streams[0][1].shape[0]
    ahead = n_slots - 1
    s = pl.program_id(0) * pl.num_programs(1) + pl.program_id(1)
    slot = s % n_slots

    @pl.when(s == 0)
    def _():
        for t in range(min(ahead, n_steps)):
            for cp in _page_copies(pt_ref, t, t, streams, sem_ref, pps, page, layer):
                cp.start()

    @pl.when(s + ahead < n_steps)
    def _():
        for cp in _page_copies(pt_ref, s + ahead, (s + ahead) % n_slots, streams, sem_ref, pps, page, layer):
            cp.start()

    for cp in _page_copies(pt_ref, s, slot, streams, sem_ref, pps, page, layer):
        cp.wait()
    return slot


def _mla_sample_body(pt_ref, ql_ref, qp_ref, ckv_pool, kpe_pool, cn_ref, kn_ref, wuv_ref, o_ref,
                     cbuf, pbuf, sem, kc_ref, kp_ref, m_ref, l_ref, acc_ref,
                     *, pps, H, ts, page, nc, n_steps, layer, scale, vh):
    c = pl.program_id(1)
    R = H * ts
    slot = _fetch_pages(pt_ref, [(ckv_pool, cbuf, False), (kpe_pool, pbuf, True)], sem, pps, page, layer, n_steps)

    @pl.when(c == 0)
    def _():
        m_ref[...] = jnp.full(m_ref.shape, NEG_INF, F32)
        l_ref[...] = jnp.zeros(l_ref.shape, F32)
        acc_ref[...] = jnp.zeros(acc_ref.shape, F32)

    kc_ref[...] = cbuf[slot].astype(BF16)
    kp_ref[...] = pbuf[slot].astype(BF16)
    ql = ql_ref[0]
    qp = qp_ref[0]
    ckv = kc_ref[...]
    s = (_dot_nt(ql, ckv) + _dot(qp, kp_ref[...])) * scale
    _flash_update(s, ckv, m_ref, l_ref, acc_ref)

    @pl.when(c == nc - 1)
    def _():
        cn = cn_ref[0]
        cols = _new_key_scores([ql.astype(F32), qp.astype(F32)], [cn, kn_ref[0]], ts)
        blk = _cols_to_block(cols, R) * scale
        qt = lax.broadcasted_iota(jnp.int32, (R, LANES), 0) % ts
        lane = lax.broadcasted_iota(jnp.int32, (R, LANES), 1)
        blk = jnp.where(lane <= qt, blk, NEG_INF)
        _flash_update_new(blk, cn, ts, m_ref, l_ref, acc_ref)
        o = (acc_ref[...] / l_ref[...]).astype(BF16)
        full = _dot(o, wuv_ref[...])
        o_ref[0] = jnp.concatenate([full[h * ts:(h + 1) * ts, h * vh:(h + 1) * vh] for h in range(H)], axis=1)


def _mla_sample_attn(q_lat, q_pe, pool_ckv, pool_kpe_t, layer, page_table, ckv_new, kpe_new, w_uv, scale, H,
                     pps=32):
    Bs, R, C = q_lat.shape
    Rr = q_pe.shape[-1]
    ts = R // H
    n_pages = page_table.shape[1]
    page = pool_ckv.shape[2]
    pps = min(pps, n_pages)
    nc = n_pages // pps
    vh = w_uv.shape[1] // H

    W = pps * page
    hbm = pl.BlockSpec(memory_space=pl.ANY)
    in_specs = [pl.BlockSpec((1, R, C), lambda b, c, pt: (b, 0, 0)),
                pl.BlockSpec((1, R, Rr), lambda b, c, pt: (b, 0, 0)),
                hbm, hbm,
                pl.BlockSpec((1, ts, C), lambda b, c, pt: (b, 0, 0)),
                pl.BlockSpec((1, ts, Rr), lambda b, c, pt: (b, 0, 0)),
                pl.BlockSpec(w_uv.shape, lambda b, c, pt: (0, 0))]
    grid_spec = pltpu.PrefetchScalarGridSpec(
        num_scalar_prefetch=1, grid=(Bs, nc), in_specs=in_specs,
        out_specs=pl.BlockSpec((1, ts, H * vh), lambda b, c, pt: (b, 0, 0)),
        scratch_shapes=[pltpu.VMEM((3, W, C), F32), pltpu.VMEM((3, Rr, W), F32), pltpu.SemaphoreType.DMA((3,)),
                        pltpu.VMEM((W, C), BF16), pltpu.VMEM((Rr, W), BF16),
                        pltpu.VMEM((R, 1), F32), pltpu.VMEM((R, 1), F32), pltpu.VMEM((R, C), F32)])
    return pl.pallas_call(
        functools.partial(_mla_sample_body, pps=pps, H=H, ts=ts, page=page, nc=nc, n_steps=Bs * nc, layer=layer,
                          scale=scale, vh=vh),
        grid_spec=grid_spec, out_shape=jax.ShapeDtypeStruct((Bs, ts, H * vh), F32),
        compiler_params=_cparams(("arbitrary", "arbitrary")), name="mla_sample_attn")(
            page_table.reshape(-1), q_lat, q_pe, pool_ckv, pool_kpe_t, ckv_new, kpe_new, w_uv)


def _t5_bias_of_dist(d, rb_ref, h, n_buckets):
    dist = jnp.maximum(d, 0)
    exact = n_buckets // 2
    df = jnp.maximum(dist, 1).astype(F32)
    large = exact + (jnp.log(df / exact) / math.log(REL_MAX_DIST / exact) * (n_buckets - exact)).astype(jnp.int32)
    large = jnp.minimum(large, n_buckets - 1)
    bucket = jnp.where(dist < exact, dist, large)
    out = jnp.zeros(d.shape, F32)
    for b in range(n_buckets):
        out = jnp.where(bucket == b, rb_ref[b, h], out)
    return out


def _bias_tables_body(rb_ref, t0_ref, t1_ref, tp_ref, tn_ref, *, NH, tq, ts8, n_buckets):
    r = lax.broadcasted_iota(jnp.int32, (tq, tq), 0)
    c = lax.broadcasted_iota(jnp.int32, (tq, tq), 1)
    r8 = lax.broadcasted_iota(jnp.int32, (ts8, LANES), 0)
    c8 = lax.broadcasted_iota(jnp.int32, (ts8, LANES), 1)
    for h in range(NH):
        far = rb_ref[n_buckets - 1, h]
        t0_ref[h] = _t5_bias_of_dist(r - c, rb_ref, h, n_buckets) - far
        t1_ref[h] = _t5_bias_of_dist(r - c + tq, rb_ref, h, n_buckets) - far
        tp_ref[h] = _t5_bias_of_dist(r8 - c8 + LANES, rb_ref, h, n_buckets)
        tn_ref[h] = _t5_bias_of_dist(r8 - c8, rb_ref, h, n_buckets)


def _bias_tables(rel_bias, tq, ts8):
    nb, NH = rel_bias.shape
    return pl.pallas_call(
        functools.partial(_bias_tables_body, NH=NH, tq=tq, ts8=ts8, n_buckets=nb),
        in_specs=[pl.BlockSpec(memory_space=pltpu.SMEM)],
        out_shape=[jax.ShapeDtypeStruct((NH, tq, tq), F32), jax.ShapeDtypeStruct((NH, tq, tq), F32),
                   jax.ShapeDtypeStruct((NH, ts8, LANES), F32), jax.ShapeDtypeStruct((NH, ts8, LANES), F32)],
        compiler_params=pltpu.CompilerParams(vmem_limit_bytes=VMEM_LIMIT_MB * 1024 * 1024),
        name="t5_bias_tables")(rel_bias)


def _bisect(lo0, hi0, count_ge, k_sel):
    def body(_, lh):
        lo, hi = lh
        mid = 0.5 * (lo + hi)
        ge = count_ge(mid) >= k_sel
        return jnp.where(ge, mid, lo), jnp.where(ge, hi, mid)

    lo, _ = lax.fori_loop(0, BISECT_ITERS, body, (lo0, hi0))
    return lo


def _dsa_prompt_body(q_ref, qi_ref, wi_ref, k_ref, v_ref, kit_ref, t0_ref, t1_ref, o_ref,
                     sc_ref, wb_ref, s_ref, lp_ref, acc_ref, *, tq, nq, k_sel, IH, ID, KVH, G, HD, scale):
    i = pl.program_id(1)
    row_tok = i * tq + lax.broadcasted_iota(jnp.int32, (tq, tq), 0)
    col_in = lax.broadcasted_iota(jnp.int32, (tq, tq), 1)
    qi = qi_ref[...]
    wi = wi_ref[...]
    for h in range(IH):
        wb_ref[h] = jnp.broadcast_to(wi[:, h:h + 1], (tq, tq))

    def idx_block(c, carry):
        lo, hi = carry
        kit = kit_ref[0, c]
        tot = jnp.zeros((tq, tq), F32)
        for h in range(IH):
            d = _dot(qi[:, h * ID:(h + 1) * ID], kit)
            tot = tot + jnp.maximum(d, 0.0) * wb_ref[h]
        ok = (c * tq + col_in) <= row_tok
        sc_ref[c] = jnp.where(ok, tot, NEG_INF)
        lo = jnp.minimum(lo, jnp.min(jnp.where(ok, tot, jnp.inf), axis=-1, keepdims=True))
        hi = jnp.maximum(hi, jnp.max(jnp.where(ok, tot, NEG_INF), axis=-1, keepdims=True))
        return lo, hi

    lo0, hi0 = lax.fori_loop(0, i + 1, idx_block,
                             (jnp.full((tq, 1), jnp.inf, F32), jnp.full((tq, 1), NEG_INF, F32)))

    def count_ge(mid):
        def blk(c, acc):
            t = sc_ref[c]
            part = jnp.zeros((tq, LANES), F32)
            for j in range(tq // LANES):
                part = part + jnp.where(t[:, j * LANES:(j + 1) * LANES] >= mid, 1.0, 0.0)
            return acc + part
        acc = lax.fori_loop(0, i + 1, blk, jnp.zeros((tq, LANES), F32))
        return jnp.sum(acc, axis=-1, keepdims=True)

    thr = _bisect(lo0, hi0, count_ge, float(k_sel))

    def to_mask(c, _):
        sc_ref[c] = jnp.where(sc_ref[c] >= thr, 0.0, NEG_INF)
        return 0

    lax.fori_loop(0, i + 1, to_mask, 0)

    q = q_ref[...]
    R = G * tq
    nlv = tq // LANES

    def lane_fold(x, op):
        r = x[:, :LANES]
        for jj in range(1, nlv):
            r = op(r, x[:, jj * LANES:(jj + 1) * LANES])
        return r

    for n in range(KVH):
        qn = jnp.concatenate([q[:, (n * G + g) * HD:(n * G + g + 1) * HD] for g in range(G)], axis=0)

        def score(c, slot, bias):
            kc = k_ref[0, pl.ds(pl.multiple_of(c * tq, tq), tq), n * HD:(n + 1) * HD]
            s = _dot_nt(qn, kc) * scale + jnp.concatenate([sc_ref[c]] * G, axis=0)
            if bias is not None:
                s = s + bias
            s_ref[slot] = s
            return lane_fold(s, jnp.maximum)

        mx = lax.fori_loop(0, jnp.maximum(i - 1, 0), lambda c, mx: jnp.maximum(mx, score(c, c, None)),
                           jnp.full((R, LANES), NEG_INF, F32))
        prev = score(jnp.maximum(i - 1, 0), jnp.where(i >= 1, i - 1, nq), t1_ref[n * G:(n + 1) * G].reshape(R, tq))
        mx = jnp.where(i >= 1, jnp.maximum(mx, prev), mx)
        mx = jnp.maximum(mx, score(i, i, t0_ref[n * G:(n + 1) * G].reshape(R, tq)))
        m = jnp.max(mx, axis=-1, keepdims=True)

        lp_ref[...] = jnp.zeros(lp_ref.shape, F32)
        acc_ref[...] = jnp.zeros(acc_ref.shape, F32)

        def pv(c, _):
            p = jnp.exp(s_ref[c] - m)
            lp_ref[...] += lane_fold(p, jnp.add)
            vc = v_ref[0, pl.ds(pl.multiple_of(c * tq, tq), tq), n * HD:(n + 1) * HD]
            acc_ref[...] += _dot(p.astype(BF16), vc)
            return 0

        lax.fori_loop(0, i + 1, pv, 0)
        o = acc_ref[...] / jnp.sum(lp_ref[...], axis=-1, keepdims=True)
        for g in range(G):
            o_ref[:, (n * G + g) * HD:(n * G + g + 1) * HD] = o[g * tq:(g + 1) * tq].astype(o_ref.dtype)


def _dsa_prompt(q, qi, wi, k_b, v_b, ki_t, t0, t1, B, T, k_sel, IH, ID, KVH, G, HD, scale, tq):
    M = B * T
    nq = T // tq
    NH = KVH * G
    row = lambda n: pl.BlockSpec((tq, n), lambda b, i: (b * nq + i, 0))
    seq = lambda n: pl.BlockSpec((1, T, n), lambda b, i: (b, 0, 0))
    tab = pl.BlockSpec((NH, tq, tq), lambda b, i: (0, 0, 0))
    return pl.pallas_call(
        functools.partial(_dsa_prompt_body, tq=tq, nq=nq, k_sel=k_sel, IH=IH, ID=ID, KVH=KVH, G=G, HD=HD,
                          scale=scale),
        grid=(B, nq),
        in_specs=[row(NH * HD), row(IH * ID), row(IH), seq(KVH * HD), seq(KVH * HD),
                  pl.BlockSpec((1, nq, ID, tq), lambda b, i: (b, 0, 0, 0)), tab, tab],
        out_specs=row(NH * HD),
        scratch_shapes=[pltpu.VMEM((nq, tq, tq), F32), pltpu.VMEM((IH, tq, tq), F32),
                        pltpu.VMEM((nq + 1, G * tq, tq), F32), pltpu.VMEM((G * tq, LANES), F32),
                        pltpu.VMEM((G * tq, HD), F32)],
        out_shape=jax.ShapeDtypeStruct((M, NH * HD), BF16),
        compiler_params=_cparams(("parallel", "arbitrary")), name="dsa_prompt")(
            q, qi, wi, k_b.reshape(B, T, -1), v_b.reshape(B, T, -1), ki_t, t0, t1)


def _dsa_sample_index_body(pt_ref, qi_ref, wi_ref, ik_pool, kn_ref, sc_ref, scn_ref, ibuf, sem,
                           *, pps, IH, ts, ts8, page, n_steps, layer):
    c = pl.program_id(1)
    slot = _fetch_pages(pt_ref, [(ik_pool, ibuf, True)], sem, pps, page, layer, n_steps)
    qi = qi_ref[0]
    wcol = wi_ref[0]
    kit = ibuf[slot].astype(BF16)
    d = jnp.maximum(_dot(qi, kit), 0.0) * wcol
    tot = d[0:ts8]
    for h in range(1, IH):
        tot = tot + d[h * ts8:(h + 1) * ts8]
    sc_ref[0] = tot

    @pl.when(c == 0)
    def _():
        cols = _new_key_scores([qi.astype(F32)], [kn_ref[0]], ts)
        ncols = []
        for col in cols:
            col = jnp.maximum(col, 0.0) * wcol
            t = col[0:ts8]
            for h in range(1, IH):
                t = t + col[h * ts8:(h + 1) * ts8]
            ncols.append(t)
        blk = _cols_to_block(ncols, ts8)
        qt = lax.broadcasted_iota(jnp.int32, (ts8, LANES), 0)
        lane = lax.broadcasted_iota(jnp.int32, (ts8, LANES), 1)
        scn_ref[0] = jnp.where(lane <= qt, blk, NEG_INF)


def _dsa_sample_index(qi_rows, wi_rows, pool_ik_t, layer, page_table, ki_new, IH, ts, ts8, pps=64):
    Bs, R, ID = qi_rows.shape
    n_pages = page_table.shape[1]
    page = pool_ik_t.shape[3]
    pps = min(pps, n_pages)
    nc = n_pages // pps
    W = pps * page

    in_specs = [pl.BlockSpec((1, R, ID), lambda b, c, pt: (b, 0, 0)),
                pl.BlockSpec((1, R, 1), lambda b, c, pt: (b, 0, 0)),
                pl.BlockSpec(memory_space=pl.ANY),
                pl.BlockSpec((1, ts, ID), lambda b, c, pt: (b, 0, 0))]
    grid_spec = pltpu.PrefetchScalarGridSpec(
        num_scalar_prefetch=1, grid=(Bs, nc), in_specs=in_specs,
        out_specs=[pl.BlockSpec((1, ts8, W), lambda b, c, pt: (b, 0, c)),
                   pl.BlockSpec((1, ts8, LANES), lambda b, c, pt: (b, 0, 0))],
        scratch_shapes=[pltpu.VMEM((3, ID, W), F32), pltpu.SemaphoreType.DMA((3,))])
    return pl.pallas_call(
        functools.partial(_dsa_sample_index_body, pps=pps, IH=IH, ts=ts, ts8=ts8, page=page, n_steps=Bs * nc,
                          layer=layer),
        grid_spec=grid_spec,
        out_shape=[jax.ShapeDtypeStruct((Bs, ts8, nc * W), F32), jax.ShapeDtypeStruct((Bs, ts8, LANES), F32)],
        compiler_params=_cparams(("arbitrary", "arbitrary")), name="dsa_sample_index")(
            page_table.reshape(-1), qi_rows, wi_rows, pool_ik_t, ki_new)


def _topk_threshold_body(sc_ref, scn_ref, thr_ref, *, k_sel):
    n_groups = sc_ref.shape[2] // LANES
    scn = scn_ref[...]
    lo0 = jnp.min(jnp.where(scn > NEG_INF, scn, jnp.inf), axis=-1, keepdims=True)
    hi0 = jnp.max(scn, axis=-1, keepdims=True)
    lo_g = hi_g = sc_ref[:, :, :LANES]
    for j in range(1, n_groups):
        t = sc_ref[:, :, j * LANES:(j + 1) * LANES]
        lo_g = jnp.minimum(lo_g, t)
        hi_g = jnp.maximum(hi_g, t)
    lo0 = jnp.minimum(lo0, jnp.min(lo_g, axis=-1, keepdims=True))
    hi0 = jnp.maximum(hi0, jnp.max(hi_g, axis=-1, keepdims=True))

    def count_ge(mid):
        acc = jnp.where(scn >= mid, 1.0, 0.0)
        for j in range(n_groups):
            acc = acc + jnp.where(sc_ref[:, :, j * LANES:(j + 1) * LANES] >= mid, 1.0, 0.0)
        return jnp.sum(acc, axis=-1, keepdims=True)

    thr = _bisect(lo0, hi0, count_ge, float(k_sel))
    thr_ref[...] = jnp.broadcast_to(thr, thr_ref.shape)


def _topk_threshold(scores, scores_new, k_sel, bb=16):
    Bs, ts8, P = scores.shape
    bb = min(bb, Bs)
    assert Bs % bb == 0
    return pl.pallas_call(
        functools.partial(_topk_threshold_body, k_sel=k_sel), grid=(Bs // bb,),
        in_specs=[pl.BlockSpec((bb, ts8, P), lambda i: (i, 0, 0)),
                  pl.BlockSpec((bb, ts8, LANES), lambda i: (i, 0, 0))],
        out_specs=pl.BlockSpec((bb, ts8, LANES), lambda i: (i, 0, 0)),
        out_shape=jax.ShapeDtypeStruct((Bs, ts8, LANES), F32),
        compiler_params=_cparams(("parallel",)), name="topk_threshold")(scores, scores_new)


def _dsa_sample_attn_body(pt_ref, far_ref, q_ref, sc_ref, scn_ref, thr_ref, k_pool, v_pool, kn_ref, vn_ref,
                          tp_ref, tn_ref, o_ref, kbuf, vbuf, sem, kb_ref, vb_ref, m_ref, l_ref, acc_ref,
                          *, pps, KVH, G, HD, ts, ts8, page, nc, n_steps, layer, scale):
    c = pl.program_id(1)
    R = G * ts8
    W = pps * page
    slot = _fetch_pages(pt_ref, [(k_pool, kbuf, False), (v_pool, vbuf, False)], sem, pps, page, layer, n_steps)

    @pl.when(c == 0)
    def _():
        m_ref[...] = jnp.full(m_ref.shape, NEG_INF, F32)
        l_ref[...] = jnp.zeros(l_ref.shape, F32)
        acc_ref[...] = jnp.zeros(acc_ref.shape, F32)

    for n in range(KVH):
        kb_ref[:, n * HD:(n + 1) * HD] = kbuf[slot, pl.ds(n, W, stride=KVH), :].astype(BF16)
        vb_ref[:, n * HD:(n + 1) * HD] = vbuf[slot, pl.ds(n, W, stride=KVH), :].astype(BF16)
    thr = thr_ref[0][:, :1]
    msk = jnp.concatenate([jnp.where(sc_ref[0] >= thr, 0.0, NEG_INF)] * G, axis=0)
    last = c == nc - 1
    for n in range(KVH):
        qn = q_ref[0, n]
        far = jnp.concatenate([jnp.full((ts8, 1), far_ref[n * G + g], F32) for g in range(G)], axis=0)
        near = tp_ref[n * G:(n + 1) * G].reshape(R, LANES)
        delta = jnp.where(last, near - far, 0.0)
        s = _dot_nt(qn, kb_ref[:, n * HD:(n + 1) * HD]) * scale + far + msk
        s = jnp.concatenate([s[:, :W - LANES], s[:, W - LANES:] + delta], axis=1)
        _flash_update(s, vb_ref[:, n * HD:(n + 1) * HD], m_ref.at[n], l_ref.at[n], acc_ref.at[n])

    @pl.when(last)
    def _():
        mskn = jnp.concatenate([jnp.where(scn_ref[0] >= thr, 0.0, NEG_INF)] * G, axis=0)
        outs = []
        for n in range(KVH):
            qn = q_ref[0, n].astype(F32)
            kn = kn_ref[0][:, n * HD:(n + 1) * HD]
            vn = vn_ref[0][:, n * HD:(n + 1) * HD]
            cols = _new_key_scores([qn], [kn], ts)
            blk = _cols_to_block(cols, R) * scale + tn_ref[n * G:(n + 1) * G].reshape(R, LANES) + mskn
            _flash_update_new(blk, vn, ts, m_ref.at[n], l_ref.at[n], acc_ref.at[n])
            o = acc_ref[n] / l_ref[n]
            outs += [o[g * ts8:g * ts8 + ts] for g in range(G)]
        o_ref[0] = jnp.concatenate(outs, axis=1)


def _dsa_sample_attn(q_rows, scores, scores_new, thr, pool_k, pool_v, layer, page_table, k_new, v_new, tp, tn,
                     far_bias, KVH, G, HD, ts, ts8, scale, pps=32):
    Bs = q_rows.shape[0]
    R = G * ts8
    n_pages = page_table.shape[1]
    page = pool_k.shape[2] // KVH
    pps = min(pps, n_pages)
    nc = n_pages // pps
    W = pps * page
    NH = KVH * G

    hbm = pl.BlockSpec(memory_space=pl.ANY)
    in_specs = [pl.BlockSpec((1, KVH, R, HD), lambda b, c, pt, f: (b, 0, 0, 0)),
                pl.BlockSpec((1, ts8, W), lambda b, c, pt, f: (b, 0, c)),
                pl.BlockSpec((1, ts8, LANES), lambda b, c, pt, f: (b, 0, 0)),
                pl.BlockSpec((1, ts8, LANES), lambda b, c, pt, f: (b, 0, 0)),
                hbm, hbm]
    in_specs += [pl.BlockSpec((1, ts, KVH * HD), lambda b, c, pt, f: (b, 0, 0))] * 2
    in_specs += [pl.BlockSpec((NH, ts8, LANES), lambda b, c, pt, f: (0, 0, 0))] * 2
    grid_spec = pltpu.PrefetchScalarGridSpec(
        num_scalar_prefetch=2, grid=(Bs, nc), in_specs=in_specs,
        out_specs=pl.BlockSpec((1, ts, NH * HD), lambda b, c, pt, f: (b, 0, 0)),
        scratch_shapes=[pltpu.VMEM((2, W * KVH, HD), F32), pltpu.VMEM((2, W * KVH, HD), F32),
                        pltpu.SemaphoreType.DMA((2,)),
                        pltpu.VMEM((W, KVH * HD), BF16), pltpu.VMEM((W, KVH * HD), BF16),
                        pltpu.VMEM((KVH, R, 1), F32), pltpu.VMEM((KVH, R, 1), F32), pltpu.VMEM((KVH, R, HD), F32)])
    return pl.pallas_call(
        functools.partial(_dsa_sample_attn_body, pps=pps, KVH=KVH, G=G, HD=HD, ts=ts, ts8=ts8, page=page,
                          nc=nc, n_steps=Bs * nc, layer=layer, scale=scale),
        grid_spec=grid_spec, out_shape=jax.ShapeDtypeStruct((Bs, ts, NH * HD), F32),
        compiler_params=_cparams(("arbitrary", "arbitrary")), name="dsa_sample_attn")(
            page_table.reshape(-1), far_bias, q_rows, scores, scores_new, thr, pool_k, pool_v,
            k_new, v_new, tp, tn)


def _rot_tables(pos, half, group, extra=None):
    lane = np.arange(LANES)
    g = lane % group
    rotary = g < 2 * half
    freq = np.float32(ROPE_THETA) ** (-np.arange(half, dtype=np.float32) / np.float32(half))
    ang = pos.astype(np.float32)[:, None] * freq[None, :].astype(np.float32)
    cos = np.cos(ang)[:, g % half]
    sin = np.sin(ang)[:, g % half]
    c_tab = np.where(rotary[None, :], cos, np.float32(1.0))
    s_tab = np.where(rotary[None, :], np.where((g < half)[None, :], -sin, sin), np.float32(0.0))
    if extra is not None:
        a, b, val = extra
        c_tab = np.where(((lane >= a) & (lane < b))[None, :], np.float32(val), c_tab)
    return jnp.asarray(c_tab, F32), jnp.asarray(s_tab, F32)


def kernel(x_prompt, x_sample, cache_mla_ckv, cache_mla_kpe, cache_dsa_k, cache_dsa_v, cache_dsa_idx_k,
           page_table, c_prompt, c_sample, w_ada, b_ada, ln_g, ln_b,
           mla_w_in, mla_q_norm, mla_kv_norm, mla_w_uq, mla_w_uk, mla_w_uv, mla_w_o,
           dsa_w_in, dsa_w_o, rel_bias, ffn_w_gu, ffn_w_down):
    B, T, D = x_prompt.shape
    Bs, ts, _ = x_sample.shape
    depth = w_ada.shape[0]
    alpha = (2 * depth) ** 0.25
    n_pages = page_table.shape[1]
    page = cache_mla_ckv.shape[2]
    past = n_pages * page
    Mp, Ms = B * T, Bs * ts
    ts8 = 8 * (-(-ts // 8))

    ql = mla_q_norm.shape[1]
    kvl = cache_mla_ckv.shape[-1]
    rope = cache_mla_kpe.shape[-1]
    H = mla_w_uk.shape[2]
    nope = mla_w_uk.shape[3]
    vh = mla_w_uv.shape[3]
    mla_scale = (nope + rope) ** -0.5
    KVH, HD = cache_dsa_k.shape[-2:]
    ID = cache_dsa_idx_k.shape[-1]
    NH = dsa_w_o.shape[1] // HD
    G = NH // KVH
    dq, dkv = NH * HD, KVH * HD
    IH = (dsa_w_in.shape[2] - dq - 2 * dkv - ID) // (ID + 1)
    dsa_scale = HD ** -0.5
    idx_wscale = (IH ** -0.5) * (ID ** -0.5)
    tq_dsa = min(256, T)

    pos_p = np.arange(T, dtype=np.int32)
    pos_s = np.tile(past + np.arange(ts, dtype=np.int32), Bs)
    tabs = {
        "mla": (_rot_tables(pos_p, rope // 2, rope), _rot_tables(pos_s, rope // 2, rope)),
        "idx": (_rot_tables(pos_p, IDX_ROPE // 2, ID), _rot_tables(pos_s, IDX_ROPE // 2, ID)),
        "tail": (_rot_tables(pos_p, IDX_ROPE // 2, LANES, (ID, ID + IH, idx_wscale)),
                 _rot_tables(pos_s, IDX_ROPE // 2, LANES, (ID, ID + IH, idx_wscale))),
    }

    c_all = jnp.concatenate([c_prompt, c_sample], axis=0)
    nc_rows = c_all.shape[0]
    c_all = jnp.pad(c_all, ((0, -nc_rows % 8), (0, 0)))
    mods = [_mm(c_all, w_ada, layer=i, bias=b_ada[i].reshape(1, -1), act="silu", tm=c_all.shape[0], tn=1024,
                name="ada_mod") for i in range(depth)]

    pool_kpe_t = jnp.swapaxes(cache_mla_kpe, 2, 3)
    pool_ik_t = jnp.swapaxes(cache_dsa_idx_k, 2, 3)
    pool_k = cache_dsa_k.reshape(cache_dsa_k.shape[0], -1, page * cache_dsa_k.shape[3], cache_dsa_k.shape[4])
    pool_v = cache_dsa_v.reshape(pool_k.shape)

    def mod_vectors(i):
        m = mods[i]
        mp = [m[:B, j * D:(j + 1) * D].reshape(B, 1, D) for j in range(6)]
        ms = [jnp.repeat(m[B:B + Bs, j * D:(j + 1) * D], ts, axis=0) for j in range(6)]
        return mp, ms

    groups = [dict(y=x_prompt.reshape(Mp, D), rpg=T, nb=B, tb=T, g=0),
              dict(y=x_sample.reshape(Ms, D), rpg=None, nb=1, tb=Ms, g=1)]
    modv = [mod_vectors(i) for i in range(depth)]
    for grp in groups:
        sh1, sc1 = modv[0][grp["g"]][0], modv[0][grp["g"]][1]
        grp["u"] = _modulate(grp["y"], sh1, sc1, grp["rpg"])

    t0 = t1 = tp = tn = far_bias = None
    if depth > 1:
        t0, t1, tp, tn = _bias_tables(rel_bias, tq_dsa, ts8)
        far_bias = rel_bias[-1]

    outs = {k: [[], []] for k in ("ckv", "kpe", "k", "v", "ik")}
    for i in range(depth):
        j = i // 2
        if i % 2 == 0:
            w_in_pad = jnp.pad(mla_w_in[j], ((0, 0), (0, LANES - rope))).astype(BF16)
            w_uq = mla_w_uq[j].reshape(ql, H, nope + rope)
            w_nope = w_uq[:, :, :nope].reshape(ql, H * nope).astype(BF16)
            w_pe = w_uq[:, :, nope:].reshape(ql, H * rope).astype(BF16)
            w_uk_t = jnp.transpose(mla_w_uk[j], (1, 2, 0)).astype(BF16)
            w_uv = mla_w_uv[j].reshape(kvl, H * vh).astype(BF16)
            w_o = mla_w_o[j].astype(BF16)
        else:
            w_in = dsa_w_in[j].astype(BF16)
            w_tail = jnp.pad(dsa_w_in[j][:, dq + 2 * dkv + IH * ID:], ((0, 0), (0, LANES - ID - IH))).astype(BF16)
            w_o = dsa_w_o[j].astype(BF16)
        w_down = ffn_w_down[i].astype(BF16)

        for grp in groups:
            g = grp["g"]
            u = grp["u"]
            mv = modv[i][g]
            if i % 2 == 0:
                ct, st = tabs["mla"][g]
                cq, ckv, ckv_b, kpe, kpe_b = _mla_in(u, w_in_pad, mla_q_norm[j], mla_kv_norm[j], ct, st,
                                                     ql, kvl, rope)
                q_lat, q_pe = _mla_q(cq, w_nope, w_pe, w_uk_t, ct, st, grp["nb"], grp["tb"], H, nope, rope, kvl)
                if g == 0:
                    o = _mla_prompt_attn(q_lat, q_pe, ckv_b.reshape(B, T, kvl), kpe_b.reshape(B, T, rope),
                                         w_uv, mla_scale)
                else:
                    def rows(a):
                        n = a.shape[-1]
                        return a.reshape(H, Bs, ts, n).transpose(1, 0, 2, 3).reshape(Bs, H * ts, n)
                    o = _mla_sample_attn(rows(q_lat), rows(q_pe), cache_mla_ckv, pool_kpe_t, j, page_table,
                                         ckv.reshape(Bs, ts, kvl), kpe.reshape(Bs, ts, rope), w_uv, mla_scale, H)
                    o = o.reshape(Ms, H * vh).astype(BF16)
                outs["ckv"][g].append(ckv)
                outs["kpe"][g].append(kpe)
            else:
                q = _mm(u, w_in, n_cols=dq, col0=0, out_dtype=BF16, name="dsa_q")
                k, k_b = _mm(u, w_in, n_cols=dkv, col0=dq, second_dtype=BF16, name="dsa_k")
                v, v_b = _mm(u, w_in, n_cols=dkv, col0=dq + dkv, second_dtype=BF16, name="dsa_v")
                ci, si = tabs["idx"][g]
                qi = _mm(u, w_in, n_cols=IH * ID, col0=dq + 2 * dkv, out_dtype=BF16,
                         rot=(ci, si, IDX_ROPE // 2), name="dsa_qi")
                ctl, stl = tabs["tail"][g]
                tail = _mm(u, w_tail, rot=(ctl, stl, IDX_ROPE // 2), name="dsa_tail")
                ki = tail[:, :ID]
                wi = tail[:, ID:ID + IH]
                if g == 0:
                    ki_t = jnp.swapaxes(ki.astype(BF16).reshape(B, T // tq_dsa, tq_dsa, ID), 2, 3)
                    o = _dsa_prompt(q, qi, wi, k_b, v_b, ki_t, t0, t1, B, T,
                                    min(TOPK_MAX, T // 4), IH, ID, KVH, G, HD, dsa_scale, tq_dsa)
                else:
                    pad_t = ((0, 0), (0, 0), (0, ts8 - ts), (0, 0))
                    qi_rows = jnp.pad(qi.reshape(Bs, ts, IH, ID).transpose(0, 2, 1, 3), pad_t
                                      ).reshape(Bs, IH * ts8, ID)
                    wi_rows = jnp.pad(wi.reshape(Bs, ts, IH).transpose(0, 2, 1), ((0, 0), (0, 0), (0, ts8 - ts))
                                      ).reshape(Bs, IH * ts8, 1)
                    sc, sc_new = _dsa_sample_index(qi_rows, wi_rows, pool_ik_t, j, page_table,
                                                   ki.reshape(Bs, ts, ID), IH, ts, ts8)
                    thr = _topk_threshold(sc, sc_new, min(TOPK_MAX, (past + ts) // 4))
                    q_rows = jnp.pad(q.reshape(Bs, ts, NH, HD).transpose(0, 2, 1, 3), pad_t
                                     ).reshape(Bs, KVH, G * ts8, HD)
                    o = _dsa_sample_attn(q_rows, sc, sc_new, thr, pool_k, pool_v, j,
                                         page_table, k.reshape(Bs, ts, dkv), v.reshape(Bs, ts, dkv), tp, tn, far_bias,
                                         KVH, G, HD, ts, ts8, dsa_scale)
                    o = o.reshape(Ms, NH * HD).astype(BF16)
                outs["k"][g].append(k)
                outs["v"][g].append(v)
                outs["ik"][g].append(ki)
            y1, u2 = _mm_postnorm(o, w_o, grp["y"], mv[2], ln_g[i, 0], ln_b[i, 0], (mv[3], mv[4]), grp["rpg"],
                                  alpha, name="attn_out_postnorm")
            hmid = _swiglu_up(u2, ffn_w_gu, i)
            nxt = None
            if i + 1 < depth:
                nmv = modv[i + 1][g]
                nxt = (nmv[0], nmv[1])
            grp["y"], grp["u"] = _mm_postnorm(hmid, w_down, y1, mv[5], ln_g[i, 1], ln_b[i, 1], nxt, grp["rpg"],
                                              alpha, name="ffn_down_postnorm")

    def stack(key, g, shape):
        return jnp.stack([a.reshape(shape) for a in outs[key][g]])

    yp = groups[0]["y"].reshape(B, T, D)
    ys = groups[1]["y"].reshape(Bs, ts, D)
    return (yp, ys,
            stack("ckv", 0, (B, T, kvl)), stack("kpe", 0, (B, T, rope)),
            stack("k", 0, (B, T, KVH, HD)), stack("v", 0, (B, T, KVH, HD)), stack("ik", 0, (B, T, ID)),
            stack("ckv", 1, (Bs, ts, kvl)), stack("kpe", 1, (Bs, ts, rope)),
            stack("k", 1, (Bs, ts, KVH, HD)), stack("v", 1, (Bs, ts, KVH, HD)), stack("ik", 1, (Bs, ts, ID)))
```

```python
import functools
import math

import jax
import jax.numpy as jnp
import numpy as np
from jax import lax
from jax.experimental import pallas as pl
from jax.experimental.pallas import tpu as pltpu

F32 = jnp.float32
BF16 = jnp.bfloat16
NEG_INF = float("-inf")

LANES = 128
ROPE_THETA = 10000.0
IDX_ROPE = 32
TOPK_MAX = 256
REL_MAX_DIST = 128
LN_EPS = 1e-5
RMS_EPS = 1e-6
BISECT_ITERS = 28
VMEM_LIMIT_MB = 56
POSTNORM_VMEM_BUDGET_MB = 46


def _cparams(sem, vmem_mb=VMEM_LIMIT_MB):
    return pltpu.CompilerParams(dimension_semantics=sem, vmem_limit_bytes=vmem_mb * 1024 * 1024)


def _dot(a, b):
    return jnp.dot(a, b, preferred_element_type=F32)


def _dot_nt(a, b):
    return lax.dot_general(a, b, (((1,), (1,)), ((), ())), preferred_element_type=F32)


def _rot(v, c, s, half):
    lane = lax.broadcasted_iota(jnp.int32, v.shape, 1)
    partner = jnp.where((lane % (2 * half)) < half,
                        pltpu.roll(v, LANES - half, 1), pltpu.roll(v, half, 1))
    return v * c + partner * s


def _mm_body(*refs, nk, act, rot_half, has_bias, second):
    it = iter(refs)
    x_ref = next(it)
    w_ref = next(it)
    b_ref = next(it) if has_bias else None
    c_ref = next(it) if rot_half else None
    s_ref = next(it) if rot_half else None
    o_ref = next(it)
    o2_ref = next(it) if second else None
    acc_ref = next(it) if nk > 1 else None

    x = x_ref[...]
    if act == "silu":
        xf = x.astype(F32)
        x = xf * jax.nn.sigmoid(xf)
    part = _dot(x.astype(BF16), w_ref[...].astype(BF16))

    def finish(acc):
        if has_bias:
            acc = acc + b_ref[...]
        if rot_half:
            c = c_ref[...]
            s = s_ref[...]
            for j in range(acc.shape[1] // LANES):
                sl = slice(j * LANES, (j + 1) * LANES)
                r = _rot(acc[:, sl], c, s, rot_half)
                o_ref[:, sl] = r.astype(o_ref.dtype)
                if second:
                    o2_ref[:, sl] = r.astype(o2_ref.dtype)
        else:
            o_ref[...] = acc.astype(o_ref.dtype)
            if second:
                o2_ref[...] = acc.astype(o2_ref.dtype)

    if nk == 1:
        finish(part)
    else:
        k = pl.program_id(2)

        @pl.when(k == 0)
        def _():
            acc_ref[...] = part

        @pl.when(k > 0)
        def _():
            acc_ref[...] += part

        @pl.when(k == nk - 1)
        def _():
            finish(acc_ref[...])


def _mm(x, w, *, layer=None, n_cols=None, col0=0, tm=1024, tn=512, tk=None, out_dtype=F32, second_dtype=None,
        act=None, bias=None, rot=None, name="mm"):
    M, K = x.shape
    N = n_cols if n_cols is not None else w.shape[-1]
    tm = min(tm, M) if rot is None else min(tm, M, rot[0].shape[0])
    tn = min(tn, N)
    while N % tn or col0 % tn:
        tn //= 2
    tk = K if tk is None else min(tk, K)
    assert M % tm == 0 and K % tk == 0 and tn % LANES == 0
    nk = K // tk
    jb = col0 // tn
    if layer is None:
        w_spec = pl.BlockSpec((tk, tn), lambda i, j, k: (k, j + jb))
    else:
        w_spec = pl.BlockSpec((None, tk, tn), lambda i, j, k: (layer, k, j + jb))
    in_specs = [pl.BlockSpec((tm, tk), lambda i, j, k: (i, k)), w_spec]
    args = [x, w]
    if bias is not None:
        in_specs.append(pl.BlockSpec((1, tn), lambda i, j, k: (0, j)))
        args.append(bias)
    rot_half = 0
    if rot is not None:
        c_tab, s_tab, rot_half = rot
        nr = c_tab.shape[0] // tm
        assert c_tab.shape[0] % tm == 0
        for t in (c_tab, s_tab):
            in_specs.append(pl.BlockSpec((tm, LANES), lambda i, j, k: (i % nr, 0)))
            args.append(t)
    out_shape = [jax.ShapeDtypeStruct((M, N), out_dtype)]
    out_specs = [pl.BlockSpec((tm, tn), lambda i, j, k: (i, j))]
    if second_dtype is not None:
        out_shape.append(jax.ShapeDtypeStruct((M, N), second_dtype))
        out_specs.append(pl.BlockSpec((tm, tn), lambda i, j, k: (i, j)))
    scratch = [pltpu.VMEM((tm, tn), F32)] if nk > 1 else []
    res = pl.pallas_call(
        functools.partial(_mm_body, nk=nk, act=act, rot_half=rot_half, has_bias=bias is not None,
                          second=second_dtype is not None),
        grid=(M // tm, N // tn, nk),
        in_specs=in_specs, out_specs=out_specs, out_shape=out_shape, scratch_shapes=scratch,
        compiler_params=_cparams(("parallel", "parallel", "arbitrary")), name=name)(*args)
    return res if second_dtype is not None else res[0]


def _mod_spec(table, col, tm, n, rows_per_group):
    if rows_per_group is None:
        return pl.BlockSpec((tm, n), lambda i, *_: (i, col))
    assert table.shape[0] % 8 == 0 and rows_per_group % tm == 0
    return pl.BlockSpec((table.shape[0], n), lambda i, *_: (0, col))


def _mod_row(ref, per):
    return ref[...] if per is None else ref[pl.ds(pl.program_id(0) // per, 1), :]


def _modulate_body(y_ref, sh_ref, sc_ref, u_ref, *, per):
    u_ref[...] = (y_ref[...] * (1.0 + _mod_row(sc_ref, per)) + _mod_row(sh_ref, per)).astype(u_ref.dtype)


def _modulate(y, table, sh_col, sc_col, rows_per_group, tm=512):
    M, D = y.shape
    tm = min(tm, M)
    per = None if rows_per_group is None else rows_per_group // tm
    return pl.pallas_call(
        functools.partial(_modulate_body, per=per), grid=(M // tm,),
        in_specs=[pl.BlockSpec((tm, D), lambda i: (i, 0)), _mod_spec(table, sh_col, tm, D, rows_per_group),
                  _mod_spec(table, sc_col, tm, D, rows_per_group)],
        out_specs=pl.BlockSpec((tm, D), lambda i: (i, 0)),
        out_shape=jax.ShapeDtypeStruct((M, D), BF16),
        compiler_params=_cparams(("parallel",)), name="modulate")(y, table, table)


def _mm_postnorm_body(*refs, alpha, with_mod, per):
    if with_mod:
        x_ref, w_ref, y_ref, g_ref, lg_ref, lb_ref, sh_ref, sc_ref, yo_ref, u_ref = refs
    else:
        x_ref, w_ref, y_ref, g_ref, lg_ref, lb_ref, yo_ref = refs
    h = _dot(x_ref[...].astype(BF16), w_ref[...].astype(BF16))
    z = alpha * y_ref[...] + _mod_row(g_ref, per) * h
    mu = jnp.mean(z, axis=-1, keepdims=True)
    zc = z - mu
    var = jnp.mean(zc * zc, axis=-1, keepdims=True)
    yn = zc * lax.rsqrt(var + LN_EPS) * lg_ref[...] + lb_ref[...]
    yo_ref[...] = yn
    if with_mod:
        u_ref[...] = (yn * (1.0 + _mod_row(sc_ref, per)) + _mod_row(sh_ref, per)).astype(u_ref.dtype)


def _mm_postnorm(x, w, y, gate, ln_g, ln_b, mod, rows_per_group, alpha, name="mm_postnorm"):
    M, K = x.shape
    N = w.shape[1]

    def vmem_bytes(tm):
        return K * N * 2 + 2 * tm * K * 2 + tm * N * (2 * 4 + 2 * 4 + 2 * 2 + 4 + 4)

    tm = min(512, M)
    while tm > 8 and vmem_bytes(tm) > POSTNORM_VMEM_BUDGET_MB * 1024 * 1024:
        tm //= 2
    assert M % tm == 0
    row = pl.BlockSpec((tm, N), lambda i: (i, 0))
    vec = pl.BlockSpec((1, N), lambda i: (0, 0))
    per = None if rows_per_group is None else rows_per_group // tm
    in_specs = [pl.BlockSpec((tm, K), lambda i: (i, 0)), pl.BlockSpec((K, N), lambda i: (0, 0)),
                row, _mod_spec(gate[0], gate[1], tm, N, rows_per_group), vec, vec]
    args = [x, w, y, gate[0], ln_g.reshape(1, N), ln_b.reshape(1, N)]
    out_shape = [jax.ShapeDtypeStruct((M, N), F32)]
    out_specs = [row]
    if mod is not None:
        for table, col in mod:
            in_specs.append(_mod_spec(table, col, tm, N, rows_per_group))
            args.append(table)
        out_shape.append(jax.ShapeDtypeStruct((M, N), BF16))
        out_specs.append(row)
    res = pl.pallas_call(
        functools.partial(_mm_postnorm_body, alpha=alpha, with_mod=mod is not None, per=per),
        grid=(M // tm,), in_specs=in_specs, out_specs=out_specs, out_shape=out_shape,
        compiler_params=_cparams(("parallel",)), name=name)(*args)
    return (res[0], res[1]) if mod is not None else (res[0], None)


def _swiglu_body(x_ref, wg_ref, wu_ref, o_ref):
    x = x_ref[...]
    g = _dot(x, wg_ref[...].astype(BF16))
    u = _dot(x, wu_ref[...].astype(BF16))
    o_ref[...] = (g * jax.nn.sigmoid(g) * u).astype(o_ref.dtype)


def _swiglu_up(x, w_gu, layer, tm=1024, tn=512):
    M, K = x.shape
    F = w_gu.shape[2] // 2
    tm = min(tm, M)
    tn = min(tn, F)
    while F % tn:
        tn //= 2
    nj = F // tn
    return pl.pallas_call(
        _swiglu_body, grid=(M // tm, nj),
        in_specs=[pl.BlockSpec((tm, K), lambda i, j: (i, 0)),
                  pl.BlockSpec((None, K, tn), lambda i, j: (layer, 0, j)),
                  pl.BlockSpec((None, K, tn), lambda i, j: (layer, 0, j + nj))],
        out_specs=pl.BlockSpec((tm, tn), lambda i, j: (i, j)),
        out_shape=jax.ShapeDtypeStruct((M, F), BF16),
        compiler_params=_cparams(("parallel", "parallel")), name="swiglu_up")(x, w_gu, w_gu)


def _mla_in_body(x_ref, w_ref, qg_ref, kvg_ref, c_ref, s_ref, cq_ref, ckv_ref, ckvb_ref, kpe_ref, kpeb_ref,
                 *, ql, kvl, rope):
    acc = _dot(x_ref[...], w_ref[...])

    def rms(v, g):
        return v * lax.rsqrt(jnp.mean(v * v, axis=-1, keepdims=True) + RMS_EPS) * g

    cq_ref[...] = rms(acc[:, :ql], qg_ref[...]).astype(cq_ref.dtype)
    ckv = rms(acc[:, ql:ql + kvl], kvg_ref[...])
    ckv_ref[...] = ckv
    ckvb_ref[...] = ckv.astype(ckvb_ref.dtype)
    kpe = _rot(acc[:, ql + kvl:], c_ref[...], s_ref[...], rope // 2)[:, :rope]
    kpe_ref[...] = kpe
    kpeb_ref[...] = kpe.astype(kpeb_ref.dtype)


def _mla_in(u, w_pad, q_g, kv_g, c_tab, s_tab, ql, kvl, rope, tm=512):
    M, K = u.shape
    N = w_pad.shape[1]
    tm = min(tm, M)
    nr = c_tab.shape[0] // tm
    tab = pl.BlockSpec((tm, LANES), lambda i: (i % nr, 0))

    def row(n):
        return pl.BlockSpec((tm, n), lambda i: (i, 0))

    return pl.pallas_call(
        functools.partial(_mla_in_body, ql=ql, kvl=kvl, rope=rope), grid=(M // tm,),
        in_specs=[row(K), pl.BlockSpec((K, N), lambda i: (0, 0)),
                  pl.BlockSpec((1, ql), lambda i: (0, 0)), pl.BlockSpec((1, kvl), lambda i: (0, 0)), tab, tab],
        out_specs=[row(ql), row(kvl), row(kvl), row(rope), row(rope)],
        out_shape=[jax.ShapeDtypeStruct((M, ql), BF16), jax.ShapeDtypeStruct((M, kvl), F32),
                   jax.ShapeDtypeStruct((M, kvl), BF16), jax.ShapeDtypeStruct((M, rope), F32),
                   jax.ShapeDtypeStruct((M, rope), BF16)],
        compiler_params=_cparams(("parallel",)), name="mla_in")(
            u, w_pad, q_g.reshape(1, ql), kv_g.reshape(1, kvl), c_tab, s_tab)


def _mla_q_body(cq_ref, wn_ref, wp_ref, wuk_ref, c_ref, s_ref, ql_ref, qp_ref, *, H, nope, rope):
    cq = cq_ref[...]
    qn = _dot(cq, wn_ref[...]).astype(BF16)
    for h in range(H):
        ql_ref[0, h] = _dot(qn[:, h * nope:(h + 1) * nope], wuk_ref[h]).astype(ql_ref.dtype)
    qp = _dot(cq, wp_ref[...])
    c = c_ref[...]
    s = s_ref[...]
    per = LANES // rope
    for j in range(H // per):
        r = _rot(qp[:, j * LANES:(j + 1) * LANES], c, s, rope // 2)
        for e in range(per):
            qp_ref[0, j * per + e] = r[:, e * rope:(e + 1) * rope].astype(qp_ref.dtype)


def _mla_q(cq, w_nope, w_pe, w_uk_t, c_tab, s_tab, nb, tb, H, nope, rope, kvl, tm=512):
    M, ql = cq.shape
    tm = min(tm, tb)
    per = tb // tm
    nr = c_tab.shape[0] // tm
    tab = pl.BlockSpec((tm, LANES), lambda i: (i % nr, 0))
    full2 = lambda a: pl.BlockSpec(a.shape, lambda i: (0, 0))
    return pl.pallas_call(
        functools.partial(_mla_q_body, H=H, nope=nope, rope=rope), grid=(M // tm,),
        in_specs=[pl.BlockSpec((tm, ql), lambda i: (i, 0)), full2(w_nope), full2(w_pe),
                  pl.BlockSpec(w_uk_t.shape, lambda i: (0, 0, 0)), tab, tab],
        out_specs=[pl.BlockSpec((1, H, tm, kvl), lambda i: (i // per, 0, i % per, 0)),
                   pl.BlockSpec((1, H, tm, rope), lambda i: (i // per, 0, i % per, 0))],
        out_shape=[jax.ShapeDtypeStruct((nb, H, tb, kvl), BF16), jax.ShapeDtypeStruct((nb, H, tb, rope), BF16)],
        compiler_params=_cparams(("parallel",)), name="mla_q")(cq, w_nope, w_pe, w_uk_t, c_tab, s_tab)


def _flash_update(s, v, m_ref, l_ref, acc_ref):
    m_old = m_ref[...]
    m_new = jnp.maximum(m_old, jnp.max(s, axis=-1, keepdims=True))
    m_safe = jnp.where(m_new == NEG_INF, 0.0, m_new)
    a = jnp.exp(m_old - m_safe)
    p = jnp.exp(s - m_safe)
    l_ref[...] = a * l_ref[...] + jnp.sum(p, axis=-1, keepdims=True)
    acc_ref[...] = a * acc_ref[...] + _dot(p.astype(BF16), v)
    m_ref[...] = m_new


def _lane_fold(x, op):
    r = x[:, :LANES]
    for j in range(1, x.shape[1] // LANES):
        r = op(r, x[:, j * LANES:(j + 1) * LANES])
    return r


def _mla_prompt_body(ql_ref, qp_ref, ckv_ref, kpe_ref, wuv_ref, o_ref, s_ref, mx_ref, lp_ref, acc_ref,
                     *, H, tq, tk, scale, vh):
    i = pl.program_id(1)
    R = H * tq
    n_blk = lax.div(i * tq + tq - 1, tk) + 1

    def score(c, masked):
        ql = ql_ref[0].reshape(R, ql_ref.shape[-1])
        qp = qp_ref[0].reshape(R, qp_ref.shape[-1])
        start = pl.multiple_of(c * tk, tk)
        s = (_dot_nt(ql, ckv_ref[0, pl.ds(start, tk), :]) + _dot_nt(qp, kpe_ref[0, pl.ds(start, tk), :])) * scale
        if masked:
            tok = i * tq + (lax.broadcasted_iota(jnp.int32, s.shape, 0) % tq)
            key = c * tk + lax.broadcasted_iota(jnp.int32, s.shape, 1)
            s = jnp.where(key <= tok, s, NEG_INF)
        s_ref[c] = s
        return _lane_fold(s, jnp.maximum)

    mx_ref[...] = jnp.full(mx_ref.shape, NEG_INF, F32)

    def far(c, _):
        mx_ref[...] = jnp.maximum(mx_ref[...], score(c, False))
        return 0

    lax.fori_loop(0, n_blk - 1, far, 0)
    mx = jnp.maximum(mx_ref[...], score(n_blk - 1, True))
    m = jnp.max(mx, axis=-1, keepdims=True)

    lp_ref[...] = jnp.zeros(lp_ref.shape, F32)
    acc_ref[...] = jnp.zeros(acc_ref.shape, F32)

    def pv(c, _):
        p = jnp.exp(s_ref[c] - m)
        lp_ref[...] += _lane_fold(p, jnp.add)
        acc_ref[...] += _dot(p.astype(BF16), ckv_ref[0, pl.ds(pl.multiple_of(c * tk, tk), tk), :])
        return 0

    lax.fori_loop(0, n_blk, pv, 0)
    o = (acc_ref[...] / jnp.sum(lp_ref[...], axis=-1, keepdims=True)).astype(BF16)
    for h in range(H):
        o_ref[:, h * vh:(h + 1) * vh] = _dot(o[h * tq:(h + 1) * tq], wuv_ref[:, h * vh:(h + 1) * vh]
                                             ).astype(o_ref.dtype)


def _mla_prompt_attn(q_lat, q_pe, ckv_b, kpe_b, w_uv, scale, tq=128, tk=256):
    B, H, T, C = q_lat.shape
    R = q_pe.shape[-1]
    tq = min(tq, T)
    tk = min(tk, T)
    nq, nk = T // tq, T // tk
    vh = w_uv.shape[1] // H
    return pl.pallas_call(
        functools.partial(_mla_prompt_body, H=H, tq=tq, tk=tk, scale=scale, vh=vh),
        grid=(B, nq),
        in_specs=[pl.BlockSpec((1, H, tq, C), lambda b, i: (b, 0, i, 0)),
                  pl.BlockSpec((1, H, tq, R), lambda b, i: (b, 0, i, 0)),
                  pl.BlockSpec((1, T, C), lambda b, i: (b, 0, 0)), pl.BlockSpec((1, T, R), lambda b, i: (b, 0, 0)),
                  pl.BlockSpec(w_uv.shape, lambda b, i: (0, 0))],
        out_specs=pl.BlockSpec((tq, H * vh), lambda b, i: (b * nq + i, 0)),
        out_shape=jax.ShapeDtypeStruct((B * T, H * vh), BF16),
        scratch_shapes=[pltpu.VMEM((nk, H * tq, tk), F32), pltpu.VMEM((H * tq, LANES), F32),
                        pltpu.VMEM((H * tq, LANES), F32), pltpu.VMEM((H * tq, C), F32)],
        compiler_params=_cparams(("parallel", "arbitrary")), name="mla_prompt_attn")(
            q_lat, q_pe, ckv_b, kpe_b, w_uv)


def _new_key_scores(q_parts, k_parts, n_new):
    cols = []
    for t in range(n_new):
        acc = None
        for q, k in zip(q_parts, k_parts):
            d = jnp.sum(q * k[t:t + 1, :], axis=-1, keepdims=True)
            acc = d if acc is None else acc + d
        cols.append(acc)
    return cols


def _cols_to_block(cols, rows):
    lane = lax.broadcasted_iota(jnp.int32, (rows, LANES), 1)
    blk = jnp.full((rows, LANES), NEG_INF, F32)
    for t, c in enumerate(cols):
        blk = jnp.where(lane == t, c, blk)
    return blk


def _flash_update_new(s_blk, v_new, n_new, m_ref, l_ref, acc_ref):
    m_old = m_ref[...]
    m_new = jnp.maximum(m_old, jnp.max(s_blk, axis=-1, keepdims=True))
    m_safe = jnp.where(m_new == NEG_INF, 0.0, m_new)
    a = jnp.exp(m_old - m_safe)
    p = jnp.exp(s_blk - m_safe)
    l_ref[...] = a * l_ref[...] + jnp.sum(p, axis=-1, keepdims=True)
    acc = a * acc_ref[...]
    for t in range(n_new):
        acc = acc + p[:, t:t + 1] * v_new[t:t + 1, :]
    acc_ref[...] = acc
    m_ref[...] = m_new


def _page_copies(pt_ref, step, slot, streams, sem_ref, pps, page, layer):
    copies = []
    for r in range(pps):
        pid = pt_ref[step * pps + r]
        for pool, buf, lanes in streams:
            if lanes:
                dst = buf.at[slot, :, pl.ds(r * page, page)]
            else:
                rows = pool.shape[2]
                dst = buf.at[slot, pl.ds(r * rows, rows)]
            copies.append(pltpu.make_async_copy(pool.at[layer, pid], dst, sem_ref.at[slot]))
    return copies


def _fetch_pages(pt_ref, streams, sem_ref, pps, page, layer, n_steps):
    n_slots = streams[0][1].shape[0]
    ahead = n_slots - 1
    s = pl.program_id(0) * pl.num_programs(1) + pl.program_id(1)
    slot = s % n_slots

    @pl.when(s == 0)
    def _():
        for t in range(min(ahead, n_steps)):
            for cp in _page_copies(pt_ref, t, t, streams, sem_ref, pps, page, layer):
                cp.start()

    @pl.when(s + ahead < n_steps)
    def _():
        for cp in _page_copies(pt_ref, s + ahead, (s + ahead) % n_slots, streams, sem_ref, pps, page, layer):
            cp.start()

    for cp in _page_copies(pt_ref, s, slot, streams, sem_ref, pps, page, layer):
        cp.wait()
    return slot


def _mla_sample_body(pt_ref, ql_ref, qp_ref, ckv_pool, kpe_pool, cn_ref, kn_ref, wuv_ref, o_ref,
                     cbuf, pbuf, sem, kc_ref, kp_ref, m_ref, l_ref, acc_ref,
                     *, pps, H, ts, page, nc, n_steps, layer, scale, vh):
    c = pl.program_id(1)
    R = H * ts
    slot = _fetch_pages(pt_ref, [(ckv_pool, cbuf, False), (kpe_pool, pbuf, True)], sem, pps, page, layer, n_steps)

    @pl.when(c == 0)
    def _():
        m_ref[...] = jnp.full(m_ref.shape, NEG_INF, F32)
        l_ref[...] = jnp.zeros(l_ref.shape, F32)
        acc_ref[...] = jnp.zeros(acc_ref.shape, F32)

    kc_ref[...] = cbuf[slot].astype(BF16)
    kp_ref[...] = pbuf[slot].astype(BF16)
    ql = ql_ref[0]
    qp = qp_ref[0]
    ckv = kc_ref[...]
    s = (_dot_nt(ql, ckv) + _dot(qp, kp_ref[...])) * scale
    _flash_update(s, ckv, m_ref, l_ref, acc_ref)

    @pl.when(c == nc - 1)
    def _():
        cn = cn_ref[0]
        cols = _new_key_scores([ql.astype(F32), qp.astype(F32)], [cn, kn_ref[0]], ts)
        blk = _cols_to_block(cols, R) * scale
        qt = lax.broadcasted_iota(jnp.int32, (R, LANES), 0) % ts
        lane = lax.broadcasted_iota(jnp.int32, (R, LANES), 1)
        blk = jnp.where(lane <= qt, blk, NEG_INF)
        _flash_update_new(blk, cn, ts, m_ref, l_ref, acc_ref)
        o = (acc_ref[...] / l_ref[...]).astype(BF16)
        full = _dot(o, wuv_ref[...])
        o_ref[0] = jnp.concatenate([full[h * ts:(h + 1) * ts, h * vh:(h + 1) * vh] for h in range(H)], axis=1)


def _mla_sample_attn(q_lat, q_pe, pool_ckv, pool_kpe_t, layer, page_table, ckv_new, kpe_new, w_uv, scale, H,
                     pps=32):
    Bs, R, C = q_lat.shape
    Rr = q_pe.shape[-1]
    ts = R // H
    n_pages = page_table.shape[1]
    page = pool_ckv.shape[2]
    pps = min(pps, n_pages)
    nc = n_pages // pps
    vh = w_uv.shape[1] // H

    W = pps * page
    hbm = pl.BlockSpec(memory_space=pl.ANY)
    in_specs = [pl.BlockSpec((1, R, C), lambda b, c, pt: (b, 0, 0)),
                pl.BlockSpec((1, R, Rr), lambda b, c, pt: (b, 0, 0)),
                hbm, hbm,
                pl.BlockSpec((1, ts, C), lambda b, c, pt: (b, 0, 0)),
                pl.BlockSpec((1, ts, Rr), lambda b, c, pt: (b, 0, 0)),
                pl.BlockSpec(w_uv.shape, lambda b, c, pt: (0, 0))]
    grid_spec = pltpu.PrefetchScalarGridSpec(
        num_scalar_prefetch=1, grid=(Bs, nc), in_specs=in_specs,
        out_specs=pl.BlockSpec((1, ts, H * vh), lambda b, c, pt: (b, 0, 0)),
        scratch_shapes=[pltpu.VMEM((3, W, C), F32), pltpu.VMEM((3, Rr, W), F32), pltpu.SemaphoreType.DMA((3,)),
                        pltpu.VMEM((W, C), BF16), pltpu.VMEM((Rr, W), BF16),
                        pltpu.VMEM((R, 1), F32), pltpu.VMEM((R, 1), F32), pltpu.VMEM((R, C), F32)])
    return pl.pallas_call(
        functools.partial(_mla_sample_body, pps=pps, H=H, ts=ts, page=page, nc=nc, n_steps=Bs * nc, layer=layer,
                          scale=scale, vh=vh),
        grid_spec=grid_spec, out_shape=jax.ShapeDtypeStruct((Bs, ts, H * vh), F32),
        compiler_params=_cparams(("arbitrary", "arbitrary")), name="mla_sample_attn")(
            page_table.reshape(-1), q_lat, q_pe, pool_ckv, pool_kpe_t, ckv_new, kpe_new, w_uv)


def _t5_bias_of_dist(d, rb_ref, h, n_buckets):
    dist = jnp.maximum(d, 0)
    exact = n_buckets // 2
    df = jnp.maximum(dist, 1).astype(F32)
    large = exact + (jnp.log(df / exact) / math.log(REL_MAX_DIST / exact) * (n_buckets - exact)).astype(jnp.int32)
    large = jnp.minimum(large, n_buckets - 1)
    bucket = jnp.where(dist < exact, dist, large)
    out = jnp.zeros(d.shape, F32)
    for b in range(n_buckets):
        out = jnp.where(bucket == b, rb_ref[b, h], out)
    return out


def _bias_tables_body(rb_ref, t0_ref, t1_ref, tp_ref, tn_ref, *, NH, tq, ts8, n_buckets):
    r = lax.broadcasted_iota(jnp.int32, (tq, tq), 0)
    c = lax.broadcasted_iota(jnp.int32, (tq, tq), 1)
    r8 = lax.broadcasted_iota(jnp.int32, (ts8, LANES), 0)
    c8 = lax.broadcasted_iota(jnp.int32, (ts8, LANES), 1)
    for h in range(NH):
        far = rb_ref[n_buckets - 1, h]
        t0_ref[h] = _t5_bias_of_dist(r - c, rb_ref, h, n_buckets) - far
        t1_ref[h] = _t5_bias_of_dist(r - c + tq, rb_ref, h, n_buckets) - far
        tp_ref[h] = _t5_bias_of_dist(r8 - c8 + LANES, rb_ref, h, n_buckets)
        tn_ref[h] = _t5_bias_of_dist(r8 - c8, rb_ref, h, n_buckets)


def _bias_tables(rel_bias, tq, ts8):
    nb, NH = rel_bias.shape
    return pl.pallas_call(
        functools.partial(_bias_tables_body, NH=NH, tq=tq, ts8=ts8, n_buckets=nb),
        in_specs=[pl.BlockSpec(memory_space=pltpu.SMEM)],
        out_shape=[jax.ShapeDtypeStruct((NH, tq, tq), F32), jax.ShapeDtypeStruct((NH, tq, tq), F32),
                   jax.ShapeDtypeStruct((NH, ts8, LANES), F32), jax.ShapeDtypeStruct((NH, ts8, LANES), F32)],
        compiler_params=pltpu.CompilerParams(vmem_limit_bytes=VMEM_LIMIT_MB * 1024 * 1024),
        name="t5_bias_tables")(rel_bias)


def _bisect(lo0, hi0, count_ge, k_sel):
    def body(_, lh):
        lo, hi = lh
        mid = 0.5 * (lo + hi)
        ge = count_ge(mid) >= k_sel
        return jnp.where(ge, mid, lo), jnp.where(ge, hi, mid)

    lo, _ = lax.fori_loop(0, BISECT_ITERS, body, (lo0, hi0))
    return lo


def _dsa_prompt_body(q_ref, qi_ref, wi_ref, k_ref, v_ref, kit_ref, t0_ref, t1_ref, o_ref,
                     sc_ref, wb_ref, s_ref, lp_ref, acc_ref, *, tq, nq, k_sel, IH, ID, KVH, G, HD, scale):
    i = pl.program_id(1)
    row_tok = i * tq + lax.broadcasted_iota(jnp.int32, (tq, tq), 0)
    col_in = lax.broadcasted_iota(jnp.int32, (tq, tq), 1)
    qi = qi_ref[...]
    wi = wi_ref[...]
    for h in range(IH):
        wb_ref[h] = jnp.broadcast_to(wi[:, h:h + 1], (tq, tq))

    def idx_block(c, carry):
        lo, hi = carry
        kit = kit_ref[0, c]
        tot = jnp.zeros((tq, tq), F32)
        for h in range(IH):
            d = _dot(qi[:, h * ID:(h + 1) * ID], kit)
            tot = tot + jnp.maximum(d, 0.0) * wb_ref[h]
        ok = (c * tq + col_in) <= row_tok
        sc_ref[c] = jnp.where(ok, tot, NEG_INF)
        lo = jnp.minimum(lo, jnp.min(jnp.where(ok, tot, jnp.inf), axis=-1, keepdims=True))
        hi = jnp.maximum(hi, jnp.max(jnp.where(ok, tot, NEG_INF), axis=-1, keepdims=True))
        return lo, hi

    lo0, hi0 = lax.fori_loop(0, i + 1, idx_block,
                             (jnp.full((tq, 1), jnp.inf, F32), jnp.full((tq, 1), NEG_INF, F32)))

    def count_ge(mid):
        def blk(c, acc):
            t = sc_ref[c]
            part = jnp.zeros((tq, LANES), F32)
            for j in range(tq // LANES):
                part = part + jnp.where(t[:, j * LANES:(j + 1) * LANES] >= mid, 1.0, 0.0)
            return acc + part
        acc = lax.fori_loop(0, i + 1, blk, jnp.zeros((tq, LANES), F32))
        return jnp.sum(acc, axis=-1, keepdims=True)

    thr = _bisect(lo0, hi0, count_ge, float(k_sel))

    def to_mask(c, _):
        sc_ref[c] = jnp.where(sc_ref[c] >= thr, 0.0, NEG_INF)
        return 0

    lax.fori_loop(0, i + 1, to_mask, 0)

    q = q_ref[...]
    R = G * tq
    nlv = tq // LANES

    def lane_fold(x, op):
        r = x[:, :LANES]
        for jj in range(1, nlv):
            r = op(r, x[:, jj * LANES:(jj + 1) * LANES])
        return r

    for n in range(KVH):
        qn = jnp.concatenate([q[:, (n * G + g) * HD:(n * G + g + 1) * HD] for g in range(G)], axis=0)

        def score(c, slot, bias):
            kc = k_ref[0, pl.ds(pl.multiple_of(c * tq, tq), tq), n * HD:(n + 1) * HD]
            s = _dot_nt(qn, kc) * scale + jnp.concatenate([sc_ref[c]] * G, axis=0)
            if bias is not None:
                s = s + bias
            s_ref[slot] = s
            return lane_fold(s, jnp.maximum)

        mx = lax.fori_loop(0, jnp.maximum(i - 1, 0), lambda c, mx: jnp.maximum(mx, score(c, c, None)),
                           jnp.full((R, LANES), NEG_INF, F32))
        prev = score(jnp.maximum(i - 1, 0), jnp.where(i >= 1, i - 1, nq), t1_ref[n * G:(n + 1) * G].reshape(R, tq))
        mx = jnp.where(i >= 1, jnp.maximum(mx, prev), mx)
        mx = jnp.maximum(mx, score(i, i, t0_ref[n * G:(n + 1) * G].reshape(R, tq)))
        m = jnp.max(mx, axis=-1, keepdims=True)

        lp_ref[...] = jnp.zeros(lp_ref.shape, F32)
        acc_ref[...] = jnp.zeros(acc_ref.shape, F32)

        def pv(c, _):
            p = jnp.exp(s_ref[c] - m)
            lp_ref[...] += lane_fold(p, jnp.add)
            vc = v_ref[0, pl.ds(pl.multiple_of(c * tq, tq), tq), n * HD:(n + 1) * HD]
            acc_ref[...] += _dot(p.astype(BF16), vc)
            return 0

        lax.fori_loop(0, i + 1, pv, 0)
        o = acc_ref[...] / jnp.sum(lp_ref[...], axis=-1, keepdims=True)
        for g in range(G):
            o_ref[:, (n * G + g) * HD:(n * G + g + 1) * HD] = o[g * tq:(g + 1) * tq].astype(o_ref.dtype)


def _dsa_prompt(q, qi, wi, k_b, v_b, ki_t, t0, t1, B, T, k_sel, IH, ID, KVH, G, HD, scale, tq):
    M = B * T
    nq = T // tq
    NH = KVH * G
    row = lambda n: pl.BlockSpec((tq, n), lambda b, i: (b * nq + i, 0))
    seq = lambda n: pl.BlockSpec((1, T, n), lambda b, i: (b, 0, 0))
    tab = pl.BlockSpec((NH, tq, tq), lambda b, i: (0, 0, 0))
    return pl.pallas_call(
        functools.partial(_dsa_prompt_body, tq=tq, nq=nq, k_sel=k_sel, IH=IH, ID=ID, KVH=KVH, G=G, HD=HD,
                          scale=scale),
        grid=(B, nq),
        in_specs=[row(NH * HD), row(IH * ID), row(IH), seq(KVH * HD), seq(KVH * HD),
                  pl.BlockSpec((1, nq, ID, tq), lambda b, i: (b, 0, 0, 0)), tab, tab],
        out_specs=row(NH * HD),
        scratch_shapes=[pltpu.VMEM((nq, tq, tq), F32), pltpu.VMEM((IH, tq, tq), F32),
                        pltpu.VMEM((nq + 1, G * tq, tq), F32), pltpu.VMEM((G * tq, LANES), F32),
                        pltpu.VMEM((G * tq, HD), F32)],
        out_shape=jax.ShapeDtypeStruct((M, NH * HD), BF16),
        compiler_params=_cparams(("parallel", "arbitrary")), name="dsa_prompt")(
            q, qi, wi, k_b.reshape(B, T, -1), v_b.reshape(B, T, -1), ki_t, t0, t1)


def _dsa_sample_index_body(pt_ref, qi_ref, wi_ref, ik_pool, kn_ref, sc_ref, scn_ref, ibuf, sem,
                           *, pps, IH, ts, ts8, page, n_steps, layer):
    c = pl.program_id(1)
    slot = _fetch_pages(pt_ref, [(ik_pool, ibuf, True)], sem, pps, page, layer, n_steps)
    qi = qi_ref[0]
    wcol = wi_ref[0]
    kit = ibuf[slot].astype(BF16)
    d = jnp.maximum(_dot(qi, kit), 0.0) * wcol
    tot = d[0:ts8]
    for h in range(1, IH):
        tot = tot + d[h * ts8:(h + 1) * ts8]
    sc_ref[0] = tot

    @pl.when(c == 0)
    def _():
        cols = _new_key_scores([qi.astype(F32)], [kn_ref[0]], ts)
        ncols = []
        for col in cols:
            col = jnp.maximum(col, 0.0) * wcol
            t = col[0:ts8]
            for h in range(1, IH):
                t = t + col[h * ts8:(h + 1) * ts8]
            ncols.append(t)
        blk = _cols_to_block(ncols, ts8)
        qt = lax.broadcasted_iota(jnp.int32, (ts8, LANES), 0)
        lane = lax.broadcasted_iota(jnp.int32, (ts8, LANES), 1)
        scn_ref[0] = jnp.where(lane <= qt, blk, NEG_INF)


def _dsa_sample_index(qi_rows, wi_rows, pool_ik_t, layer, page_table, ki_new, IH, ts, ts8, pps=64):
    Bs, R, ID = qi_rows.shape
    n_pages = page_table.shape[1]
    page = pool_ik_t.shape[3]
    pps = min(pps, n_pages)
    nc = n_pages // pps
    W = pps * page

    in_specs = [pl.BlockSpec((1, R, ID), lambda b, c, pt: (b, 0, 0)),
                pl.BlockSpec((1, R, 1), lambda b, c, pt: (b, 0, 0)),
                pl.BlockSpec(memory_space=pl.ANY),
                pl.BlockSpec((1, ts, ID), lambda b, c, pt: (b, 0, 0))]
    grid_spec = pltpu.PrefetchScalarGridSpec(
        num_scalar_prefetch=1, grid=(Bs, nc), in_specs=in_specs,
        out_specs=[pl.BlockSpec((1, ts8, W), lambda b, c, pt: (b, 0, c)),
                   pl.BlockSpec((1, ts8, LANES), lambda b, c, pt: (b, 0, 0))],
        scratch_shapes=[pltpu.VMEM((3, ID, W), F32), pltpu.SemaphoreType.DMA((3,))])
    return pl.pallas_call(
        functools.partial(_dsa_sample_index_body, pps=pps, IH=IH, ts=ts, ts8=ts8, page=page, n_steps=Bs * nc,
                          layer=layer),
        grid_spec=grid_spec,
        out_shape=[jax.ShapeDtypeStruct((Bs, ts8, nc * W), F32), jax.ShapeDtypeStruct((Bs, ts8, LANES), F32)],
        compiler_params=_cparams(("arbitrary", "arbitrary")), name="dsa_sample_index")(
            page_table.reshape(-1), qi_rows, wi_rows, pool_ik_t, ki_new)


def _topk_threshold_body(sc_ref, scn_ref, thr_ref, *, k_sel):
    n_groups = sc_ref.shape[2] // LANES
    scn = scn_ref[...]
    lo0 = jnp.min(jnp.where(scn > NEG_INF, scn, jnp.inf), axis=-1, keepdims=True)
    hi0 = jnp.max(scn, axis=-1, keepdims=True)
    lo_g = hi_g = sc_ref[:, :, :LANES]
    for j in range(1, n_groups):
        t = sc_ref[:, :, j * LANES:(j + 1) * LANES]
        lo_g = jnp.minimum(lo_g, t)
        hi_g = jnp.maximum(hi_g, t)
    lo0 = jnp.minimum(lo0, jnp.min(lo_g, axis=-1, keepdims=True))
    hi0 = jnp.maximum(hi0, jnp.max(hi_g, axis=-1, keepdims=True))

    def count_ge(mid):
        acc = jnp.where(scn >= mid, 1.0, 0.0)
        for j in range(n_groups):
            acc = acc + jnp.where(sc_ref[:, :, j * LANES:(j + 1) * LANES] >= mid, 1.0, 0.0)
        return jnp.sum(acc, axis=-1, keepdims=True)

    thr = _bisect(lo0, hi0, count_ge, float(k_sel))
    thr_ref[...] = jnp.broadcast_to(thr, thr_ref.shape)


def _topk_threshold(scores, scores_new, k_sel, bb=16):
    Bs, ts8, P = scores.shape
    bb = min(bb, Bs)
    assert Bs % bb == 0
    return pl.pallas_call(
        functools.partial(_topk_threshold_body, k_sel=k_sel), grid=(Bs // bb,),
        in_specs=[pl.BlockSpec((bb, ts8, P), lambda i: (i, 0, 0)),
                  pl.BlockSpec((bb, ts8, LANES), lambda i: (i, 0, 0))],
        out_specs=pl.BlockSpec((bb, ts8, LANES), lambda i: (i, 0, 0)),
        out_shape=jax.ShapeDtypeStruct((Bs, ts8, LANES), F32),
        compiler_params=_cparams(("parallel",)), name="topk_threshold")(scores, scores_new)


def _dsa_sample_attn_body(pt_ref, far_ref, q_ref, sc_ref, scn_ref, thr_ref, k_pool, v_pool, kn_ref, vn_ref,
                          tp_ref, tn_ref, o_ref, kbuf, vbuf, sem, kb_ref, vb_ref, m_ref, l_ref, acc_ref,
                          *, pps, KVH, G, HD, ts, ts8, page, nc, n_steps, layer, scale):
    c = pl.program_id(1)
    R = G * ts8
    W = pps * page
    slot = _fetch_pages(pt_ref, [(k_pool, kbuf, False), (v_pool, vbuf, False)], sem, pps, page, layer, n_steps)

    @pl.when(c == 0)
    def _():
        m_ref[...] = jnp.full(m_ref.shape, NEG_INF, F32)
        l_ref[...] = jnp.zeros(l_ref.shape, F32)
        acc_ref[...] = jnp.zeros(acc_ref.shape, F32)

    for n in range(KVH):
        kb_ref[:, n * HD:(n + 1) * HD] = kbuf[slot, pl.ds(n, W, stride=KVH), :].astype(BF16)
        vb_ref[:, n * HD:(n + 1) * HD] = vbuf[slot, pl.ds(n, W, stride=KVH), :].astype(BF16)
    thr = thr_ref[0][:, :1]
    msk = jnp.concatenate([jnp.where(sc_ref[0] >= thr, 0.0, NEG_INF)] * G, axis=0)
    last = c == nc - 1
    for n in range(KVH):
        qn = q_ref[0, n]
        far = jnp.concatenate([jnp.full((ts8, 1), far_ref[n * G + g], F32) for g in range(G)], axis=0)
        near = tp_ref[n * G:(n + 1) * G].reshape(R, LANES)
        delta = jnp.where(last, near - far, 0.0)
        s = _dot_nt(qn, kb_ref[:, n * HD:(n + 1) * HD]) * scale + far + msk
        s = jnp.concatenate([s[:, :W - LANES], s[:, W - LANES:] + delta], axis=1)
        _flash_update(s, vb_ref[:, n * HD:(n + 1) * HD], m_ref.at[n], l_ref.at[n], acc_ref.at[n])

    @pl.when(last)
    def _():
        mskn = jnp.concatenate([jnp.where(scn_ref[0] >= thr, 0.0, NEG_INF)] * G, axis=0)
        outs = []
        for n in range(KVH):
            qn = q_ref[0, n].astype(F32)
            kn = kn_ref[0][:, n * HD:(n + 1) * HD]
            vn = vn_ref[0][:, n * HD:(n + 1) * HD]
            cols = _new_key_scores([qn], [kn], ts)
            blk = _cols_to_block(cols, R) * scale + tn_ref[n * G:(n + 1) * G].reshape(R, LANES) + mskn
            _flash_update_new(blk, vn, ts, m_ref.at[n], l_ref.at[n], acc_ref.at[n])
            o = acc_ref[n] / l_ref[n]
            outs += [o[g * ts8:g * ts8 + ts] for g in range(G)]
        o_ref[0] = jnp.concatenate(outs, axis=1)


def _dsa_sample_attn(q_rows, scores, scores_new, thr, pool_k, pool_v, layer, page_table, k_new, v_new, tp, tn,
                     far_bias, KVH, G, HD, ts, ts8, scale, pps=32):
    Bs = q_rows.shape[0]
    R = G * ts8
    n_pages = page_table.shape[1]
    page = pool_k.shape[2] // KVH
    pps = min(pps, n_pages)
    nc = n_pages // pps
    W = pps * page
    NH = KVH * G

    hbm = pl.BlockSpec(memory_space=pl.ANY)
    in_specs = [pl.BlockSpec((1, KVH, R, HD), lambda b, c, pt, f: (b, 0, 0, 0)),
                pl.BlockSpec((1, ts8, W), lambda b, c, pt, f: (b, 0, c)),
                pl.BlockSpec((1, ts8, LANES), lambda b, c, pt, f: (b, 0, 0)),
                pl.BlockSpec((1, ts8, LANES), lambda b, c, pt, f: (b, 0, 0)),
                hbm, hbm]
    in_specs += [pl.BlockSpec((1, ts, KVH * HD), lambda b, c, pt, f: (b, 0, 0))] * 2
    in_specs += [pl.BlockSpec((NH, ts8, LANES), lambda b, c, pt, f: (0, 0, 0))] * 2
    grid_spec = pltpu.PrefetchScalarGridSpec(
        num_scalar_prefetch=2, grid=(Bs, nc), in_specs=in_specs,
        out_specs=pl.BlockSpec((1, ts, NH * HD), lambda b, c, pt, f: (b, 0, 0)),
        scratch_shapes=[pltpu.VMEM((2, W * KVH, HD), F32), pltpu.VMEM((2, W * KVH, HD), F32),
                        pltpu.SemaphoreType.DMA((2,)),
                        pltpu.VMEM((W, KVH * HD), BF16), pltpu.VMEM((W, KVH * HD), BF16),
                        pltpu.VMEM((KVH, R, 1), F32), pltpu.VMEM((KVH, R, 1), F32), pltpu.VMEM((KVH, R, HD), F32)])
    return pl.pallas_call(
        functools.partial(_dsa_sample_attn_body, pps=pps, KVH=KVH, G=G, HD=HD, ts=ts, ts8=ts8, page=page,
                          nc=nc, n_steps=Bs * nc, layer=layer, scale=scale),
        grid_spec=grid_spec, out_shape=jax.ShapeDtypeStruct((Bs, ts, NH * HD), F32),
        compiler_params=_cparams(("arbitrary", "arbitrary")), name="dsa_sample_attn")(
            page_table.reshape(-1), far_bias, q_rows, scores, scores_new, thr, pool_k, pool_v,
            k_new, v_new, tp, tn)


def _rot_tables(pos, half, group, extra=None):
    lane = np.arange(LANES)
    g = lane % group
    rotary = g < 2 * half
    freq = np.float32(ROPE_THETA) ** (-np.arange(half, dtype=np.float32) / np.float32(half))
    ang = pos.astype(np.float32)[:, None] * freq[None, :].astype(np.float32)
    cos = np.cos(ang)[:, g % half]
    sin = np.sin(ang)[:, g % half]
    c_tab = np.where(rotary[None, :], cos, np.float32(1.0))
    s_tab = np.where(rotary[None, :], np.where((g < half)[None, :], -sin, sin), np.float32(0.0))
    if extra is not None:
        a, b, val = extra
        c_tab = np.where(((lane >= a) & (lane < b))[None, :], np.float32(val), c_tab)
    return jnp.asarray(c_tab, F32), jnp.asarray(s_tab, F32)


def kernel(x_prompt, x_sample, cache_mla_ckv, cache_mla_kpe, cache_dsa_k, cache_dsa_v, cache_dsa_idx_k,
           page_table, c_prompt, c_sample, w_ada, b_ada, ln_g, ln_b,
           mla_w_in, mla_q_norm, mla_kv_norm, mla_w_uq, mla_w_uk, mla_w_uv, mla_w_o,
           dsa_w_in, dsa_w_o, rel_bias, ffn_w_gu, ffn_w_down):
    B, T, D = x_prompt.shape
    Bs, ts, _ = x_sample.shape
    depth = w_ada.shape[0]
    alpha = (2 * depth) ** 0.25
    n_pages = page_table.shape[1]
    page = cache_mla_ckv.shape[2]
    past = n_pages * page
    Mp, Ms = B * T, Bs * ts
    ts8 = 8 * (-(-ts // 8))

    ql = mla_q_norm.shape[1]
    kvl = cache_mla_ckv.shape[-1]
    rope = cache_mla_kpe.shape[-1]
    H = mla_w_uk.shape[2]
    nope = mla_w_uk.shape[3]
    vh = mla_w_uv.shape[3]
    mla_scale = (nope + rope) ** -0.5
    KVH, HD = cache_dsa_k.shape[-2:]
    ID = cache_dsa_idx_k.shape[-1]
    NH = dsa_w_o.shape[1] // HD
    G = NH // KVH
    dq, dkv = NH * HD, KVH * HD
    IH = (dsa_w_in.shape[2] - dq - 2 * dkv - ID) // (ID + 1)
    dsa_scale = HD ** -0.5
    idx_wscale = (IH ** -0.5) * (ID ** -0.5)
    tq_dsa = min(256, T)

    pos_p = np.arange(T, dtype=np.int32)
    pos_s = np.tile(past + np.arange(ts, dtype=np.int32), Bs)
    tabs = {
        "mla": (_rot_tables(pos_p, rope // 2, rope), _rot_tables(pos_s, rope // 2, rope)),
        "idx": (_rot_tables(pos_p, IDX_ROPE // 2, ID), _rot_tables(pos_s, IDX_ROPE // 2, ID)),
        "tail": (_rot_tables(pos_p, IDX_ROPE // 2, LANES, (ID, ID + IH, idx_wscale)),
                 _rot_tables(pos_s, IDX_ROPE // 2, LANES, (ID, ID + IH, idx_wscale))),
    }

    c_all = jnp.concatenate([c_prompt, c_sample], axis=0)
    nc_rows = c_all.shape[0]
    c_all = jnp.pad(c_all, ((0, -nc_rows % 8), (0, 0)))
    mods = [_mm(c_all, w_ada, layer=i, bias=b_ada[i].reshape(1, -1), act="silu", tm=c_all.shape[0], tn=1024,
                name="ada_mod") for i in range(depth)]

    pool_kpe_t = jnp.swapaxes(cache_mla_kpe, 2, 3)
    pool_ik_t = jnp.swapaxes(cache_dsa_idx_k, 2, 3)
    pool_k = cache_dsa_k.reshape(cache_dsa_k.shape[0], -1, page * cache_dsa_k.shape[3], cache_dsa_k.shape[4])
    pool_v = cache_dsa_v.reshape(pool_k.shape)

    modv = [(mods[i], jnp.repeat(mods[i][B:B + Bs], ts, axis=0)) for i in range(depth)]

    groups = [dict(y=x_prompt.reshape(Mp, D), rpg=T, nb=B, tb=T, g=0),
              dict(y=x_sample.reshape(Ms, D), rpg=None, nb=1, tb=Ms, g=1)]
    for grp in groups:
        grp["u"] = _modulate(grp["y"], modv[0][grp["g"]], 0, 1, grp["rpg"])

    t0 = t1 = tp = tn = far_bias = None
    if depth > 1:
        t0, t1, tp, tn = _bias_tables(rel_bias, tq_dsa, ts8)
        far_bias = rel_bias[-1]

    outs = {k: [[], []] for k in ("ckv", "kpe", "k", "v", "ik")}
    for i in range(depth):
        j = i // 2
        if i % 2 == 0:
            w_in_pad = jnp.pad(mla_w_in[j], ((0, 0), (0, LANES - rope))).astype(BF16)
            w_uq = mla_w_uq[j].reshape(ql, H, nope + rope)
            w_nope = w_uq[:, :, :nope].reshape(ql, H * nope).astype(BF16)
            w_pe = w_uq[:, :, nope:].reshape(ql, H * rope).astype(BF16)
            w_uk_t = jnp.transpose(mla_w_uk[j], (1, 2, 0)).astype(BF16)
            w_uv = mla_w_uv[j].reshape(kvl, H * vh).astype(BF16)
            w_o = mla_w_o[j].astype(BF16)
        else:
            w_in = dsa_w_in[j].astype(BF16)
            w_tail = jnp.pad(dsa_w_in[j][:, dq + 2 * dkv + IH * ID:], ((0, 0), (0, LANES - ID - IH))).astype(BF16)
            w_o = dsa_w_o[j].astype(BF16)
        w_down = ffn_w_down[i].astype(BF16)

        for grp in groups:
            g = grp["g"]
            u = grp["u"]
            mv = modv[i][g]
            if i % 2 == 0:
                ct, st = tabs["mla"][g]
                cq, ckv, ckv_b, kpe, kpe_b = _mla_in(u, w_in_pad, mla_q_norm[j], mla_kv_norm[j], ct, st,
                                                     ql, kvl, rope)
                q_lat, q_pe = _mla_q(cq, w_nope, w_pe, w_uk_t, ct, st, grp["nb"], grp["tb"], H, nope, rope, kvl)
                if g == 0:
                    o = _mla_prompt_attn(q_lat, q_pe, ckv_b.reshape(B, T, kvl), kpe_b.reshape(B, T, rope),
                                         w_uv, mla_scale)
                else:
                    def rows(a):
                        n = a.shape[-1]
                        return a.reshape(H, Bs, ts, n).transpose(1, 0, 2, 3).reshape(Bs, H * ts, n)
                    o = _mla_sample_attn(rows(q_lat), rows(q_pe), cache_mla_ckv, pool_kpe_t, j, page_table,
                                         ckv.reshape(Bs, ts, kvl), kpe.reshape(Bs, ts, rope), w_uv, mla_scale, H)
                    o = o.reshape(Ms, H * vh).astype(BF16)
                outs["ckv"][g].append(ckv)
                outs["kpe"][g].append(kpe)
            else:
                q = _mm(u, w_in, n_cols=dq, col0=0, out_dtype=BF16, name="dsa_q")
                k, k_b = _mm(u, w_in, n_cols=dkv, col0=dq, second_dtype=BF16, name="dsa_k")
                v, v_b = _mm(u, w_in, n_cols=dkv, col0=dq + dkv, second_dtype=BF16, name="dsa_v")
                ci, si = tabs["idx"][g]
                qi = _mm(u, w_in, n_cols=IH * ID, col0=dq + 2 * dkv, out_dtype=BF16,
                         rot=(ci, si, IDX_ROPE // 2), name="dsa_qi")
                ctl, stl = tabs["tail"][g]
                tail = _mm(u, w_tail, rot=(ctl, stl, IDX_ROPE // 2), name="dsa_tail")
                ki = tail[:, :ID]
                wi = tail[:, ID:ID + IH]
                if g == 0:
                    ki_t = jnp.swapaxes(ki.astype(BF16).reshape(B, T // tq_dsa, tq_dsa, ID), 2, 3)
                    o = _dsa_prompt(q, qi, wi, k_b, v_b, ki_t, t0, t1, B, T,
                                    min(TOPK_MAX, T // 4), IH, ID, KVH, G, HD, dsa_scale, tq_dsa)
                else:
                    pad_t = ((0, 0), (0, 0), (0, ts8 - ts), (0, 0))
                    qi_rows = jnp.pad(qi.reshape(Bs, ts, IH, ID).transpose(0, 2, 1, 3), pad_t
                                      ).reshape(Bs, IH * ts8, ID)
                    wi_rows = jnp.pad(wi.reshape(Bs, ts, IH).transpose(0, 2, 1), ((0, 0), (0, 0), (0, ts8 - ts))
                                      ).reshape(Bs, IH * ts8, 1)
                    sc, sc_new = _dsa_sample_index(qi_rows, wi_rows, pool_ik_t, j, page_table,
                                                   ki.reshape(Bs, ts, ID), IH, ts, ts8)
                    thr = _topk_threshold(sc, sc_new, min(TOPK_MAX, (past + ts) // 4))
                    q_rows = jnp.pad(q.reshape(Bs, ts, NH, HD).transpose(0, 2, 1, 3), pad_t
                                     ).reshape(Bs, KVH, G * ts8, HD)
                    o = _dsa_sample_attn(q_rows, sc, sc_new, thr, pool_k, pool_v, j,
                                         page_table, k.reshape(Bs, ts, dkv), v.reshape(Bs, ts, dkv), tp, tn, far_bias,
                                         KVH, G, HD, ts, ts8, dsa_scale)
                    o = o.reshape(Ms, NH * HD).astype(BF16)
                outs["k"][g].append(k)
                outs["v"][g].append(v)
                outs["ik"][g].append(ki)
            y1, u2 = _mm_postnorm(o, w_o, grp["y"], (mv, 2), ln_g[i, 0], ln_b[i, 0], ((mv, 3), (mv, 4)),
                                  grp["rpg"], alpha, name="attn_out_postnorm")
            hmid = _swiglu_up(u2, ffn_w_gu, i)
            nxt = None
            if i + 1 < depth:
                nmv = modv[i + 1][g]
                nxt = ((nmv, 0), (nmv, 1))
            grp["y"], grp["u"] = _mm_postnorm(hmid, w_down, y1, (mv, 5), ln_g[i, 1], ln_b[i, 1], nxt, grp["rpg"],
                                              alpha, name="ffn_down_postnorm")

    def stack(key, g, shape):
        return jnp.stack([a.reshape(shape) for a in outs[key][g]])

    yp = groups[0]["y"].reshape(B, T, D)
    ys = groups[1]["y"].reshape(Bs, ts, D)
    return (yp, ys,
            stack("ckv", 0, (B, T, kvl)), stack("kpe", 0, (B, T, rope)),
            stack("k", 0, (B, T, KVH, HD)), stack("v", 0, (B, T, KVH, HD)), stack("ik", 0, (B, T, ID)),
            stack("ckv", 1, (Bs, ts, kvl)), stack("kpe", 1, (Bs, ts, rope)),
            stack("k", 1, (Bs, ts, KVH, HD)), stack("v", 1, (Bs, ts, KVH, HD)), stack("ik", 1, (Bs, ts, ID)))
```

```python
import functools
import math

import jax
import jax.numpy as jnp
import numpy as np
from jax import lax
from jax.experimental import pallas as pl
from jax.experimental.pallas import tpu as pltpu

F32 = jnp.float32
BF16 = jnp.bfloat16
NEG_INF = float("-inf")

LANES = 128
ROPE_THETA = 10000.0
IDX_ROPE = 32
TOPK_MAX = 256
REL_MAX_DIST = 128
LN_EPS = 1e-5
RMS_EPS = 1e-6
BISECT_ITERS = 28
VMEM_LIMIT_MB = 56
POSTNORM_VMEM_BUDGET_MB = 46


def _cparams(sem, vmem_mb=VMEM_LIMIT_MB):
    return pltpu.CompilerParams(dimension_semantics=sem, vmem_limit_bytes=vmem_mb * 1024 * 1024)


def _dot(a, b):
    return jnp.dot(a, b, preferred_element_type=F32)


def _dot_nt(a, b):
    return lax.dot_general(a, b, (((1,), (1,)), ((), ())), preferred_element_type=F32)


def _rot(v, c, s, half):
    lane = lax.broadcasted_iota(jnp.int32, v.shape, 1)
    partner = jnp.where((lane % (2 * half)) < half,
                        pltpu.roll(v, LANES - half, 1), pltpu.roll(v, half, 1))
    return v * c + partner * s


def _mm_body(*refs, nk, act, rot_half, has_bias, second):
    it = iter(refs)
    x_ref = next(it)
    w_ref = next(it)
    b_ref = next(it) if has_bias else None
    c_ref = next(it) if rot_half else None
    s_ref = next(it) if rot_half else None
    o_ref = next(it)
    o2_ref = next(it) if second else None
    acc_ref = next(it) if nk > 1 else None

    x = x_ref[...]
    if act == "silu":
        xf = x.astype(F32)
        x = xf * jax.nn.sigmoid(xf)
    part = _dot(x.astype(BF16), w_ref[...].astype(BF16))

    def finish(acc):
        if has_bias:
            acc = acc + b_ref[...]
        if rot_half:
            c = c_ref[...]
            s = s_ref[...]
            for j in range(acc.shape[1] // LANES):
                sl = slice(j * LANES, (j + 1) * LANES)
                r = _rot(acc[:, sl], c, s, rot_half)
                o_ref[:, sl] = r.astype(o_ref.dtype)
                if second:
                    o2_ref[:, sl] = r.astype(o2_ref.dtype)
        else:
            o_ref[...] = acc.astype(o_ref.dtype)
            if second:
                o2_ref[...] = acc.astype(o2_ref.dtype)

    if nk == 1:
        finish(part)
    else:
        k = pl.program_id(2)

        @pl.when(k == 0)
        def _():
            acc_ref[...] = part

        @pl.when(k > 0)
        def _():
            acc_ref[...] += part

        @pl.when(k == nk - 1)
        def _():
            finish(acc_ref[...])


def _mm(x, w, *, layer=None, n_cols=None, col0=0, tm=1024, tn=512, tk=None, out_dtype=F32, second_dtype=None,
        act=None, bias=None, rot=None, name="mm"):
    M, K = x.shape
    N = n_cols if n_cols is not None else w.shape[-1]
    tm = min(tm, M) if rot is None else min(tm, M, rot[0].shape[0])
    tn = min(tn, N)
    while N % tn or col0 % tn:
        tn //= 2
    tk = K if tk is None else min(tk, K)
    assert M % tm == 0 and K % tk == 0 and tn % LANES == 0
    nk = K // tk
    jb = col0 // tn
    if layer is None:
        w_spec = pl.BlockSpec((tk, tn), lambda i, j, k: (k, j + jb))
    else:
        w_spec = pl.BlockSpec((None, tk, tn), lambda i, j, k: (layer, k, j + jb))
    in_specs = [pl.BlockSpec((tm, tk), lambda i, j, k: (i, k)), w_spec]
    args = [x, w]
    if bias is not None:
        in_specs.append(pl.BlockSpec((1, tn), lambda i, j, k: (0, j)))
        args.append(bias)
    rot_half = 0
    if rot is not None:
        c_tab, s_tab, rot_half = rot
        nr = c_tab.shape[0] // tm
        assert c_tab.shape[0] % tm == 0
        for t in (c_tab, s_tab):
            in_specs.append(pl.BlockSpec((tm, LANES), lambda i, j, k: (i % nr, 0)))
            args.append(t)
    out_shape = [jax.ShapeDtypeStruct((M, N), out_dtype)]
    out_specs = [pl.BlockSpec((tm, tn), lambda i, j, k: (i, j))]
    if second_dtype is not None:
        out_shape.append(jax.ShapeDtypeStruct((M, N), second_dtype))
        out_specs.append(pl.BlockSpec((tm, tn), lambda i, j, k: (i, j)))
    scratch = [pltpu.VMEM((tm, tn), F32)] if nk > 1 else []
    res = pl.pallas_call(
        functools.partial(_mm_body, nk=nk, act=act, rot_half=rot_half, has_bias=bias is not None,
                          second=second_dtype is not None),
        grid=(M // tm, N // tn, nk),
        in_specs=in_specs, out_specs=out_specs, out_shape=out_shape, scratch_shapes=scratch,
        compiler_params=_cparams(("parallel", "parallel", "arbitrary")), name=name)(*args)
    return res if second_dtype is not None else res[0]


def _mod_spec(table, col, tm, n, rows_per_group):
    if rows_per_group is None:
        return pl.BlockSpec((tm, n), lambda i, *_: (i, col))
    assert table.shape[0] % 8 == 0 and rows_per_group % tm == 0
    return pl.BlockSpec((table.shape[0], n), lambda i, *_: (0, col))


def _mod_row(ref, per):
    return ref[...] if per is None else ref[pl.ds(pl.program_id(0) // per, 1), :]


def _modulate_body(y_ref, sh_ref, sc_ref, u_ref, *, per):
    u_ref[...] = (y_ref[...] * (1.0 + _mod_row(sc_ref, per)) + _mod_row(sh_ref, per)).astype(u_ref.dtype)


def _modulate(y, table, sh_col, sc_col, rows_per_group, tm=512):
    M, D = y.shape
    tm = min(tm, M)
    per = None if rows_per_group is None else rows_per_group // tm
    return pl.pallas_call(
        functools.partial(_modulate_body, per=per), grid=(M // tm,),
        in_specs=[pl.BlockSpec((tm, D), lambda i: (i, 0)), _mod_spec(table, sh_col, tm, D, rows_per_group),
                  _mod_spec(table, sc_col, tm, D, rows_per_group)],
        out_specs=pl.BlockSpec((tm, D), lambda i: (i, 0)),
        out_shape=jax.ShapeDtypeStruct((M, D), BF16),
        compiler_params=_cparams(("parallel",)), name="modulate")(y, table, table)


def _mm_postnorm_body(*refs, alpha, with_mod, per):
    if with_mod:
        x_ref, w_ref, y_ref, g_ref, lg_ref, lb_ref, sh_ref, sc_ref, yo_ref, u_ref = refs
    else:
        x_ref, w_ref, y_ref, g_ref, lg_ref, lb_ref, yo_ref = refs
    h = _dot(x_ref[...].astype(BF16), w_ref[...].astype(BF16))
    z = alpha * y_ref[...] + _mod_row(g_ref, per) * h
    mu = jnp.mean(z, axis=-1, keepdims=True)
    zc = z - mu
    var = jnp.mean(zc * zc, axis=-1, keepdims=True)
    yn = zc * lax.rsqrt(var + LN_EPS) * lg_ref[...] + lb_ref[...]
    yo_ref[...] = yn
    if with_mod:
        u_ref[...] = (yn * (1.0 + _mod_row(sc_ref, per)) + _mod_row(sh_ref, per)).astype(u_ref.dtype)


def _mm_postnorm(x, w, y, gate, ln_g, ln_b, mod, rows_per_group, alpha, layer=None, name="mm_postnorm"):
    M, K = x.shape
    N = w.shape[-1]

    def vmem_bytes(tm):
        return K * N * 2 + 2 * tm * K * 2 + tm * N * (2 * 4 + 2 * 4 + 2 * 2 + 4 + 4)

    tm = min(512, M)
    while tm > 8 and vmem_bytes(tm) > POSTNORM_VMEM_BUDGET_MB * 1024 * 1024:
        tm //= 2
    assert M % tm == 0
    row = pl.BlockSpec((tm, N), lambda i: (i, 0))
    vec = pl.BlockSpec((1, N), lambda i: (0, 0))
    per = None if rows_per_group is None else rows_per_group // tm
    if layer is None:
        w_spec = pl.BlockSpec((K, N), lambda i: (0, 0))
    else:
        w_spec = pl.BlockSpec((None, K, N), lambda i: (layer, 0, 0))
    in_specs = [pl.BlockSpec((tm, K), lambda i: (i, 0)), w_spec,
                row, _mod_spec(gate[0], gate[1], tm, N, rows_per_group), vec, vec]
    args = [x, w, y, gate[0], ln_g.reshape(1, N), ln_b.reshape(1, N)]
    out_shape = [jax.ShapeDtypeStruct((M, N), F32)]
    out_specs = [row]
    if mod is not None:
        for table, col in mod:
            in_specs.append(_mod_spec(table, col, tm, N, rows_per_group))
            args.append(table)
        out_shape.append(jax.ShapeDtypeStruct((M, N), BF16))
        out_specs.append(row)
    res = pl.pallas_call(
        functools.partial(_mm_postnorm_body, alpha=alpha, with_mod=mod is not None, per=per),
        grid=(M // tm,), in_specs=in_specs, out_specs=out_specs, out_shape=out_shape,
        compiler_params=_cparams(("parallel",)), name=name)(*args)
    return (res[0], res[1]) if mod is not None else (res[0], None)


def _swiglu_body(x_ref, wg_ref, wu_ref, o_ref):
    x = x_ref[...]
    g = _dot(x, wg_ref[...].astype(BF16))
    u = _dot(x, wu_ref[...].astype(BF16))
    o_ref[...] = (g * jax.nn.sigmoid(g) * u).astype(o_ref.dtype)


def _swiglu_up(x, w_gu, layer, tm=1024, tn=512):
    M, K = x.shape
    F = w_gu.shape[2] // 2
    tm = min(tm, M)
    tn = min(tn, F)
    while F % tn:
        tn //= 2
    nj = F // tn
    return pl.pallas_call(
        _swiglu_body, grid=(M // tm, nj),
        in_specs=[pl.BlockSpec((tm, K), lambda i, j: (i, 0)),
                  pl.BlockSpec((None, K, tn), lambda i, j: (layer, 0, j)),
                  pl.BlockSpec((None, K, tn), lambda i, j: (layer, 0, j + nj))],
        out_specs=pl.BlockSpec((tm, tn), lambda i, j: (i, j)),
        out_shape=jax.ShapeDtypeStruct((M, F), BF16),
        compiler_params=_cparams(("parallel", "parallel")), name="swiglu_up")(x, w_gu, w_gu)


def _mla_in_body(x_ref, w_ref, qg_ref, kvg_ref, c_ref, s_ref, cq_ref, ckv_ref, ckvb_ref, kpe_ref, kpeb_ref,
                 *, ql, kvl, rope):
    acc = _dot(x_ref[...], w_ref[...])

    def rms(v, g):
        return v * lax.rsqrt(jnp.mean(v * v, axis=-1, keepdims=True) + RMS_EPS) * g

    cq_ref[...] = rms(acc[:, :ql], qg_ref[...]).astype(cq_ref.dtype)
    ckv = rms(acc[:, ql:ql + kvl], kvg_ref[...])
    ckv_ref[...] = ckv
    ckvb_ref[...] = ckv.astype(ckvb_ref.dtype)
    kpe = _rot(acc[:, ql + kvl:], c_ref[...], s_ref[...], rope // 2)[:, :rope]
    kpe_ref[...] = kpe
    kpeb_ref[...] = kpe.astype(kpeb_ref.dtype)


def _mla_in(u, w_pad, q_g, kv_g, c_tab, s_tab, ql, kvl, rope, tm=512):
    M, K = u.shape
    N = w_pad.shape[1]
    tm = min(tm, M)
    nr = c_tab.shape[0] // tm
    tab = pl.BlockSpec((tm, LANES), lambda i: (i % nr, 0))

    def row(n):
        return pl.BlockSpec((tm, n), lambda i: (i, 0))

    return pl.pallas_call(
        functools.partial(_mla_in_body, ql=ql, kvl=kvl, rope=rope), grid=(M // tm,),
        in_specs=[row(K), pl.BlockSpec((K, N), lambda i: (0, 0)),
                  pl.BlockSpec((1, ql), lambda i: (0, 0)), pl.BlockSpec((1, kvl), lambda i: (0, 0)), tab, tab],
        out_specs=[row(ql), row(kvl), row(kvl), row(rope), row(rope)],
        out_shape=[jax.ShapeDtypeStruct((M, ql), BF16), jax.ShapeDtypeStruct((M, kvl), F32),
                   jax.ShapeDtypeStruct((M, kvl), BF16), jax.ShapeDtypeStruct((M, rope), F32),
                   jax.ShapeDtypeStruct((M, rope), BF16)],
        compiler_params=_cparams(("parallel",)), name="mla_in")(
            u, w_pad, q_g.reshape(1, ql), kv_g.reshape(1, kvl), c_tab, s_tab)


def _mla_q_body(cq_ref, wn_ref, wp_ref, wuk_ref, c_ref, s_ref, ql_ref, qp_ref, *, H, nope, rope):
    cq = cq_ref[...]
    qn = _dot(cq, wn_ref[...]).astype(BF16)
    for h in range(H):
        ql_ref[0, h] = _dot(qn[:, h * nope:(h + 1) * nope], wuk_ref[h]).astype(ql_ref.dtype)
    qp = _dot(cq, wp_ref[...])
    c = c_ref[...]
    s = s_ref[...]
    per = LANES // rope
    for j in range(H // per):
        r = _rot(qp[:, j * LANES:(j + 1) * LANES], c, s, rope // 2)
        for e in range(per):
            qp_ref[0, j * per + e] = r[:, e * rope:(e + 1) * rope].astype(qp_ref.dtype)


def _mla_q(cq, w_nope, w_pe, w_uk_t, c_tab, s_tab, nb, tb, H, nope, rope, kvl, tm=512):
    M, ql = cq.shape
    tm = min(tm, tb)
    per = tb // tm
    nr = c_tab.shape[0] // tm
    tab = pl.BlockSpec((tm, LANES), lambda i: (i % nr, 0))
    full2 = lambda a: pl.BlockSpec(a.shape, lambda i: (0, 0))
    return pl.pallas_call(
        functools.partial(_mla_q_body, H=H, nope=nope, rope=rope), grid=(M // tm,),
        in_specs=[pl.BlockSpec((tm, ql), lambda i: (i, 0)), full2(w_nope), full2(w_pe),
                  pl.BlockSpec(w_uk_t.shape, lambda i: (0, 0, 0)), tab, tab],
        out_specs=[pl.BlockSpec((1, H, tm, kvl), lambda i: (i // per, 0, i % per, 0)),
                   pl.BlockSpec((1, H, tm, rope), lambda i: (i // per, 0, i % per, 0))],
        out_shape=[jax.ShapeDtypeStruct((nb, H, tb, kvl), BF16), jax.ShapeDtypeStruct((nb, H, tb, rope), BF16)],
        compiler_params=_cparams(("parallel",)), name="mla_q")(cq, w_nope, w_pe, w_uk_t, c_tab, s_tab)


def _flash_update(s, v, m_ref, l_ref, acc_ref):
    m_old = m_ref[...]
    m_new = jnp.maximum(m_old, jnp.max(s, axis=-1, keepdims=True))
    m_safe = jnp.where(m_new == NEG_INF, 0.0, m_new)
    a = jnp.exp(m_old - m_safe)
    p = jnp.exp(s - m_safe)
    l_ref[...] = a * l_ref[...] + jnp.sum(p, axis=-1, keepdims=True)
    acc_ref[...] = a * acc_ref[...] + _dot(p.astype(BF16), v)
    m_ref[...] = m_new


def _lane_fold(x, op):
    r = x[:, :LANES]
    for j in range(1, x.shape[1] // LANES):
        r = op(r, x[:, j * LANES:(j + 1) * LANES])
    return r


def _mla_prompt_body(ql_ref, qp_ref, ckv_ref, kpe_ref, wuv_ref, o_ref, s_ref, mx_ref, lp_ref, acc_ref,
                     *, H, tq, tk, scale, vh):
    i = pl.program_id(1)
    R = H * tq
    n_blk = lax.div(i * tq + tq - 1, tk) + 1

    def score(c, masked):
        ql = ql_ref[0].reshape(R, ql_ref.shape[-1])
        qp = qp_ref[0].reshape(R, qp_ref.shape[-1])
        start = pl.multiple_of(c * tk, tk)
        s = (_dot_nt(ql, ckv_ref[0, pl.ds(start, tk), :]) + _dot_nt(qp, kpe_ref[0, pl.ds(start, tk), :])) * scale
        if masked:
            tok = i * tq + (lax.broadcasted_iota(jnp.int32, s.shape, 0) % tq)
            key = c * tk + lax.broadcasted_iota(jnp.int32, s.shape, 1)
            s = jnp.where(key <= tok, s, NEG_INF)
        s_ref[c] = s
        return _lane_fold(s, jnp.maximum)

    mx_ref[...] = jnp.full(mx_ref.shape, NEG_INF, F32)

    def far(c, _):
        mx_ref[...] = jnp.maximum(mx_ref[...], score(c, False))
        return 0

    lax.fori_loop(0, n_blk - 1, far, 0)
    mx = jnp.maximum(mx_ref[...], score(n_blk - 1, True))
    m = jnp.max(mx, axis=-1, keepdims=True)

    lp_ref[...] = jnp.zeros(lp_ref.shape, F32)
    acc_ref[...] = jnp.zeros(acc_ref.shape, F32)

    def pv(c, _):
        p = jnp.exp(s_ref[c] - m)
        lp_ref[...] += _lane_fold(p, jnp.add)
        acc_ref[...] += _dot(p.astype(BF16), ckv_ref[0, pl.ds(pl.multiple_of(c * tk, tk), tk), :])
        return 0

    lax.fori_loop(0, n_blk, pv, 0)
    o = (acc_ref[...] / jnp.sum(lp_ref[...], axis=-1, keepdims=True)).astype(BF16)
    for h in range(H):
        o_ref[:, h * vh:(h + 1) * vh] = _dot(o[h * tq:(h + 1) * tq], wuv_ref[:, h * vh:(h + 1) * vh]
                                             ).astype(o_ref.dtype)


def _mla_prompt_attn(q_lat, q_pe, ckv_b, kpe_b, w_uv, scale, tq=128, tk=256):
    B, H, T, C = q_lat.shape
    R = q_pe.shape[-1]
    tq = min(tq, T)
    tk = min(tk, T)
    nq, nk = T // tq, T // tk
    vh = w_uv.shape[1] // H
    return pl.pallas_call(
        functools.partial(_mla_prompt_body, H=H, tq=tq, tk=tk, scale=scale, vh=vh),
        grid=(B, nq),
        in_specs=[pl.BlockSpec((1, H, tq, C), lambda b, i: (b, 0, i, 0)),
                  pl.BlockSpec((1, H, tq, R), lambda b, i: (b, 0, i, 0)),
                  pl.BlockSpec((1, T, C), lambda b, i: (b, 0, 0)), pl.BlockSpec((1, T, R), lambda b, i: (b, 0, 0)),
                  pl.BlockSpec(w_uv.shape, lambda b, i: (0, 0))],
        out_specs=pl.BlockSpec((tq, H * vh), lambda b, i: (b * nq + i, 0)),
        out_shape=jax.ShapeDtypeStruct((B * T, H * vh), BF16),
        scratch_shapes=[pltpu.VMEM((nk, H * tq, tk), F32), pltpu.VMEM((H * tq, LANES), F32),
                        pltpu.VMEM((H * tq, LANES), F32), pltpu.VMEM((H * tq, C), F32)],
        compiler_params=_cparams(("parallel", "arbitrary")), name="mla_prompt_attn")(
            q_lat, q_pe, ckv_b, kpe_b, w_uv)


def _new_key_scores(q_parts, k_parts, n_new):
    cols = []
    for t in range(n_new):
        acc = None
        for q, k in zip(q_parts, k_parts):
            d = jnp.sum(q * k[t:t + 1, :], axis=-1, keepdims=True)
            acc = d if acc is None else acc + d
        cols.append(acc)
    return cols


def _cols_to_block(cols, rows):
    lane = lax.broadcasted_iota(jnp.int32, (rows, LANES), 1)
    blk = jnp.full((rows, LANES), NEG_INF, F32)
    for t, c in enumerate(cols):
        blk = jnp.where(lane == t, c, blk)
    return blk


def _flash_update_new(s_blk, v_new, n_new, m_ref, l_ref, acc_ref):
    m_old = m_ref[...]
    m_new = jnp.maximum(m_old, jnp.max(s_blk, axis=-1, keepdims=True))
    m_safe = jnp.where(m_new == NEG_INF, 0.0, m_new)
    a = jnp.exp(m_old - m_safe)
    p = jnp.exp(s_blk - m_safe)
    l_ref[...] = a * l_ref[...] + jnp.sum(p, axis=-1, keepdims=True)
    acc = a * acc_ref[...]
    for t in range(n_new):
        acc = acc + p[:, t:t + 1] * v_new[t:t + 1, :]
    acc_ref[...] = acc
    m_ref[...] = m_new


def _page_copies(pt_ref, step, slot, streams, sem_ref, pps, page, layer):
    copies = []
    for r in range(pps):
        pid = pt_ref[step * pps + r]
        for pool, buf, lanes in streams:
            if lanes:
                dst = buf.at[slot, :, pl.ds(r * page, page)]
            else:
                rows = pool.shape[2]
                dst = buf.at[slot, pl.ds(r * rows, rows)]
            copies.append(pltpu.make_async_copy(pool.at[layer, pid], dst, sem_ref.at[slot]))
    return copies


def _fetch_pages(pt_ref, streams, sem_ref, pps, page, layer, n_steps):
    n_slots = streams[0][1].shape[0]
    ahead = n_slots - 1
    s = pl.program_id(0) * pl.num_programs(1) + pl.program_id(1)
    slot = s % n_slots

    @pl.when(s == 0)
    def _():
        for t in range(min(ahead, n_steps)):
            for cp in _page_copies(pt_ref, t, t, streams, sem_ref, pps, page, layer):
                cp.start()

    @pl.when(s + ahead < n_steps)
    def _():
        for cp in _page_copies(pt_ref, s + ahead, (s + ahead) % n_slots, streams, sem_ref, pps, page, layer):
            cp.start()

    for cp in _page_copies(pt_ref, s, slot, streams, sem_ref, pps, page, layer):
        cp.wait()
    return slot


def _mla_sample_body(pt_ref, ql_ref, qp_ref, ckv_pool, kpe_pool, cn_ref, kn_ref, wuv_ref, o_ref,
                     cbuf, pbuf, sem, kc_ref, kp_ref, m_ref, l_ref, acc_ref,
                     *, pps, H, ts, page, nc, n_steps, layer, scale, vh):
    c = pl.program_id(1)
    R = H * ts
    slot = _fetch_pages(pt_ref, [(ckv_pool, cbuf, False), (kpe_pool, pbuf, True)], sem, pps, page, layer, n_steps)

    @pl.when(c == 0)
    def _():
        m_ref[...] = jnp.full(m_ref.shape, NEG_INF, F32)
        l_ref[...] = jnp.zeros(l_ref.shape, F32)
        acc_ref[...] = jnp.zeros(acc_ref.shape, F32)

    kc_ref[...] = cbuf[slot].astype(BF16)
    kp_ref[...] = pbuf[slot].astype(BF16)
    ql = ql_ref[0]
    qp = qp_ref[0]
    ckv = kc_ref[...]
    s = (_dot_nt(ql, ckv) + _dot(qp, kp_ref[...])) * scale
    _flash_update(s, ckv, m_ref, l_ref, acc_ref)

    @pl.when(c == nc - 1)
    def _():
        cn = cn_ref[0]
        cols = _new_key_scores([ql.astype(F32), qp.astype(F32)], [cn, kn_ref[0]], ts)
        blk = _cols_to_block(cols, R) * scale
        qt = lax.broadcasted_iota(jnp.int32, (R, LANES), 0) % ts
        lane = lax.broadcasted_iota(jnp.int32, (R, LANES), 1)
        blk = jnp.where(lane <= qt, blk, NEG_INF)
        _flash_update_new(blk, cn, ts, m_ref, l_ref, acc_ref)
        o = (acc_ref[...] / l_ref[...]).astype(BF16)
        full = _dot(o, wuv_ref[...])
        o_ref[0] = jnp.concatenate([full[h * ts:(h + 1) * ts, h * vh:(h + 1) * vh] for h in range(H)], axis=1)


def _mla_sample_attn(q_lat, q_pe, pool_ckv, pool_kpe_t, layer, page_table, ckv_new, kpe_new, w_uv, scale, H,
                     pps=32):
    Bs, R, C = q_lat.shape
    Rr = q_pe.shape[-1]
    ts = R // H
    n_pages = page_table.shape[1]
    page = pool_ckv.shape[2]
    pps = min(pps, n_pages)
    nc = n_pages // pps
    vh = w_uv.shape[1] // H

    W = pps * page
    hbm = pl.BlockSpec(memory_space=pl.ANY)
    in_specs = [pl.BlockSpec((1, R, C), lambda b, c, pt: (b, 0, 0)),
                pl.BlockSpec((1, R, Rr), lambda b, c, pt: (b, 0, 0)),
                hbm, hbm,
                pl.BlockSpec((1, ts, C), lambda b, c, pt: (b, 0, 0)),
                pl.BlockSpec((1, ts, Rr), lambda b, c, pt: (b, 0, 0)),
                pl.BlockSpec(w_uv.shape, lambda b, c, pt: (0, 0))]
    grid_spec = pltpu.PrefetchScalarGridSpec(
        num_scalar_prefetch=1, grid=(Bs, nc), in_specs=in_specs,
        out_specs=pl.BlockSpec((1, ts, H * vh), lambda b, c, pt: (b, 0, 0)),
        scratch_shapes=[pltpu.VMEM((3, W, C), F32), pltpu.VMEM((3, Rr, W), F32), pltpu.SemaphoreType.DMA((3,)),
                        pltpu.VMEM((W, C), BF16), pltpu.VMEM((Rr, W), BF16),
                        pltpu.VMEM((R, 1), F32), pltpu.VMEM((R, 1), F32), pltpu.VMEM((R, C), F32)])
    return pl.pallas_call(
        functools.partial(_mla_sample_body, pps=pps, H=H, ts=ts, page=page, nc=nc, n_steps=Bs * nc, layer=layer,
                          scale=scale, vh=vh),
        grid_spec=grid_spec, out_shape=jax.ShapeDtypeStruct((Bs, ts, H * vh), F32),
        compiler_params=_cparams(("arbitrary", "arbitrary")), name="mla_sample_attn")(
            page_table.reshape(-1), q_lat, q_pe, pool_ckv, pool_kpe_t, ckv_new, kpe_new, w_uv)


def _t5_bias_of_dist(d, rb_ref, h, n_buckets):
    dist = jnp.maximum(d, 0)
    exact = n_buckets // 2
    df = jnp.maximum(dist, 1).astype(F32)
    large = exact + (jnp.log(df / exact) / math.log(REL_MAX_DIST / exact) * (n_buckets - exact)).astype(jnp.int32)
    large = jnp.minimum(large, n_buckets - 1)
    bucket = jnp.where(dist < exact, dist, large)
    out = jnp.zeros(d.shape, F32)
    for b in range(n_buckets):
        out = jnp.where(bucket == b, rb_ref[b, h], out)
    return out


def _bias_tables_body(rb_ref, t0_ref, t1_ref, tp_ref, tn_ref, *, NH, tq, ts8, n_buckets):
    r = lax.broadcasted_iota(jnp.int32, (tq, tq), 0)
    c = lax.broadcasted_iota(jnp.int32, (tq, tq), 1)
    r8 = lax.broadcasted_iota(jnp.int32, (ts8, LANES), 0)
    c8 = lax.broadcasted_iota(jnp.int32, (ts8, LANES), 1)
    for h in range(NH):
        far = rb_ref[n_buckets - 1, h]
        t0_ref[h] = _t5_bias_of_dist(r - c, rb_ref, h, n_buckets) - far
        t1_ref[h] = _t5_bias_of_dist(r - c + tq, rb_ref, h, n_buckets) - far
        tp_ref[h] = _t5_bias_of_dist(r8 - c8 + LANES, rb_ref, h, n_buckets)
        tn_ref[h] = _t5_bias_of_dist(r8 - c8, rb_ref, h, n_buckets)


def _bias_tables(rel_bias, tq, ts8):
    nb, NH = rel_bias.shape
    return pl.pallas_call(
        functools.partial(_bias_tables_body, NH=NH, tq=tq, ts8=ts8, n_buckets=nb),
        in_specs=[pl.BlockSpec(memory_space=pltpu.SMEM)],
        out_shape=[jax.ShapeDtypeStruct((NH, tq, tq), F32), jax.ShapeDtypeStruct((NH, tq, tq), F32),
                   jax.ShapeDtypeStruct((NH, ts8, LANES), F32), jax.ShapeDtypeStruct((NH, ts8, LANES), F32)],
        compiler_params=pltpu.CompilerParams(vmem_limit_bytes=VMEM_LIMIT_MB * 1024 * 1024),
        name="t5_bias_tables")(rel_bias)


def _bisect(lo0, hi0, count_ge, k_sel):
    def body(_, lh):
        lo, hi = lh
        mid = 0.5 * (lo + hi)
        ge = count_ge(mid) >= k_sel
        return jnp.where(ge, mid, lo), jnp.where(ge, hi, mid)

    lo, _ = lax.fori_loop(0, BISECT_ITERS, body, (lo0, hi0))
    return lo


def _dsa_prompt_body(q_ref, qi_ref, wi_ref, k_ref, v_ref, kit_ref, t0_ref, t1_ref, o_ref,
                     sc_ref, wb_ref, s_ref, lp_ref, acc_ref, *, tq, nq, k_sel, IH, ID, KVH, G, HD, scale):
    i = pl.program_id(1)
    row_tok = i * tq + lax.broadcasted_iota(jnp.int32, (tq, tq), 0)
    col_in = lax.broadcasted_iota(jnp.int32, (tq, tq), 1)
    qi = qi_ref[...]
    wi = wi_ref[...]
    for h in range(IH):
        wb_ref[h] = jnp.broadcast_to(wi[:, h:h + 1], (tq, tq))

    def idx_block(c, carry):
        lo, hi = carry
        kit = kit_ref[0, c]
        tot = jnp.zeros((tq, tq), F32)
        for h in range(IH):
            d = _dot(qi[:, h * ID:(h + 1) * ID], kit)
            tot = tot + jnp.maximum(d, 0.0) * wb_ref[h]
        ok = (c * tq + col_in) <= row_tok
        sc_ref[c] = jnp.where(ok, tot, NEG_INF)
        lo = jnp.minimum(lo, jnp.min(jnp.where(ok, tot, jnp.inf), axis=-1, keepdims=True))
        hi = jnp.maximum(hi, jnp.max(jnp.where(ok, tot, NEG_INF), axis=-1, keepdims=True))
        return lo, hi

    lo0, hi0 = lax.fori_loop(0, i + 1, idx_block,
                             (jnp.full((tq, 1), jnp.inf, F32), jnp.full((tq, 1), NEG_INF, F32)))

    def count_ge(mid):
        def blk(c, acc):
            t = sc_ref[c]
            part = jnp.zeros((tq, LANES), F32)
            for j in range(tq // LANES):
                part = part + jnp.where(t[:, j * LANES:(j + 1) * LANES] >= mid, 1.0, 0.0)
            return acc + part
        acc = lax.fori_loop(0, i + 1, blk, jnp.zeros((tq, LANES), F32))
        return jnp.sum(acc, axis=-1, keepdims=True)

    thr = _bisect(lo0, hi0, count_ge, float(k_sel))

    def to_mask(c, _):
        sc_ref[c] = jnp.where(sc_ref[c] >= thr, 0.0, NEG_INF)
        return 0

    lax.fori_loop(0, i + 1, to_mask, 0)

    q = q_ref[...]
    R = G * tq
    nlv = tq // LANES

    def lane_fold(x, op):
        r = x[:, :LANES]
        for jj in range(1, nlv):
            r = op(r, x[:, jj * LANES:(jj + 1) * LANES])
        return r

    for n in range(KVH):
        qn = jnp.concatenate([q[:, (n * G + g) * HD:(n * G + g + 1) * HD] for g in range(G)], axis=0)

        def score(c, slot, bias):
            kc = k_ref[0, pl.ds(pl.multiple_of(c * tq, tq), tq), n * HD:(n + 1) * HD]
            s = _dot_nt(qn, kc) * scale + jnp.concatenate([sc_ref[c]] * G, axis=0)
            if bias is not None:
                s = s + bias
            s_ref[slot] = s
            return lane_fold(s, jnp.maximum)

        mx = lax.fori_loop(0, jnp.maximum(i - 1, 0), lambda c, mx: jnp.maximum(mx, score(c, c, None)),
                           jnp.full((R, LANES), NEG_INF, F32))
        prev = score(jnp.maximum(i - 1, 0), jnp.where(i >= 1, i - 1, nq), t1_ref[n * G:(n + 1) * G].reshape(R, tq))
        mx = jnp.where(i >= 1, jnp.maximum(mx, prev), mx)
        mx = jnp.maximum(mx, score(i, i, t0_ref[n * G:(n + 1) * G].reshape(R, tq)))
        m = jnp.max(mx, axis=-1, keepdims=True)

        lp_ref[...] = jnp.zeros(lp_ref.shape, F32)
        acc_ref[...] = jnp.zeros(acc_ref.shape, F32)

        def pv(c, _):
            p = jnp.exp(s_ref[c] - m)
            lp_ref[...] += lane_fold(p, jnp.add)
            vc = v_ref[0, pl.ds(pl.multiple_of(c * tq, tq), tq), n * HD:(n + 1) * HD]
            acc_ref[...] += _dot(p.astype(BF16), vc)
            return 0

        lax.fori_loop(0, i + 1, pv, 0)
        o = acc_ref[...] / jnp.sum(lp_ref[...], axis=-1, keepdims=True)
        for g in range(G):
            o_ref[:, (n * G + g) * HD:(n * G + g + 1) * HD] = o[g * tq:(g + 1) * tq].astype(o_ref.dtype)


def _dsa_prompt(q, qi, wi, k_b, v_b, ki_t, t0, t1, B, T, k_sel, IH, ID, KVH, G, HD, scale, tq):
    M = B * T
    nq = T // tq
    NH = KVH * G
    row = lambda n: pl.BlockSpec((tq, n), lambda b, i: (b * nq + i, 0))
    seq = lambda n: pl.BlockSpec((1, T, n), lambda b, i: (b, 0, 0))
    tab = pl.BlockSpec((NH, tq, tq), lambda b, i: (0, 0, 0))
    return pl.pallas_call(
        functools.partial(_dsa_prompt_body, tq=tq, nq=nq, k_sel=k_sel, IH=IH, ID=ID, KVH=KVH, G=G, HD=HD,
                          scale=scale),
        grid=(B, nq),
        in_specs=[row(NH * HD), row(IH * ID), row(IH), seq(KVH * HD), seq(KVH * HD),
                  pl.BlockSpec((1, nq, ID, tq), lambda b, i: (b, 0, 0, 0)), tab, tab],
        out_specs=row(NH * HD),
        scratch_shapes=[pltpu.VMEM((nq, tq, tq), F32), pltpu.VMEM((IH, tq, tq), F32),
                        pltpu.VMEM((nq + 1, G * tq, tq), F32), pltpu.VMEM((G * tq, LANES), F32),
                        pltpu.VMEM((G * tq, HD), F32)],
        out_shape=jax.ShapeDtypeStruct((M, NH * HD), BF16),
        compiler_params=_cparams(("parallel", "arbitrary")), name="dsa_prompt")(
            q, qi, wi, k_b.reshape(B, T, -1), v_b.reshape(B, T, -1), ki_t, t0, t1)


def _dsa_sample_index_body(pt_ref, qi_ref, wi_ref, ik_pool, kn_ref, sc_ref, scn_ref, ibuf, sem,
                           *, pps, IH, ts, ts8, page, n_steps, layer):
    c = pl.program_id(1)
    slot = _fetch_pages(pt_ref, [(ik_pool, ibuf, True)], sem, pps, page, layer, n_steps)
    qi = qi_ref[0]
    wcol = wi_ref[0]
    kit = ibuf[slot].astype(BF16)
    d = jnp.maximum(_dot(qi, kit), 0.0) * wcol
    tot = d[0:ts8]
    for h in range(1, IH):
        tot = tot + d[h * ts8:(h + 1) * ts8]
    sc_ref[0] = tot

    @pl.when(c == 0)
    def _():
        cols = _new_key_scores([qi.astype(F32)], [kn_ref[0]], ts)
        ncols = []
        for col in cols:
            col = jnp.maximum(col, 0.0) * wcol
            t = col[0:ts8]
            for h in range(1, IH):
                t = t + col[h * ts8:(h + 1) * ts8]
            ncols.append(t)
        blk = _cols_to_block(ncols, ts8)
        qt = lax.broadcasted_iota(jnp.int32, (ts8, LANES), 0)
        lane = lax.broadcasted_iota(jnp.int32, (ts8, LANES), 1)
        scn_ref[0] = jnp.where(lane <= qt, blk, NEG_INF)


def _dsa_sample_index(qi_rows, wi_rows, pool_ik_t, layer, page_table, ki_new, IH, ts, ts8, pps=64):
    Bs, R, ID = qi_rows.shape
    n_pages = page_table.shape[1]
    page = pool_ik_t.shape[3]
    pps = min(pps, n_pages)
    nc = n_pages // pps
    W = pps * page

    in_specs = [pl.BlockSpec((1, R, ID), lambda b, c, pt: (b, 0, 0)),
                pl.BlockSpec((1, R, 1), lambda b, c, pt: (b, 0, 0)),
                pl.BlockSpec(memory_space=pl.ANY),
                pl.BlockSpec((1, ts, ID), lambda b, c, pt: (b, 0, 0))]
    grid_spec = pltpu.PrefetchScalarGridSpec(
        num_scalar_prefetch=1, grid=(Bs, nc), in_specs=in_specs,
        out_specs=[pl.BlockSpec((1, ts8, W), lambda b, c, pt: (b, 0, c)),
                   pl.BlockSpec((1, ts8, LANES), lambda b, c, pt: (b, 0, 0))],
        scratch_shapes=[pltpu.VMEM((3, ID, W), F32), pltpu.SemaphoreType.DMA((3,))])
    return pl.pallas_call(
        functools.partial(_dsa_sample_index_body, pps=pps, IH=IH, ts=ts, ts8=ts8, page=page, n_steps=Bs * nc,
                          layer=layer),
        grid_spec=grid_spec,
        out_shape=[jax.ShapeDtypeStruct((Bs, ts8, nc * W), F32), jax.ShapeDtypeStruct((Bs, ts8, LANES), F32)],
        compiler_params=_cparams(("arbitrary", "arbitrary")), name="dsa_sample_index")(
            page_table.reshape(-1), qi_rows, wi_rows, pool_ik_t, ki_new)


def _topk_threshold_body(sc_ref, scn_ref, thr_ref, *, k_sel):
    n_groups = sc_ref.shape[2] // LANES
    scn = scn_ref[...]
    lo0 = jnp.min(jnp.where(scn > NEG_INF, scn, jnp.inf), axis=-1, keepdims=True)
    hi0 = jnp.max(scn, axis=-1, keepdims=True)
    lo_g = hi_g = sc_ref[:, :, :LANES]
    for j in range(1, n_groups):
        t = sc_ref[:, :, j * LANES:(j + 1) * LANES]
        lo_g = jnp.minimum(lo_g, t)
        hi_g = jnp.maximum(hi_g, t)
    lo0 = jnp.minimum(lo0, jnp.min(lo_g, axis=-1, keepdims=True))
    hi0 = jnp.maximum(hi0, jnp.max(hi_g, axis=-1, keepdims=True))

    def count_ge(mid):
        acc = jnp.where(scn >= mid, 1.0, 0.0)
        for j in range(n_groups):
            acc = acc + jnp.where(sc_ref[:, :, j * LANES:(j + 1) * LANES] >= mid, 1.0, 0.0)
        return jnp.sum(acc, axis=-1, keepdims=True)

    thr = _bisect(lo0, hi0, count_ge, float(k_sel))
    thr_ref[...] = jnp.broadcast_to(thr, thr_ref.shape)


def _topk_threshold(scores, scores_new, k_sel, bb=16):
    Bs, ts8, P = scores.shape
    bb = min(bb, Bs)
    assert Bs % bb == 0
    return pl.pallas_call(
        functools.partial(_topk_threshold_body, k_sel=k_sel), grid=(Bs // bb,),
        in_specs=[pl.BlockSpec((bb, ts8, P), lambda i: (i, 0, 0)),
                  pl.BlockSpec((bb, ts8, LANES), lambda i: (i, 0, 0))],
        out_specs=pl.BlockSpec((bb, ts8, LANES), lambda i: (i, 0, 0)),
        out_shape=jax.ShapeDtypeStruct((Bs, ts8, LANES), F32),
        compiler_params=_cparams(("parallel",)), name="topk_threshold")(scores, scores_new)


def _dsa_sample_attn_body(pt_ref, far_ref, q_ref, sc_ref, scn_ref, thr_ref, k_pool, v_pool, kn_ref, vn_ref,
                          tp_ref, tn_ref, o_ref, kbuf, vbuf, sem, kb_ref, vb_ref, m_ref, l_ref, acc_ref,
                          *, pps, KVH, G, HD, ts, ts8, page, nc, n_steps, layer, scale):
    c = pl.program_id(1)
    R = G * ts8
    W = pps * page
    slot = _fetch_pages(pt_ref, [(k_pool, kbuf, False), (v_pool, vbuf, False)], sem, pps, page, layer, n_steps)

    @pl.when(c == 0)
    def _():
        m_ref[...] = jnp.full(m_ref.shape, NEG_INF, F32)
        l_ref[...] = jnp.zeros(l_ref.shape, F32)
        acc_ref[...] = jnp.zeros(acc_ref.shape, F32)

    for n in range(KVH):
        kb_ref[:, n * HD:(n + 1) * HD] = kbuf[slot, pl.ds(n, W, stride=KVH), :].astype(BF16)
        vb_ref[:, n * HD:(n + 1) * HD] = vbuf[slot, pl.ds(n, W, stride=KVH), :].astype(BF16)
    thr = thr_ref[0][:, :1]
    msk = jnp.concatenate([jnp.where(sc_ref[0] >= thr, 0.0, NEG_INF)] * G, axis=0)
    last = c == nc - 1
    for n in range(KVH):
        qn = q_ref[0, n]
        far = jnp.concatenate([jnp.full((ts8, 1), far_ref[n * G + g], F32) for g in range(G)], axis=0)
        near = tp_ref[n * G:(n + 1) * G].reshape(R, LANES)
        delta = jnp.where(last, near - far, 0.0)
        s = _dot_nt(qn, kb_ref[:, n * HD:(n + 1) * HD]) * scale + far + msk
        s = jnp.concatenate([s[:, :W - LANES], s[:, W - LANES:] + delta], axis=1)
        _flash_update(s, vb_ref[:, n * HD:(n + 1) * HD], m_ref.at[n], l_ref.at[n], acc_ref.at[n])

    @pl.when(last)
    def _():
        mskn = jnp.concatenate([jnp.where(scn_ref[0] >= thr, 0.0, NEG_INF)] * G, axis=0)
        outs = []
        for n in range(KVH):
            qn = q_ref[0, n].astype(F32)
            kn = kn_ref[0][:, n * HD:(n + 1) * HD]
            vn = vn_ref[0][:, n * HD:(n + 1) * HD]
            cols = _new_key_scores([qn], [kn], ts)
            blk = _cols_to_block(cols, R) * scale + tn_ref[n * G:(n + 1) * G].reshape(R, LANES) + mskn
            _flash_update_new(blk, vn, ts, m_ref.at[n], l_ref.at[n], acc_ref.at[n])
            o = acc_ref[n] / l_ref[n]
            outs += [o[g * ts8:g * ts8 + ts] for g in range(G)]
        o_ref[0] = jnp.concatenate(outs, axis=1)


def _dsa_sample_attn(q_rows, scores, scores_new, thr, pool_k, pool_v, layer, page_table, k_new, v_new, tp, tn,
                     far_bias, KVH, G, HD, ts, ts8, scale, pps=32):
    Bs = q_rows.shape[0]
    R = G * ts8
    n_pages = page_table.shape[1]
    page = pool_k.shape[2] // KVH
    pps = min(pps, n_pages)
    nc = n_pages // pps
    W = pps * page
    NH = KVH * G

    hbm = pl.BlockSpec(memory_space=pl.ANY)
    in_specs = [pl.BlockSpec((1, KVH, R, HD), lambda b, c, pt, f: (b, 0, 0, 0)),
                pl.BlockSpec((1, ts8, W), lambda b, c, pt, f: (b, 0, c)),
                pl.BlockSpec((1, ts8, LANES), lambda b, c, pt, f: (b, 0, 0)),
                pl.BlockSpec((1, ts8, LANES), lambda b, c, pt, f: (b, 0, 0)),
                hbm, hbm]
    in_specs += [pl.BlockSpec((1, ts, KVH * HD), lambda b, c, pt, f: (b, 0, 0))] * 2
    in_specs += [pl.BlockSpec((NH, ts8, LANES), lambda b, c, pt, f: (0, 0, 0))] * 2
    grid_spec = pltpu.PrefetchScalarGridSpec(
        num_scalar_prefetch=2, grid=(Bs, nc), in_specs=in_specs,
        out_specs=pl.BlockSpec((1, ts, NH * HD), lambda b, c, pt, f: (b, 0, 0)),
        scratch_shapes=[pltpu.VMEM((2, W * KVH, HD), F32), pltpu.VMEM((2, W * KVH, HD), F32),
                        pltpu.SemaphoreType.DMA((2,)),
                        pltpu.VMEM((W, KVH * HD), BF16), pltpu.VMEM((W, KVH * HD), BF16),
                        pltpu.VMEM((KVH, R, 1), F32), pltpu.VMEM((KVH, R, 1), F32), pltpu.VMEM((KVH, R, HD), F32)])
    return pl.pallas_call(
        functools.partial(_dsa_sample_attn_body, pps=pps, KVH=KVH, G=G, HD=HD, ts=ts, ts8=ts8, page=page,
                          nc=nc, n_steps=Bs * nc, layer=layer, scale=scale),
        grid_spec=grid_spec, out_shape=jax.ShapeDtypeStruct((Bs, ts, NH * HD), F32),
        compiler_params=_cparams(("arbitrary", "arbitrary")), name="dsa_sample_attn")(
            page_table.reshape(-1), far_bias, q_rows, scores, scores_new, thr, pool_k, pool_v,
            k_new, v_new, tp, tn)


def _rot_tables(pos, half, group, extra=None):
    lane = np.arange(LANES)
    g = lane % group
    rotary = g < 2 * half
    freq = np.float32(ROPE_THETA) ** (-np.arange(half, dtype=np.float32) / np.float32(half))
    ang = pos.astype(np.float32)[:, None] * freq[None, :].astype(np.float32)
    cos = np.cos(ang)[:, g % half]
    sin = np.sin(ang)[:, g % half]
    c_tab = np.where(rotary[None, :], cos, np.float32(1.0))
    s_tab = np.where(rotary[None, :], np.where((g < half)[None, :], -sin, sin), np.float32(0.0))
    if extra is not None:
        a, b, val = extra
        c_tab = np.where(((lane >= a) & (lane < b))[None, :], np.float32(val), c_tab)
    return jnp.asarray(c_tab, F32), jnp.asarray(s_tab, F32)


def kernel(x_prompt, x_sample, cache_mla_ckv, cache_mla_kpe, cache_dsa_k, cache_dsa_v, cache_dsa_idx_k,
           page_table, c_prompt, c_sample, w_ada, b_ada, ln_g, ln_b,
           mla_w_in, mla_q_norm, mla_kv_norm, mla_w_uq, mla_w_uk, mla_w_uv, mla_w_o,
           dsa_w_in, dsa_w_o, rel_bias, ffn_w_gu, ffn_w_down):
    B, T, D = x_prompt.shape
    Bs, ts, _ = x_sample.shape
    depth = w_ada.shape[0]
    alpha = (2 * depth) ** 0.25
    n_pages = page_table.shape[1]
    page = cache_mla_ckv.shape[2]
    past = n_pages * page
    Mp, Ms = B * T, Bs * ts
    ts8 = 8 * (-(-ts // 8))

    ql = mla_q_norm.shape[1]
    kvl = cache_mla_ckv.shape[-1]
    rope = cache_mla_kpe.shape[-1]
    H = mla_w_uk.shape[2]
    nope = mla_w_uk.shape[3]
    vh = mla_w_uv.shape[3]
    mla_scale = (nope + rope) ** -0.5
    KVH, HD = cache_dsa_k.shape[-2:]
    ID = cache_dsa_idx_k.shape[-1]
    NH = dsa_w_o.shape[1] // HD
    G = NH // KVH
    dq, dkv = NH * HD, KVH * HD
    IH = (dsa_w_in.shape[2] - dq - 2 * dkv - ID) // (ID + 1)
    dsa_scale = HD ** -0.5
    idx_wscale = (IH ** -0.5) * (ID ** -0.5)
    tq_dsa = min(256, T)

    pos_p = np.arange(T, dtype=np.int32)
    pos_s = np.tile(past + np.arange(ts, dtype=np.int32), Bs)
    tabs = {
        "mla": (_rot_tables(pos_p, rope // 2, rope), _rot_tables(pos_s, rope // 2, rope)),
        "idx": (_rot_tables(pos_p, IDX_ROPE // 2, ID), _rot_tables(pos_s, IDX_ROPE // 2, ID)),
        "tail": (_rot_tables(pos_p, IDX_ROPE // 2, LANES, (ID, ID + IH, idx_wscale)),
                 _rot_tables(pos_s, IDX_ROPE // 2, LANES, (ID, ID + IH, idx_wscale))),
    }

    c_all = jnp.concatenate([c_prompt, c_sample], axis=0)
    nc_rows = c_all.shape[0]
    c_all = jnp.pad(c_all, ((0, -nc_rows % 8), (0, 0)))
    mods = [_mm(c_all, w_ada, layer=i, bias=b_ada[i].reshape(1, -1), act="silu", tm=c_all.shape[0], tn=1024,
                name="ada_mod") for i in range(depth)]

    pool_kpe_t = jnp.swapaxes(cache_mla_kpe, 2, 3)
    pool_ik_t = jnp.swapaxes(cache_dsa_idx_k, 2, 3)
    pool_k = cache_dsa_k.reshape(cache_dsa_k.shape[0], -1, page * cache_dsa_k.shape[3], cache_dsa_k.shape[4])
    pool_v = cache_dsa_v.reshape(pool_k.shape)

    modv = [(mods[i], jnp.repeat(mods[i][B:B + Bs], ts, axis=0)) for i in range(depth)]

    groups = [dict(y=x_prompt.reshape(Mp, D), rpg=T, nb=B, tb=T, g=0),
              dict(y=x_sample.reshape(Ms, D), rpg=None, nb=1, tb=Ms, g=1)]
    for grp in groups:
        grp["u"] = _modulate(grp["y"], modv[0][grp["g"]], 0, 1, grp["rpg"])

    t0 = t1 = tp = tn = far_bias = None
    if depth > 1:
        t0, t1, tp, tn = _bias_tables(rel_bias, tq_dsa, ts8)
        far_bias = rel_bias[-1]

    w_down_all = ffn_w_down.astype(BF16)
    outs = {k: [[], []] for k in ("ckv", "kpe", "k", "v", "ik")}
    for i in range(depth):
        j = i // 2
        if i % 2 == 0:
            w_in_pad = jnp.pad(mla_w_in[j], ((0, 0), (0, LANES - rope))).astype(BF16)
            w_uq = mla_w_uq[j].reshape(ql, H, nope + rope)
            w_nope = w_uq[:, :, :nope].reshape(ql, H * nope).astype(BF16)
            w_pe = w_uq[:, :, nope:].reshape(ql, H * rope).astype(BF16)
            w_uk_t = jnp.transpose(mla_w_uk[j], (1, 2, 0)).astype(BF16)
            w_uv = mla_w_uv[j].reshape(kvl, H * vh).astype(BF16)
            w_o = mla_w_o[j].astype(BF16)
        else:
            w_in = dsa_w_in[j].astype(BF16)
            w_tail = jnp.pad(dsa_w_in[j][:, dq + 2 * dkv + IH * ID:], ((0, 0), (0, LANES - ID - IH))).astype(BF16)
            w_o = dsa_w_o[j].astype(BF16)

        for grp in groups:
            g = grp["g"]
            u = grp["u"]
            mv = modv[i][g]
            if i % 2 == 0:
                ct, st = tabs["mla"][g]
                cq, ckv, ckv_b, kpe, kpe_b = _mla_in(u, w_in_pad, mla_q_norm[j], mla_kv_norm[j], ct, st,
                                                     ql, kvl, rope)
                q_lat, q_pe = _mla_q(cq, w_nope, w_pe, w_uk_t, ct, st, grp["nb"], grp["tb"], H, nope, rope, kvl)
                if g == 0:
                    o = _mla_prompt_attn(q_lat, q_pe, ckv_b.reshape(B, T, kvl), kpe_b.reshape(B, T, rope),
                                         w_uv, mla_scale)
                else:
                    def rows(a):
                        n = a.shape[-1]
                        return a.reshape(H, Bs, ts, n).transpose(1, 0, 2, 3).reshape(Bs, H * ts, n)
                    o = _mla_sample_attn(rows(q_lat), rows(q_pe), cache_mla_ckv, pool_kpe_t, j, page_table,
                                         ckv.reshape(Bs, ts, kvl), kpe.reshape(Bs, ts, rope), w_uv, mla_scale, H)
                    o = o.reshape(Ms, H * vh).astype(BF16)
                outs["ckv"][g].append(ckv)
                outs["kpe"][g].append(kpe)
            else:
                q = _mm(u, w_in, n_cols=dq, col0=0, out_dtype=BF16, name="dsa_q")
                k, k_b = _mm(u, w_in, n_cols=dkv, col0=dq, second_dtype=BF16, name="dsa_k")
                v, v_b = _mm(u, w_in, n_cols=dkv, col0=dq + dkv, second_dtype=BF16, name="dsa_v")
                ci, si = tabs["idx"][g]
                qi = _mm(u, w_in, n_cols=IH * ID, col0=dq + 2 * dkv, out_dtype=BF16,
                         rot=(ci, si, IDX_ROPE // 2), name="dsa_qi")
                ctl, stl = tabs["tail"][g]
                tail = _mm(u, w_tail, rot=(ctl, stl, IDX_ROPE // 2), name="dsa_tail")
                ki = tail[:, :ID]
                wi = tail[:, ID:ID + IH]
                if g == 0:
                    ki_t = jnp.swapaxes(ki.astype(BF16).reshape(B, T // tq_dsa, tq_dsa, ID), 2, 3)
                    o = _dsa_prompt(q, qi, wi, k_b, v_b, ki_t, t0, t1, B, T,
                                    min(TOPK_MAX, T // 4), IH, ID, KVH, G, HD, dsa_scale, tq_dsa)
                else:
                    pad_t = ((0, 0), (0, 0), (0, ts8 - ts), (0, 0))
                    qi_rows = jnp.pad(qi.reshape(Bs, ts, IH, ID).transpose(0, 2, 1, 3), pad_t
                                      ).reshape(Bs, IH * ts8, ID)
                    wi_rows = jnp.pad(wi.reshape(Bs, ts, IH).transpose(0, 2, 1), ((0, 0), (0, 0), (0, ts8 - ts))
                                      ).reshape(Bs, IH * ts8, 1)
                    sc, sc_new = _dsa_sample_index(qi_rows, wi_rows, pool_ik_t, j, page_table,
                                                   ki.reshape(Bs, ts, ID), IH, ts, ts8)
                    thr = _topk_threshold(sc, sc_new, min(TOPK_MAX, (past + ts) // 4))
                    q_rows = jnp.pad(q.reshape(Bs, ts, NH, HD).transpose(0, 2, 1, 3), pad_t
                                     ).reshape(Bs, KVH, G * ts8, HD)
                    o = _dsa_sample_attn(q_rows, sc, sc_new, thr, pool_k, pool_v, j,
                                         page_table, k.reshape(Bs, ts, dkv), v.reshape(Bs, ts, dkv), tp, tn, far_bias,
                                         KVH, G, HD, ts, ts8, dsa_scale)
                    o = o.reshape(Ms, NH * HD).astype(BF16)
                outs["k"][g].append(k)
                outs["v"][g].append(v)
                outs["ik"][g].append(ki)
            y1, u2 = _mm_postnorm(o, w_o, grp["y"], (mv, 2), ln_g[i, 0], ln_b[i, 0], ((mv, 3), (mv, 4)),
                                  grp["rpg"], alpha, name="attn_out_postnorm")
            hmid = _swiglu_up(u2, ffn_w_gu, i)
            nxt = None
            if i + 1 < depth:
                nmv = modv[i + 1][g]
                nxt = ((nmv, 0), (nmv, 1))
            grp["y"], grp["u"] = _mm_postnorm(hmid, w_down_all, y1, (mv, 5), ln_g[i, 1], ln_b[i, 1], nxt,
                                              grp["rpg"], alpha, layer=i, name="ffn_down_postnorm")

    def stack(key, g, shape):
        return jnp.stack([a.reshape(shape) for a in outs[key][g]])

    yp = groups[0]["y"].reshape(B, T, D)
    ys = groups[1]["y"].reshape(Bs, ts, D)
    return (yp, ys,
            stack("ckv", 0, (B, T, kvl)), stack("kpe", 0, (B, T, rope)),
            stack("k", 0, (B, T, KVH, HD)), stack("v", 0, (B, T, KVH, HD)), stack("ik", 0, (B, T, ID)),
            stack("ckv", 1, (Bs, ts, kvl)), stack("kpe", 1, (Bs, ts, rope)),
            stack("k", 1, (Bs, ts, KVH, HD)), stack("v", 1, (Bs, ts, KVH, HD)), stack("ik", 1, (Bs, ts, ID)))
```
